```python
import math
import jax, jax.numpy as jnp
from jax import lax
import numpy as np

D_MODEL = 1024
BATCH = 4
SEQ = 4096
DEPTH = 4
DEC_BATCH = 32
DEC_SEQ = 8
PAST_LEN = 8192
PAGE_SIZE = 128

CONV_W = 4
RMS_EPS = 1e-6
L2_EPS = 1e-6
N_BRANCH = 3
D_RNN = D_MODEL // 2
LRU_BLOCKS = 8
LRU_BS = D_RNN // LRU_BLOCKS
LRU_C = 8.0
NSA_HEADS = 8
NSA_HD = 64
NSA_KV = 2
NSA_REP = NSA_HEADS // NSA_KV
NSA_WIDTH = NSA_HEADS * NSA_HD
CMP_BLOCK = 32
SLC_BLOCK = 64
N_SEL = 16
WINDOW = 512
Q_BLOCK = 128
ROT_DIM = NSA_HD // 4
ROPE_THETA = 500000.0
DN_HEADS = 4
DN_HD = 128
DN_WIDTH = DN_HEADS * DN_HD
DN_CHUNK = 64

SPLIT_SIZES = (D_RNN, D_RNN, NSA_WIDTH, 2 * N_BRANCH * NSA_KV * NSA_HD, NSA_WIDTH, N_BRANCH * NSA_HEADS,
               3 * DN_WIDTH, DN_WIDTH, DN_HEADS, DN_HEADS, N_BRANCH * D_MODEL)
IN_COLS = (2 * D_RNN + 2 * NSA_WIDTH + 2 * N_BRANCH * NSA_KV * NSA_HD + N_BRANCH * NSA_HEADS
           + 4 * DN_WIDTH + 2 * DN_HEADS + N_BRANCH * D_MODEL)

kernel_name = 'hybrid_rglru_nsa_gdn_step'


def rmsnorm(x, g):
    xf = x.astype(jnp.float32)
    y = xf * lax.rsqrt(jnp.mean(xf * xf, axis=-1, keepdims=True) + RMS_EPS)
    return y * g.astype(jnp.float32)


def l2norm(x):
    return x * lax.rsqrt(jnp.sum(x * x, axis=-1, keepdims=True) + L2_EPS)


def masked_softmax(s, mask):
    s = jnp.where(mask, s.astype(jnp.float32), -1e30)
    p = jax.nn.softmax(s, axis=-1)
    return jnp.where(mask, p, 0.0)


def rope(x, pos):
    half = ROT_DIM // 2
    inv = ROPE_THETA ** (-jnp.arange(half, dtype=jnp.float32) * 2.0 / ROT_DIM)
    ang = pos.astype(jnp.float32)[:, None] * inv[None, :]
    shape = (1, pos.shape[0]) + (1,) * (x.ndim - 3) + (half,)
    c = jnp.cos(ang).reshape(shape)
    s = jnp.sin(ang).reshape(shape)
    xf = x.astype(jnp.float32)
    x1, x2, rest = xf[..., :half], xf[..., half:ROT_DIM], xf[..., ROT_DIM:]
    return jnp.concatenate([x1 * c - x2 * s, x2 * c + x1 * s, rest], axis=-1)


def causal_conv(x, buf, w, b=None):
    T = x.shape[1]
    ext = jnp.concatenate([buf.astype(x.dtype), x], axis=1)
    y = ext[:, 0:T] * w[0]
    for j in range(1, CONV_W):
        y = y + ext[:, j:j + T] * w[j]
    if b is not None:
        y = y + b
    return y, ext[:, -(CONV_W - 1):]


def rg_lru(u, h0, wa, ba, wx, bx, lam):
    f32 = jnp.float32
    B, T, _ = u.shape
    uf = u.astype(f32)
    ub = uf.reshape(B, T, LRU_BLOCKS, LRU_BS)
    r = jax.nn.sigmoid(jnp.einsum('btnc,ncd->btnd', ub, wa.astype(f32)).reshape(B, T, D_RNN) + ba)
    i = jax.nn.sigmoid(jnp.einsum('btnc,ncd->btnd', ub, wx.astype(f32)).reshape(B, T, D_RNN) + bx)
    log_a = LRU_C * r * jax.nn.log_sigmoid(lam.astype(f32))
    a = jnp.exp(log_a)
    b = jnp.sqrt(-jnp.expm1(2.0 * log_a)) * (i * uf)
    b = b.at[:, 0].add(a[:, 0] * h0.astype(f32))

    def combine(left, right):
        a_l, b_l = left
        a_r, b_r = right
        return a_l * a_r, a_r * b_l + b_r

    _, hs = lax.associative_scan(combine, (a, b), axis=1)
    return hs, hs[:, -1]


def nsa_cmp_slc(q, q_pos, kc_blk, vc_blk, ks_t, vs_t, n_top):
    B, Tq = q.shape[:2]
    ncb = kc_blk.shape[1]
    nsb = ks_t.shape[2] // SLC_BLOCK
    s = jnp.einsum('bqgrd,bngd->bqgrn', q, kc_blk)
    cmp_ok = (jnp.arange(ncb) * CMP_BLOCK + CMP_BLOCK - 1)[None, :] <= q_pos[:, None]
    p = masked_softmax(s, cmp_ok[None, :, None, None, :])
    o_cmp = jnp.einsum('bqgrn,bngd->bqgrd', p, vc_blk)
    imp = p.sum(axis=3).reshape(B, Tq, NSA_KV, nsb, SLC_BLOCK // CMP_BLOCK).sum(-1)
    blk = jnp.arange(nsb)[None, :]
    cur = (q_pos // SLC_BLOCK)[:, None]
    forced = ((blk == 0) | (blk == cur))[None, :, None, :]
    future = (blk > cur)[None, :, None, :]
    imp = jnp.where(forced, jnp.inf, jnp.where(future, -jnp.inf, imp))
    _, sel = lax.top_k(imp, n_top)
    n_sel_tok = n_top * SLC_BLOCK
    tok = (sel[..., None] * SLC_BLOCK + jnp.arange(SLC_BLOCK)).reshape(B, Tq, NSA_KV, n_sel_tok)
    idx = jnp.transpose(tok, (0, 2, 1, 3)).reshape(B, NSA_KV, Tq * n_sel_tok)
    bi = jnp.arange(B)[:, None, None]
    gi = jnp.arange(NSA_KV)[None, :, None]
    kg = ks_t[bi, gi, idx].reshape(B, NSA_KV, Tq, n_sel_tok, NSA_HD)
    vg = vs_t[bi, gi, idx].reshape(B, NSA_KV, Tq, n_sel_tok, NSA_HD)
    s = jnp.einsum('bqgrd,bgqsd->bqgrs', q, kg)
    ok = (tok <= q_pos[None, :, None, None])[:, :, :, None, :]
    p = masked_softmax(s, ok)
    o_slc = jnp.einsum('bqgrs,bgqsd->bqgrd', p, vg)
    return o_cmp, o_slc


def window_attn(q, k, v, n_prev):
    f32 = jnp.float32
    B, Tq = q.shape[:2]
    qb = Q_BLOCK if Tq % Q_BLOCK == 0 else Tq
    nqb = Tq // qb
    nback = -(-WINDOW // qb)
    front = nback * qb - n_prev
    kw_len = (nback + 1) * qb
    pad = ((0, 0), (front, 0), (0, 0), (0, 0))
    kb = jnp.pad(k, pad).reshape(B, nback + nqb, qb, NSA_KV, NSA_HD)
    vb = jnp.pad(v, pad).reshape(B, nback + nqb, qb, NSA_KV, NSA_HD)
    band = jnp.arange(nqb)[:, None] + jnp.arange(nback + 1)[None, :]
    kw = kb[:, band].reshape(B, nqb, kw_len, NSA_KV, NSA_HD).astype(f32)
    vw = vb[:, band].reshape(B, nqb, kw_len, NSA_KV, NSA_HD).astype(f32)
    qq = q.reshape(B, nqb, qb, NSA_KV, NSA_REP, NSA_HD)
    s = jnp.einsum('bnqgrd,bnkgd->bnqgrk', qq, kw)
    kidx = jnp.arange(nqb)[:, None, None] * qb + jnp.arange(kw_len)[None, None, :]
    qidx = nback * qb + jnp.arange(nqb)[:, None, None] * qb + jnp.arange(qb)[None, :, None]
    ok = (kidx >= front) & (kidx <= qidx) & (qidx - kidx < WINDOW)
    p = masked_softmax(s, ok[None, :, :, None, None, :])
    o = jnp.einsum('bnqgrk,bnkgd->bnqgrd', p, vw)
    return o.reshape(B, Tq, NSA_KV, NSA_REP, NSA_HD)


def gated_delta(q, k, v, g, beta, S0):
    B, T = q.shape[:2]
    C = DN_CHUNK
    Tp = -(-T // C) * C
    nc = Tp // C

    def chunks(a):
        a = jnp.pad(a, ((0, 0), (0, Tp - T)) + ((0, 0),) * (a.ndim - 2))
        a = a.reshape((B, nc, C) + a.shape[2:])
        return jnp.transpose(a, (1, 0, 3, 2) + tuple(range(4, a.ndim)))

    qc, kc, vc = chunks(q), chunks(k), chunks(v)
    gc, bc = chunks(g), chunks(beta)
    G = jnp.cumsum(gc, axis=-1)
    incl = jnp.tril(jnp.ones((C, C), dtype=bool))
    strict = jnp.tril(jnp.ones((C, C), dtype=bool), -1)
    dmat = jnp.exp(jnp.where(incl, G[..., :, None] - G[..., None, :], -jnp.inf))
    kk = jnp.einsum('nbhcd,nbhed->nbhce', kc, kc)
    A = jnp.where(strict, bc[..., :, None] * kk * dmat, 0.0) + jnp.eye(C, dtype=kk.dtype)
    u = lax.linalg.triangular_solve(A, bc[..., None] * vc, left_side=True, lower=True, unit_diagonal=True)
    w = lax.linalg.triangular_solve(A, bc[..., None] * kc * jnp.exp(G)[..., None],
                                    left_side=True, lower=True, unit_diagonal=True)
    qk = jnp.einsum('nbhcd,nbhed->nbhce', qc, kc) * dmat
    qdec = qc * jnp.exp(G)[..., None]
    kdec = kc * jnp.exp(G[..., -1:] - G)[..., None]
    glast = jnp.exp(G[..., -1])

    def step(S, xs):
        u_, w_, qk_, qdec_, kdec_, gl_ = xs
        vn = u_ - jnp.einsum('bhcd,bhde->bhce', w_, S)
        o = jnp.einsum('bhcd,bhde->bhce', qdec_, S) + jnp.einsum('bhce,bhef->bhcf', qk_, vn)
        S = gl_[..., None, None] * S + jnp.einsum('bhcd,bhce->bhde', kdec_, vn)
        return S, o

    S, o = lax.scan(step, S0, (u, w, qk, qdec, kdec, glast))
    o = jnp.transpose(o, (1, 0, 3, 2, 4)).reshape(B, Tp, DN_HEADS, DN_HD)[:, :T]
    return o, S


def hybrid_layer(x, past_kv, win_prev, lru_h0, lru_buf, dn_S0, dn_buf,
                 norm_g, w_in, lru_conv_w, lru_conv_b, lru_wa, lru_ba, lru_wx, lru_bx, lru_lam,
                 q_norm, k_norm, dn_conv_w, dn_A_log, dn_dt_bias, dn_o_norm,
                 w_lru_out, w_nsa_out, w_dn_out, w_out):
    f32 = jnp.float32
    B, T, _ = x.shape
    P = past_kv.shape[1]
    n_prev = win_prev.shape[1]
    pos = P + jnp.arange(T, dtype=jnp.int32)
    h = rmsnorm(x, norm_g).astype(x.dtype)
    proj = h @ w_in
    points = [int(c) for c in np.cumsum(SPLIT_SIZES)[:-1]]
    (lru_x, lru_z, nsa_q, nsa_kv, nsa_z, nsa_g,
     dn_qkv, dn_z, dn_a, dn_b, merge_g) = jnp.split(proj, points, axis=-1)

    u, lru_buf_new = causal_conv(lru_x, lru_buf, lru_conv_w, lru_conv_b)
    lru_seq, lru_h = rg_lru(u, lru_h0, lru_wa, lru_ba, lru_wx, lru_bx, lru_lam)
    out_lru = (lru_seq.astype(x.dtype) * jax.nn.silu(lru_z)) @ w_lru_out

    q = rope(rmsnorm(nsa_q.reshape(B, T, NSA_HEADS, NSA_HD), q_norm), pos) * (NSA_HD ** -0.5)
    qg = q.reshape(B, T, NSA_KV, NSA_REP, NSA_HD)
    kv = nsa_kv.reshape(B, T, 2 * N_BRANCH, NSA_KV, NSA_HD)
    k = rope(rmsnorm(kv[:, :, 0::2], k_norm[:, None, :]), pos).astype(x.dtype)
    v = kv[:, :, 1::2]
    rows = jnp.stack([k[:, :, 0], v[:, :, 0], k[:, :, 1], v[:, :, 1]], axis=2)
    win_rows = jnp.stack([k[:, :, 2], v[:, :, 2]], axis=2)
    full = jnp.concatenate([past_kv.astype(rows.dtype), rows], axis=1)
    L = P + T
    Lp = -(-L // SLC_BLOCK) * SLC_BLOCK
    full = jnp.pad(full, ((0, 0), (0, Lp - L), (0, 0), (0, 0), (0, 0)))
    kc_blk = full[:, :, 0].reshape(B, Lp // CMP_BLOCK, CMP_BLOCK, NSA_KV, NSA_HD).astype(f32).mean(2)
    vc_blk = full[:, :, 1].reshape(B, Lp // CMP_BLOCK, CMP_BLOCK, NSA_KV, NSA_HD).astype(f32).mean(2)
    ks_t = jnp.transpose(full[:, :, 2], (0, 2, 1, 3))
    vs_t = jnp.transpose(full[:, :, 3], (0, 2, 1, 3))
    n_top = min(N_SEL, Lp // SLC_BLOCK)
    qb = Q_BLOCK if T % Q_BLOCK == 0 else T
    nqb = T // qb
    q_blocks = jnp.swapaxes(qg.reshape(B, nqb, qb, NSA_KV, NSA_REP, NSA_HD), 0, 1)
    p_blocks = pos.reshape(nqb, qb)
    o_cmp, o_slc = lax.map(
        lambda a: nsa_cmp_slc(a[0], a[1], kc_blk, vc_blk, ks_t, vs_t, n_top), (q_blocks, p_blocks))
    o_cmp = jnp.swapaxes(o_cmp, 0, 1).reshape(B, T, NSA_KV, NSA_REP, NSA_HD)
    o_slc = jnp.swapaxes(o_slc, 0, 1).reshape(B, T, NSA_KV, NSA_REP, NSA_HD)
    win_all = jnp.concatenate([win_prev.astype(win_rows.dtype), win_rows], axis=1)
    o_win = window_attn(qg, win_all[:, :, 0], win_all[:, :, 1], n_prev)
    bg = jax.nn.sigmoid(nsa_g.astype(f32)).reshape(B, T, N_BRANCH, NSA_KV, NSA_REP)[..., None]
    o_nsa = (bg[:, :, 0] * o_cmp + bg[:, :, 1] * o_slc + bg[:, :, 2] * o_win).reshape(B, T, NSA_WIDTH)
    out_nsa = (o_nsa.astype(x.dtype) * jax.nn.silu(nsa_z)) @ w_nsa_out
    win_state = win_all[:, -min(WINDOW, n_prev + T):]

    qkv, dn_buf_new = causal_conv(dn_qkv, dn_buf, dn_conv_w)
    qkv = jax.nn.silu(qkv.astype(f32))
    dq, dk, dv = jnp.split(qkv, 3, axis=-1)
    dq = l2norm(dq.reshape(B, T, DN_HEADS, DN_HD)) * (DN_HD ** -0.5)
    dk = l2norm(dk.reshape(B, T, DN_HEADS, DN_HD))
    dv = dv.reshape(B, T, DN_HEADS, DN_HD)
    beta = jax.nn.sigmoid(dn_b.astype(f32))
    g = -jnp.exp(dn_A_log.astype(f32)) * jax.nn.softplus(dn_a.astype(f32) + dn_dt_bias.astype(f32))
    o_dn, dn_S = gated_delta(dq, dk, dv, g, beta, dn_S0.astype(f32))
    o_dn = rmsnorm(o_dn, dn_o_norm).reshape(B, T, DN_WIDTH)
    out_dn = (o_dn.astype(x.dtype) * jax.nn.silu(dn_z)) @ w_dn_out

    mg = jax.nn.sigmoid(merge_g.astype(f32)).reshape(B, T, N_BRANCH, D_MODEL)
    merged = mg[:, :, 0] * out_lru + mg[:, :, 1] * out_nsa + mg[:, :, 2] * out_dn
    y = merged.astype(x.dtype) @ w_out
    return x + y.astype(x.dtype), (rows, win_state, lru_h, lru_buf_new, dn_S, dn_buf_new)


def setup_inputs(seed: int = 0) -> dict:
    key = jax.random.key(seed)
    ks = jax.random.split(key, 32)
    f32 = jnp.float32
    n_pages = PAST_LEN // PAGE_SIZE
    n_used = DEC_BATCH * n_pages
    n_phys = n_used + max(1, n_used // 4)
    win_buf = min(WINDOW, PAST_LEN)

    def nrm(k, shape, scale):
        return scale * jax.random.normal(k, shape, f32)

    x_prompt = nrm(ks[0], (BATCH, SEQ, D_MODEL), 1.0)
    x_sample = nrm(ks[1], (DEC_BATCH, DEC_SEQ, D_MODEL), 1.0)
    cache_nsa_kv = nrm(ks[2], (DEPTH, n_phys, PAGE_SIZE, 4, NSA_KV, NSA_HD), 1.0)
    perm = jax.random.permutation(ks[3], n_phys).astype(jnp.int32)
    page_table = perm[:n_used].reshape(DEC_BATCH, n_pages)
    state_nsa_win = nrm(ks[4], (DEPTH, DEC_BATCH, win_buf, 2, NSA_KV, NSA_HD), 1.0)
    state_lru_h = nrm(ks[5], (DEPTH, DEC_BATCH, D_RNN), 0.5)
    state_lru_conv = nrm(ks[6], (DEPTH, DEC_BATCH, CONV_W - 1, D_RNN), 1.0)
    state_dn_S = nrm(ks[7], (DEPTH, DEC_BATCH, DN_HEADS, DN_HD, DN_HD), 0.1)
    state_dn_conv = nrm(ks[8], (DEPTH, DEC_BATCH, CONV_W - 1, 3 * DN_WIDTH), 1.0)

    norm_gain = 1.0 + nrm(ks[9], (DEPTH, D_MODEL), 0.02)
    w_in = nrm(ks[10], (DEPTH, D_MODEL, IN_COLS), D_MODEL ** -0.5)
    lru_conv_w = nrm(ks[11], (DEPTH, CONV_W, D_RNN), CONV_W ** -0.5)
    lru_conv_b = nrm(ks[12], (DEPTH, D_RNN), 0.01)
    lru_wa = nrm(ks[13], (DEPTH, LRU_BLOCKS, LRU_BS, LRU_BS), LRU_BS ** -0.5)
    lru_ba = nrm(ks[14], (DEPTH, D_RNN), 0.01)
    lru_wx = nrm(ks[15], (DEPTH, LRU_BLOCKS, LRU_BS, LRU_BS), LRU_BS ** -0.5)
    lru_bx = nrm(ks[16], (DEPTH, D_RNN), 0.01)
    a_c = jax.random.uniform(ks[17], (DEPTH, D_RNN), f32, 0.9, 0.999)
    a0 = a_c ** (1.0 / LRU_C)
    lru_lambda = jnp.log(a0) - jnp.log1p(-a0)
    nsa_q_norm = 1.0 + nrm(ks[18], (DEPTH, NSA_HD), 0.02)
    nsa_k_norm = 1.0 + nrm(ks[19], (DEPTH, N_BRANCH, NSA_HD), 0.02)
    dn_conv_w = nrm(ks[20], (DEPTH, CONV_W, 3 * DN_WIDTH), CONV_W ** -0.5)
    dn_A_log = jnp.log(jax.random.uniform(ks[21], (DEPTH, DN_HEADS), f32, 1.0, 16.0))
    dt = jnp.exp(jax.random.uniform(ks[22], (DEPTH, DN_HEADS), f32, math.log(1e-3), math.log(1e-1)))
    dn_dt_bias = dt + jnp.log(-jnp.expm1(-dt))
    dn_o_norm = 1.0 + nrm(ks[23], (DEPTH, DN_HD), 0.02)
    w_lru_out = nrm(ks[24], (DEPTH, D_RNN, D_MODEL), D_RNN ** -0.5)
    w_nsa_out = nrm(ks[25], (DEPTH, NSA_WIDTH, D_MODEL), NSA_WIDTH ** -0.5)
    w_dn_out = nrm(ks[26], (DEPTH, DN_WIDTH, D_MODEL), DN_WIDTH ** -0.5)
    w_out = nrm(ks[27], (DEPTH, D_MODEL, D_MODEL), D_MODEL ** -0.5)
    return {
        'x_prompt': x_prompt, 'x_sample': x_sample,
        'cache_nsa_kv': cache_nsa_kv, 'page_table': page_table,
        'state_nsa_win': state_nsa_win, 'state_lru_h': state_lru_h, 'state_lru_conv': state_lru_conv,
        'state_dn_S': state_dn_S, 'state_dn_conv': state_dn_conv,
        'norm_gain': norm_gain, 'w_in': w_in,
        'lru_conv_w': lru_conv_w, 'lru_conv_b': lru_conv_b, 'lru_wa': lru_wa, 'lru_ba': lru_ba,
        'lru_wx': lru_wx, 'lru_bx': lru_bx, 'lru_lambda': lru_lambda,
        'nsa_q_norm': nsa_q_norm, 'nsa_k_norm': nsa_k_norm,
        'dn_conv_w': dn_conv_w, 'dn_A_log': dn_A_log, 'dn_dt_bias': dn_dt_bias, 'dn_o_norm': dn_o_norm,
        'w_lru_out': w_lru_out, 'w_nsa_out': w_nsa_out, 'w_dn_out': w_dn_out, 'w_out': w_out,
    }


def reference(x_prompt, x_sample, cache_nsa_kv, page_table, state_nsa_win, state_lru_h, state_lru_conv,
              state_dn_S, state_dn_conv, norm_gain, w_in, lru_conv_w, lru_conv_b, lru_wa, lru_ba,
              lru_wx, lru_bx, lru_lambda, nsa_q_norm, nsa_k_norm, dn_conv_w, dn_A_log, dn_dt_bias,
              dn_o_norm, w_lru_out, w_nsa_out, w_dn_out, w_out):
    Bp = x_prompt.shape[0]
    Bd = x_sample.shape[0]
    past_len = page_table.shape[1] * PAGE_SIZE
    dt = x_prompt.dtype
    p_kv0 = jnp.zeros((Bp, 0, 4, NSA_KV, NSA_HD), dt)
    p_win0 = jnp.zeros((Bp, 0, 2, NSA_KV, NSA_HD), dt)
    p_h0 = jnp.zeros((Bp, D_RNN), jnp.float32)
    p_lbuf0 = jnp.zeros((Bp, CONV_W - 1, D_RNN), dt)
    p_S0 = jnp.zeros((Bp, DN_HEADS, DN_HD, DN_HD), jnp.float32)
    p_dbuf0 = jnp.zeros((Bp, CONV_W - 1, 3 * DN_WIDTH), dt)
    weights = (norm_gain, w_in, lru_conv_w, lru_conv_b, lru_wa, lru_ba, lru_wx, lru_bx, lru_lambda,
               nsa_q_norm, nsa_k_norm, dn_conv_w, dn_A_log, dn_dt_bias, dn_o_norm,
               w_lru_out, w_nsa_out, w_dn_out, w_out)
    xp, xs = x_prompt, x_sample
    st_p, st_s = [], []
    for l in range(DEPTH):
        lw = [w[l] for w in weights]
        xp, sp = hybrid_layer(xp, p_kv0, p_win0, p_h0, p_lbuf0, p_S0, p_dbuf0, *lw)
        past = cache_nsa_kv[l][page_table].reshape(Bd, past_len, 4, NSA_KV, NSA_HD)
        xs, ss = hybrid_layer(xs, past, state_nsa_win[l], state_lru_h[l], state_lru_conv[l],
                              state_dn_S[l], state_dn_conv[l], *lw)
        st_p.append(sp)
        st_s.append(ss)
    kv_p, win_p, lh_p, lc_p, S_p, dc_p = [jnp.stack(a) for a in zip(*st_p)]
    kv_s, win_s, lh_s, lc_s, S_s, dc_s = [jnp.stack(a) for a in zip(*st_s)]
    return (xp, xs, kv_p, kv_s, win_p, win_s, lh_p, lh_s, lc_p, lc_s, S_p, S_s, dc_p, dc_s)
```

```python
import functools
import math

import numpy as np
import jax
import jax.numpy as jnp
from jax import lax
from jax.experimental import pallas as pl
from jax.experimental.pallas import tpu as pltpu

D_MODEL = 1024
DEPTH = 4
PAGE_SIZE = 128
CONV_W = 4
RMS_EPS = 1e-6
L2_EPS = 1e-6
N_BRANCH = 3
D_RNN = D_MODEL // 2
LRU_BLOCKS = 8
LRU_BS = D_RNN // LRU_BLOCKS
LRU_C = 8.0
NSA_HEADS = 8
NSA_HD = 64
NSA_KV = 2
NSA_REP = NSA_HEADS // NSA_KV
NSA_WIDTH = NSA_HEADS * NSA_HD
CMP_BLOCK = 32
SLC_BLOCK = 64
N_SEL = 16
WINDOW = 512
Q_BLOCK = 128
ROT_DIM = NSA_HD // 4
ROPE_THETA = 500000.0
DN_HEADS = 4
DN_HD = 128
DN_WIDTH = DN_HEADS * DN_HD
DN_CHUNK = 64

LANE = 128

_SEG_NAMES = ("lru_x", "lru_z", "nsa_q", "nsa_kv", "nsa_z", "nsa_g", "dn_qkv", "dn_z", "dn_a", "dn_b", "merge_g")
_SEG_SIZES = (D_RNN, D_RNN, NSA_WIDTH, 2 * N_BRANCH * NSA_KV * NSA_HD, NSA_WIDTH, N_BRANCH * NSA_HEADS,
              3 * DN_WIDTH, DN_WIDTH, DN_HEADS, DN_HEADS, N_BRANCH * D_MODEL)
_SEG_SRC = dict(zip(_SEG_NAMES, np.concatenate([[0], np.cumsum(_SEG_SIZES)[:-1]]).tolist()))
_SEG_LEN = dict(zip(_SEG_NAMES, _SEG_SIZES))
_DST_SLOTS = (
    (("lru_x",), D_RNN), (("lru_z",), D_RNN), (("nsa_q",), NSA_WIDTH), (("nsa_z",), NSA_WIDTH),
    (("dn_z",), DN_WIDTH), (("nsa_g",), LANE), (("dn_a", "dn_b"), LANE),
    (("nsa_kv",), 2 * N_BRANCH * NSA_KV * NSA_HD), (("dn_qkv",), 3 * DN_WIDTH), (("merge_g",), N_BRANCH * D_MODEL),
)


def _dst_layout():
    off, dst = 0, {}
    for names, width in _DST_SLOTS:
        o = off
        for n in names:
            dst[n] = o
            o += _SEG_LEN[n]
        off += width
    return dst, off


_SEG_DST, IN_COLS_PAD = _dst_layout()


def _pad_in_weight(w_in):
    pieces = []
    for names, width in _DST_SLOTS:
        used = 0
        for n in names:
            pieces.append(w_in[:, :, _SEG_SRC[n]:_SEG_SRC[n] + _SEG_LEN[n]])
            used += _SEG_LEN[n]
        if used < width:
            pieces.append(jnp.zeros(w_in.shape[:2] + (width - used,), w_in.dtype))
    return jnp.concatenate(pieces, axis=-1).astype(jnp.bfloat16)


def _in_proj_kernel(x_ref, g_ref, w_ref, o_ref, h_ref):
    @pl.when(pl.program_id(1) == 0)
    def _():
        x = x_ref[...]
        ms = jnp.mean(x * x, axis=-1, keepdims=True)
        h_ref[...] = (x * lax.rsqrt(ms + RMS_EPS) * g_ref[...]).astype(jnp.bfloat16)

    o_ref[...] = jnp.dot(h_ref[...], w_ref[...], preferred_element_type=jnp.float32)


def _in_proj(x2d, gain, w_pad):
    m = x2d.shape[0]
    tm = min(512, m)
    tn = 1024
    return pl.pallas_call(
        _in_proj_kernel,
        out_shape=jax.ShapeDtypeStruct((m, IN_COLS_PAD), jnp.float32),
        grid=(m // tm, IN_COLS_PAD // tn),
        in_specs=[
            pl.BlockSpec((tm, D_MODEL), lambda i, j: (i, 0)),
            pl.BlockSpec((1, D_MODEL), lambda i, j: (0, 0)),
            pl.BlockSpec((D_MODEL, tn), lambda i, j: (0, j)),
        ],
        out_specs=pl.BlockSpec((tm, tn), lambda i, j: (i, j)),
        scratch_shapes=[pltpu.VMEM((tm, D_MODEL), jnp.bfloat16)],
        compiler_params=pltpu.CompilerParams(dimension_semantics=("parallel", "arbitrary")),
        name="in_proj",
    )(x2d, gain.reshape(1, D_MODEL), w_pad)


def _silu(z):
    return z * jax.nn.sigmoid(z)


def _out_stage_kernel(x_ref, a_lru, z_lru, a_nsa, z_nsa, a_dn, z_dn, mg0, mg1, mg2,
                      w_lru, w_nsa, w_dn, w_out, o_ref):
    def branch(a, z, w):
        y = (a[...] * _silu(z[...])).astype(jnp.bfloat16)
        return jnp.dot(y, w[...], preferred_element_type=jnp.float32)

    merged = jax.nn.sigmoid(mg0[...]) * branch(a_lru, z_lru, w_lru)
    merged = merged + jax.nn.sigmoid(mg1[...]) * branch(a_nsa, z_nsa, w_nsa)
    merged = merged + jax.nn.sigmoid(mg2[...]) * branch(a_dn, z_dn, w_dn)
    y = jnp.dot(merged.astype(jnp.bfloat16), w_out[...], preferred_element_type=jnp.float32)
    o_ref[...] = x_ref[...] + y


def _out_stage(x2d, proj, a_lru, a_nsa, a_dn, w_lru, w_nsa, w_dn, w_out):
    m = x2d.shape[0]
    tm = min(256, m)
    half = D_RNN
    row = lambda i: (i, 0)
    col = lambda c: (lambda i: (i, c))
    full = lambda i: (0, 0)
    mg_blk = _SEG_DST["merge_g"] // D_MODEL
    return pl.pallas_call(
        _out_stage_kernel,
        out_shape=jax.ShapeDtypeStruct((m, D_MODEL), jnp.float32),
        grid=(m // tm,),
        in_specs=[
            pl.BlockSpec((tm, D_MODEL), row),
            pl.BlockSpec((tm, half), row),
            pl.BlockSpec((tm, half), col(_SEG_DST["lru_z"] // half)),
            pl.BlockSpec((tm, half), row),
            pl.BlockSpec((tm, half), col(_SEG_DST["nsa_z"] // half)),
            pl.BlockSpec((tm, half), row),
            pl.BlockSpec((tm, half), col(_SEG_DST["dn_z"] // half)),
            pl.BlockSpec((tm, D_MODEL), col(mg_blk)),
            pl.BlockSpec((tm, D_MODEL), col(mg_blk + 1)),
            pl.BlockSpec((tm, D_MODEL), col(mg_blk + 2)),
            pl.BlockSpec((half, D_MODEL), full),
            pl.BlockSpec((half, D_MODEL), full),
            pl.BlockSpec((half, D_MODEL), full),
            pl.BlockSpec((D_MODEL, D_MODEL), full),
        ],
        out_specs=pl.BlockSpec((tm, D_MODEL), row),
        compiler_params=pltpu.CompilerParams(dimension_semantics=("parallel",)),
        name="out_stage",
    )(x2d, a_lru, proj, a_nsa, proj, a_dn, proj, proj, proj, proj, w_lru, w_nsa, w_dn, w_out)


def _rmsnorm(x, g):
    xf = x.astype(jnp.float32)
    y = xf * lax.rsqrt(jnp.mean(xf * xf, axis=-1, keepdims=True) + RMS_EPS)
    return y * g.astype(jnp.float32)


def _l2norm(x):
    return x * lax.rsqrt(jnp.sum(x * x, axis=-1, keepdims=True) + L2_EPS)


def _masked_softmax(s, mask):
    s = jnp.where(mask, s.astype(jnp.float32), -1e30)
    p = jax.nn.softmax(s, axis=-1)
    return jnp.where(mask, p, 0.0)


def _rope(x, pos):
    half = ROT_DIM // 2
    inv = ROPE_THETA ** (-jnp.arange(half, dtype=jnp.float32) * 2.0 / ROT_DIM)
    ang = pos.astype(jnp.float32)[:, None] * inv[None, :]
    shape = (1, pos.shape[0]) + (1,) * (x.ndim - 3) + (half,)
    c = jnp.cos(ang).reshape(shape)
    s = jnp.sin(ang).reshape(shape)
    xf = x.astype(jnp.float32)
    x1, x2, rest = xf[..., :half], xf[..., half:ROT_DIM], xf[..., ROT_DIM:]
    return jnp.concatenate([x1 * c - x2 * s, x2 * c + x1 * s, rest], axis=-1)


def _causal_conv(x, buf, w, b=None):
    T = x.shape[1]
    ext = jnp.concatenate([buf.astype(x.dtype), x], axis=1)
    y = ext[:, 0:T] * w[0]
    for j in range(1, CONV_W):
        y = y + ext[:, j:j + T] * w[j]
    if b is not None:
        y = y + b
    return y, ext[:, -(CONV_W - 1):]


def _rg_lru(u, h0, wa, ba, wx, bx, lam):
    f32 = jnp.float32
    B, T, _ = u.shape
    uf = u.astype(f32)
    ub = uf.reshape(B, T, LRU_BLOCKS, LRU_BS)
    r = jax.nn.sigmoid(jnp.einsum('btnc,ncd->btnd', ub, wa.astype(f32)).reshape(B, T, D_RNN) + ba)
    i = jax.nn.sigmoid(jnp.einsum('btnc,ncd->btnd', ub, wx.astype(f32)).reshape(B, T, D_RNN) + bx)
    log_a = LRU_C * r * jax.nn.log_sigmoid(lam.astype(f32))
    a = jnp.exp(log_a)
    b = jnp.sqrt(-jnp.expm1(2.0 * log_a)) * (i * uf)
    b = b.at[:, 0].add(a[:, 0] * h0.astype(f32))

    def combine(left, right):
        a_l, b_l = left
        a_r, b_r = right
        return a_l * a_r, a_r * b_l + b_r

    _, hs = lax.associative_scan(combine, (a, b), axis=1)
    return hs, hs[:, -1]


def _nsa_cmp_slc(q, q_pos, kc_blk, vc_blk, ks_t, vs_t, n_top):
    B, Tq = q.shape[:2]
    ncb = kc_blk.shape[1]
    nsb = ks_t.shape[2] // SLC_BLOCK
    s = jnp.einsum('bqgrd,bngd->bqgrn', q, kc_blk)
    cmp_ok = (jnp.arange(ncb) * CMP_BLOCK + CMP_BLOCK - 1)[None, :] <= q_pos[:, None]
    p = _masked_softmax(s, cmp_ok[None, :, None, None, :])
    o_cmp = jnp.einsum('bqgrn,bngd->bqgrd', p, vc_blk)
    imp = p.sum(axis=3).reshape(B, Tq, NSA_KV, nsb, SLC_BLOCK // CMP_BLOCK).sum(-1)
    blk = jnp.arange(nsb)[None, :]
    cur = (q_pos // SLC_BLOCK)[:, None]
    forced = ((blk == 0) | (blk == cur))[None, :, None, :]
    future = (blk > cur)[None, :, None, :]
    imp = jnp.where(forced, jnp.inf, jnp.where(future, -jnp.inf, imp))
    _, sel = lax.top_k(imp, n_top)
    n_sel_tok = n_top * SLC_BLOCK
    tok = (sel[..., None] * SLC_BLOCK + jnp.arange(SLC_BLOCK)).reshape(B, Tq, NSA_KV, n_sel_tok)
    idx = jnp.transpose(tok, (0, 2, 1, 3)).reshape(B, NSA_KV, Tq * n_sel_tok)
    bi = jnp.arange(B)[:, None, None]
    gi = jnp.arange(NSA_KV)[None, :, None]
    kg = ks_t[bi, gi, idx].reshape(B, NSA_KV, Tq, n_sel_tok, NSA_HD)
    vg = vs_t[bi, gi, idx].reshape(B, NSA_KV, Tq, n_sel_tok, NSA_HD)
    s = jnp.einsum('bqgrd,bgqsd->bqgrs', q, kg)
    ok = (tok <= q_pos[None, :, None, None])[:, :, :, None, :]
    p = _masked_softmax(s, ok)
    o_slc = jnp.einsum('bqgrs,bgqsd->bqgrd', p, vg)
    return o_cmp, o_slc


def _window_attn(q, k, v, n_prev):
    f32 = jnp.float32
    B, Tq = q.shape[:2]
    qb = Q_BLOCK if Tq % Q_BLOCK == 0 else Tq
    nqb = Tq // qb
    nback = -(-WINDOW // qb)
    front = nback * qb - n_prev
    kw_len = (nback + 1) * qb
    pad = ((0, 0), (front, 0), (0, 0), (0, 0))
    kb = jnp.pad(k, pad).reshape(B, nback + nqb, qb, NSA_KV, NSA_HD)
    vb = jnp.pad(v, pad).reshape(B, nback + nqb, qb, NSA_KV, NSA_HD)
    band = jnp.arange(nqb)[:, None] + jnp.arange(nback + 1)[None, :]
    kw = kb[:, band].reshape(B, nqb, kw_len, NSA_KV, NSA_HD).astype(f32)
    vw = vb[:, band].reshape(B, nqb, kw_len, NSA_KV, NSA_HD).astype(f32)
    qq = q.reshape(B, nqb, qb, NSA_KV, NSA_REP, NSA_HD)
    s = jnp.einsum('bnqgrd,bnkgd->bnqgrk', qq, kw)
    kidx = jnp.arange(nqb)[:, None, None] * qb + jnp.arange(kw_len)[None, None, :]
    qidx = nback * qb + jnp.arange(nqb)[:, None, None] * qb + jnp.arange(qb)[None, :, None]
    ok = (kidx >= front) & (kidx <= qidx) & (qidx - kidx < WINDOW)
    p = _masked_softmax(s, ok[None, :, :, None, None, :])
    o = jnp.einsum('bnqgrk,bnkgd->bnqgrd', p, vw)
    return o.reshape(B, Tq, NSA_KV, NSA_REP, NSA_HD)


def _gated_delta(q, k, v, g, beta, S0):
    B, T = q.shape[:2]
    C = DN_CHUNK
    Tp = -(-T // C) * C
    nc = Tp // C

    def chunks(a):
        a = jnp.pad(a, ((0, 0), (0, Tp - T)) + ((0, 0),) * (a.ndim - 2))
        a = a.reshape((B, nc, C) + a.shape[2:])
        return jnp.transpose(a, (1, 0, 3, 2) + tuple(range(4, a.ndim)))

    qc, kc, vc = chunks(q), chunks(k), chunks(v)
    gc, bc = chunks(g), chunks(beta)
    G = jnp.cumsum(gc, axis=-1)
    incl = jnp.tril(jnp.ones((C, C), dtype=bool))
    strict = jnp.tril(jnp.ones((C, C), dtype=bool), -1)
    dmat = jnp.exp(jnp.where(incl, G[..., :, None] - G[..., None, :], -jnp.inf))
    kk = jnp.einsum('nbhcd,nbhed->nbhce', kc, kc)
    A = jnp.where(strict, bc[..., :, None] * kk * dmat, 0.0) + jnp.eye(C, dtype=kk.dtype)
    u = lax.linalg.triangular_solve(A, bc[..., None] * vc, left_side=True, lower=True, unit_diagonal=True)
    w = lax.linalg.triangular_solve(A, bc[..., None] * kc * jnp.exp(G)[..., None],
                                    left_side=True, lower=True, unit_diagonal=True)
    qk = jnp.einsum('nbhcd,nbhed->nbhce', qc, kc) * dmat
    qdec = qc * jnp.exp(G)[..., None]
    kdec = kc * jnp.exp(G[..., -1:] - G)[..., None]
    glast = jnp.exp(G[..., -1])

    def step(S, xs):
        u_, w_, qk_, qdec_, kdec_, gl_ = xs
        vn = u_ - jnp.einsum('bhcd,bhde->bhce', w_, S)
        o = jnp.einsum('bhcd,bhde->bhce', qdec_, S) + jnp.einsum('bhce,bhef->bhcf', qk_, vn)
        S = gl_[..., None, None] * S + jnp.einsum('bhcd,bhce->bhde', kdec_, vn)
        return S, o

    S, o = lax.scan(step, S0, (u, w, qk, qdec, kdec, glast))
    o = jnp.transpose(o, (1, 0, 3, 2, 4)).reshape(B, Tp, DN_HEADS, DN_HD)[:, :T]
    return o, S


def _seg(proj, name, B, T):
    s = _SEG_DST[name]
    return proj[:, s:s + _SEG_LEN[name]].reshape(B, T, _SEG_LEN[name])


def _layer(x, past_kv, win_prev, lru_h0, lru_buf, dn_S0, dn_buf,
           norm_g, w_in_pad, lru_conv_w, lru_conv_b, lru_wa, lru_ba, lru_wx, lru_bx, lru_lam,
           q_norm, k_norm, dn_conv_w, dn_A_log, dn_dt_bias, dn_o_norm,
           w_lru_out, w_nsa_out, w_dn_out, w_out):
    f32 = jnp.float32
    B, T, _ = x.shape
    P = past_kv.shape[1]
    n_prev = win_prev.shape[1]
    pos = P + jnp.arange(T, dtype=jnp.int32)
    x2d = x.reshape(B * T, D_MODEL)
    proj = _in_proj(x2d, norm_g, w_in_pad)
    lru_x, nsa_q, nsa_kv, nsa_g, dn_qkv, dn_a, dn_b = (
        _seg(proj, n, B, T) for n in ("lru_x", "nsa_q", "nsa_kv", "nsa_g", "dn_qkv", "dn_a", "dn_b"))

    u, lru_buf_new = _causal_conv(lru_x, lru_buf, lru_conv_w, lru_conv_b)
    lru_seq, lru_h = _rg_lru(u, lru_h0, lru_wa, lru_ba, lru_wx, lru_bx, lru_lam)

    q = _rope(_rmsnorm(nsa_q.reshape(B, T, NSA_HEADS, NSA_HD), q_norm), pos) * (NSA_HD ** -0.5)
    qg = q.reshape(B, T, NSA_KV, NSA_REP, NSA_HD)
    kv = nsa_kv.reshape(B, T, 2 * N_BRANCH, NSA_KV, NSA_HD)
    k = _rope(_rmsnorm(kv[:, :, 0::2], k_norm[:, None, :]), pos).astype(x.dtype)
    v = kv[:, :, 1::2]
    rows = jnp.stack([k[:, :, 0], v[:, :, 0], k[:, :, 1], v[:, :, 1]], axis=2)
    win_rows = jnp.stack([k[:, :, 2], v[:, :, 2]], axis=2)
    full = jnp.concatenate([past_kv.astype(rows.dtype), rows], axis=1)
    L = P + T
    Lp = -(-L // SLC_BLOCK) * SLC_BLOCK
    full = jnp.pad(full, ((0, 0), (0, Lp - L), (0, 0), (0, 0), (0, 0)))
    kc_blk = full[:, :, 0].reshape(B, Lp // CMP_BLOCK, CMP_BLOCK, NSA_KV, NSA_HD).astype(f32).mean(2)
    vc_blk = full[:, :, 1].reshape(B, Lp // CMP_BLOCK, CMP_BLOCK, NSA_KV, NSA_HD).astype(f32).mean(2)
    ks_t = jnp.transpose(full[:, :, 2], (0, 2, 1, 3))
    vs_t = jnp.transpose(full[:, :, 3], (0, 2, 1, 3))
    n_top = min(N_SEL, Lp // SLC_BLOCK)
    qb = Q_BLOCK if T % Q_BLOCK == 0 else T
    nqb = T // qb
    q_blocks = jnp.swapaxes(qg.reshape(B, nqb, qb, NSA_KV, NSA_REP, NSA_HD), 0, 1)
    p_blocks = pos.reshape(nqb, qb)
    o_cmp, o_slc = lax.map(
        lambda a: _nsa_cmp_slc(a[0], a[1], kc_blk, vc_blk, ks_t, vs_t, n_top), (q_blocks, p_blocks))
    o_cmp = jnp.swapaxes(o_cmp, 0, 1).reshape(B, T, NSA_KV, NSA_REP, NSA_HD)
    o_slc = jnp.swapaxes(o_slc, 0, 1).reshape(B, T, NSA_KV, NSA_REP, NSA_HD)
    win_all = jnp.concatenate([win_prev.astype(win_rows.dtype), win_rows], axis=1)
    o_win = _window_attn(qg, win_all[:, :, 0], win_all[:, :, 1], n_prev)
    bg = jax.nn.sigmoid(nsa_g.astype(f32)).reshape(B, T, N_BRANCH, NSA_KV, NSA_REP)[..., None]
    o_nsa = (bg[:, :, 0] * o_cmp + bg[:, :, 1] * o_slc + bg[:, :, 2] * o_win).reshape(B, T, NSA_WIDTH)
    win_state = win_all[:, -min(WINDOW, n_prev + T):]

    qkv, dn_buf_new = _causal_conv(dn_qkv, dn_buf, dn_conv_w)
    qkv = jax.nn.silu(qkv.astype(f32))
    dq, dk, dv = jnp.split(qkv, 3, axis=-1)
    dq = _l2norm(dq.reshape(B, T, DN_HEADS, DN_HD)) * (DN_HD ** -0.5)
    dk = _l2norm(dk.reshape(B, T, DN_HEADS, DN_HD))
    dv = dv.reshape(B, T, DN_HEADS, DN_HD)
    beta = jax.nn.sigmoid(dn_b.astype(f32))
    g = -jnp.exp(dn_A_log.astype(f32)) * jax.nn.softplus(dn_a.astype(f32) + dn_dt_bias.astype(f32))
    o_dn, dn_S = _gated_delta(dq, dk, dv, g, beta, dn_S0.astype(f32))
    o_dn = _rmsnorm(o_dn, dn_o_norm).reshape(B, T, DN_WIDTH)

    y2d = _out_stage(x2d, proj, lru_seq.reshape(B * T, D_RNN), o_nsa.reshape(B * T, NSA_WIDTH),
                     o_dn.reshape(B * T, DN_WIDTH), w_lru_out, w_nsa_out, w_dn_out, w_out)
    return y2d.reshape(B, T, D_MODEL), (rows, win_state, lru_h, lru_buf_new, dn_S, dn_buf_new)


def kernel(x_prompt, x_sample, cache_nsa_kv, page_table, state_nsa_win, state_lru_h, state_lru_conv,
           state_dn_S, state_dn_conv, norm_gain, w_in, lru_conv_w, lru_conv_b, lru_wa, lru_ba,
           lru_wx, lru_bx, lru_lambda, nsa_q_norm, nsa_k_norm, dn_conv_w, dn_A_log, dn_dt_bias,
           dn_o_norm, w_lru_out, w_nsa_out, w_dn_out, w_out):
    Bp = x_prompt.shape[0]
    Bd = x_sample.shape[0]
    past_len = page_table.shape[1] * PAGE_SIZE
    dt = x_prompt.dtype
    bf16 = jnp.bfloat16
    p_kv0 = jnp.zeros((Bp, 0, 4, NSA_KV, NSA_HD), dt)
    p_win0 = jnp.zeros((Bp, 0, 2, NSA_KV, NSA_HD), dt)
    p_h0 = jnp.zeros((Bp, D_RNN), jnp.float32)
    p_lbuf0 = jnp.zeros((Bp, CONV_W - 1, D_RNN), dt)
    p_S0 = jnp.zeros((Bp, DN_HEADS, DN_HD, DN_HD), jnp.float32)
    p_dbuf0 = jnp.zeros((Bp, CONV_W - 1, 3 * DN_WIDTH), dt)
    weights = (norm_gain, _pad_in_weight(w_in), lru_conv_w, lru_conv_b, lru_wa, lru_ba, lru_wx, lru_bx,
               lru_lambda, nsa_q_norm, nsa_k_norm, dn_conv_w, dn_A_log, dn_dt_bias, dn_o_norm,
               w_lru_out.astype(bf16), w_nsa_out.astype(bf16), w_dn_out.astype(bf16), w_out.astype(bf16))
    xp, xs = x_prompt, x_sample
    st_p, st_s = [], []
    for l in range(DEPTH):
        lw = [w[l] for w in weights]
        xp, sp = _layer(xp, p_kv0, p_win0, p_h0, p_lbuf0, p_S0, p_dbuf0, *lw)
        past = cache_nsa_kv[l][page_table].reshape(Bd, past_len, 4, NSA_KV, NSA_HD)
        xs, ss = _layer(xs, past, state_nsa_win[l], state_lru_h[l], state_lru_conv[l],
                        state_dn_S[l], state_dn_conv[l], *lw)
        st_p.append(sp)
        st_s.append(ss)
    kv_p, win_p, lh_p, lc_p, S_p, dc_p = [jnp.stack(a) for a in zip(*st_p)]
    kv_s, win_s, lh_s, lc_s, S_s, dc_s = [jnp.stack(a) for a in zip(*st_s)]
    return (xp, xs, kv_p, kv_s, win_p, win_s, lh_p, lh_s, lc_p, lc_s, S_p, S_s, dc_p, dc_s)
```

```python
import functools
import math

import numpy as np
import jax
import jax.numpy as jnp
from jax import lax
from jax.experimental import pallas as pl
from jax.experimental.pallas import tpu as pltpu

D_MODEL = 1024
DEPTH = 4
PAGE_SIZE = 128
CONV_W = 4
RMS_EPS = 1e-6
L2_EPS = 1e-6
N_BRANCH = 3
D_RNN = D_MODEL // 2
LRU_BLOCKS = 8
LRU_BS = D_RNN // LRU_BLOCKS
LRU_C = 8.0
NSA_HEADS = 8
NSA_HD = 64
NSA_KV = 2
NSA_REP = NSA_HEADS // NSA_KV
NSA_WIDTH = NSA_HEADS * NSA_HD
CMP_BLOCK = 32
SLC_BLOCK = 64
N_SEL = 16
WINDOW = 512
Q_BLOCK = 128
ROT_DIM = NSA_HD // 4
ROPE_THETA = 500000.0
DN_HEADS = 4
DN_HD = 128
DN_WIDTH = DN_HEADS * DN_HD
DN_CHUNK = 64

LANE = 128

_SEG_NAMES = ("lru_x", "lru_z", "nsa_q", "nsa_kv", "nsa_z", "nsa_g", "dn_qkv", "dn_z", "dn_a", "dn_b", "merge_g")
_SEG_SIZES = (D_RNN, D_RNN, NSA_WIDTH, 2 * N_BRANCH * NSA_KV * NSA_HD, NSA_WIDTH, N_BRANCH * NSA_HEADS,
              3 * DN_WIDTH, DN_WIDTH, DN_HEADS, DN_HEADS, N_BRANCH * D_MODEL)
_SEG_SRC = dict(zip(_SEG_NAMES, np.concatenate([[0], np.cumsum(_SEG_SIZES)[:-1]]).tolist()))
_SEG_LEN = dict(zip(_SEG_NAMES, _SEG_SIZES))
_DST_SLOTS = (
    (("lru_x",), D_RNN), (("lru_z",), D_RNN), (("nsa_q",), NSA_WIDTH), (("nsa_z",), NSA_WIDTH),
    (("dn_z",), DN_WIDTH), (("nsa_g",), LANE), (("dn_a", "dn_b"), LANE),
    (("nsa_kv",), 2 * N_BRANCH * NSA_KV * NSA_HD), (("dn_qkv",), 3 * DN_WIDTH), (("merge_g",), N_BRANCH * D_MODEL),
)


def _dst_layout():
    off, dst = 0, {}
    for names, width in _DST_SLOTS:
        o = off
        for n in names:
            dst[n] = o
            o += _SEG_LEN[n]
        off += width
    return dst, off


_SEG_DST, IN_COLS_PAD = _dst_layout()


def _heads_rg(a):
    lead = a.shape[:-1]
    return a.reshape(lead + (NSA_KV, NSA_REP, NSA_HD)).swapaxes(-3, -2).reshape(lead + (NSA_WIDTH,))


def _heads_gr(a):
    lead = a.shape[:-1]
    return a.reshape(lead + (NSA_REP, NSA_KV, NSA_HD)).swapaxes(-3, -2).reshape(lead + (NSA_WIDTH,))


def _pad_in_weight(w_in):
    pieces = []
    for names, width in _DST_SLOTS:
        used = 0
        for n in names:
            piece = w_in[:, :, _SEG_SRC[n]:_SEG_SRC[n] + _SEG_LEN[n]]
            pieces.append(_heads_rg(piece) if n in ("nsa_q", "nsa_z") else piece)
            used += _SEG_LEN[n]
        if used < width:
            pieces.append(jnp.zeros(w_in.shape[:2] + (width - used,), w_in.dtype))
    return jnp.concatenate(pieces, axis=-1).astype(jnp.bfloat16)


def _in_proj_kernel(x_ref, g_ref, w_ref, o_ref, h_ref):
    @pl.when(pl.program_id(1) == 0)
    def _():
        x = x_ref[...]
        ms = jnp.mean(x * x, axis=-1, keepdims=True)
        h_ref[...] = (x * lax.rsqrt(ms + RMS_EPS) * g_ref[...]).astype(jnp.bfloat16)

    o_ref[...] = jnp.dot(h_ref[...], w_ref[...], preferred_element_type=jnp.float32)


def _in_proj(x2d, gain, w_pad):
    m = x2d.shape[0]
    tm = min(512, m)
    tn = 1024
    return pl.pallas_call(
        _in_proj_kernel,
        out_shape=jax.ShapeDtypeStruct((m, IN_COLS_PAD), jnp.float32),
        grid=(m // tm, IN_COLS_PAD // tn),
        in_specs=[
            pl.BlockSpec((tm, D_MODEL), lambda i, j: (i, 0)),
            pl.BlockSpec((1, D_MODEL), lambda i, j: (0, 0)),
            pl.BlockSpec((D_MODEL, tn), lambda i, j: (0, j)),
        ],
        out_specs=pl.BlockSpec((tm, tn), lambda i, j: (i, j)),
        scratch_shapes=[pltpu.VMEM((tm, D_MODEL), jnp.bfloat16)],
        compiler_params=pltpu.CompilerParams(dimension_semantics=("parallel", "arbitrary")),
        name="in_proj",
    )(x2d, gain.reshape(1, D_MODEL), w_pad)


def _silu(z):
    return z * jax.nn.sigmoid(z)


def _out_stage_kernel(x_ref, a_lru, z_lru, a_nsa, z_nsa, a_dn, z_dn, mg0, mg1, mg2,
                      w_lru, w_nsa, w_dn, w_out, o_ref):
    def branch(a, z, w):
        y = (a[...] * _silu(z[...])).astype(jnp.bfloat16)
        return jnp.dot(y, w[...], preferred_element_type=jnp.float32)

    merged = jax.nn.sigmoid(mg0[...]) * branch(a_lru, z_lru, w_lru)
    merged = merged + jax.nn.sigmoid(mg1[...]) * branch(a_nsa, z_nsa, w_nsa)
    merged = merged + jax.nn.sigmoid(mg2[...]) * branch(a_dn, z_dn, w_dn)
    y = jnp.dot(merged.astype(jnp.bfloat16), w_out[...], preferred_element_type=jnp.float32)
    o_ref[...] = x_ref[...] + y


def _out_stage(x2d, proj, a_lru, a_nsa, a_dn, w_lru, w_nsa, w_dn, w_out):
    m = x2d.shape[0]
    tm = min(256, m)
    half = D_RNN
    row = lambda i: (i, 0)
    col = lambda c: (lambda i: (i, c))
    full = lambda i: (0, 0)
    mg_blk = _SEG_DST["merge_g"] // D_MODEL
    return pl.pallas_call(
        _out_stage_kernel,
        out_shape=jax.ShapeDtypeStruct((m, D_MODEL), jnp.float32),
        grid=(m // tm,),
        in_specs=[
            pl.BlockSpec((tm, D_MODEL), row),
            pl.BlockSpec((tm, half), row),
            pl.BlockSpec((tm, half), col(_SEG_DST["lru_z"] // half)),
            pl.BlockSpec((tm, half), row),
            pl.BlockSpec((tm, half), col(_SEG_DST["nsa_z"] // half)),
            pl.BlockSpec((tm, half), row),
            pl.BlockSpec((tm, half), col(_SEG_DST["dn_z"] // half)),
            pl.BlockSpec((tm, D_MODEL), col(mg_blk)),
            pl.BlockSpec((tm, D_MODEL), col(mg_blk + 1)),
            pl.BlockSpec((tm, D_MODEL), col(mg_blk + 2)),
            pl.BlockSpec((half, D_MODEL), full),
            pl.BlockSpec((half, D_MODEL), full),
            pl.BlockSpec((half, D_MODEL), full),
            pl.BlockSpec((D_MODEL, D_MODEL), full),
        ],
        out_specs=pl.BlockSpec((tm, D_MODEL), row),
        compiler_params=pltpu.CompilerParams(dimension_semantics=("parallel",)),
        name="out_stage",
    )(x2d, a_lru, proj, a_nsa, proj, a_dn, proj, proj, proj, proj, w_lru, w_nsa, w_dn, w_out)


NEG_BIG = -1e30
SLC_CHUNK = 512
N_ROWS = NSA_KV * NSA_REP * Q_BLOCK
KV_LANES = NSA_KV * NSA_HD
WIN_KEYS = WINDOW + Q_BLOCK
MAX_SLC_BLOCKS = LANE // 2


def _dot_t(a, b):
    return lax.dot_general(a, b, (((1,), (1,)), ((), ())), preferred_element_type=jnp.float32)


def _select_blocks(imp_t, cur, n_top):
    nb = imp_t.shape[0]
    jj = lax.broadcasted_iota(jnp.int32, imp_t.shape, 0)
    v = jnp.where((jj == 0) | (jj == cur), jnp.inf, jnp.where(jj > cur, -jnp.inf, imp_t))
    sub = lax.broadcasted_iota(jnp.int32, (8, imp_t.shape[1]), 0)
    ranks = []
    for a in range(nb // 8):
        va = v[8 * a:8 * a + 8]
        rank = jnp.zeros(va.shape, jnp.float32)
        for j in range(nb):
            row = v[j:j + 1]
            ge = jnp.where(row >= va, 1.0, 0.0)
            gt = jnp.where(row > va, 1.0, 0.0)
            if j < 8 * a:
                rank = rank + ge
            elif j >= 8 * a + 8:
                rank = rank + gt
            else:
                rank = rank + jnp.where(sub > (j - 8 * a), ge, gt)
        ranks.append(rank)
    rank = jnp.concatenate(ranks, axis=0)
    return jnp.where((rank < n_top) & (jj <= cur), 1.0, 0.0)


def _nsa_prompt_kernel(q_ref, gate_ref, kc_ref, vc_ref, ks_ref, vs_ref, kw_ref, vw_ref, o_ref,
                       kcb_ref, vcb_ref, m_ref, l_ref, acc_ref, *, seq_len, n_top):
    f32, bf16 = jnp.float32, jnp.bfloat16
    i = pl.program_id(1)
    nsb = seq_len // SLC_BLOCK
    half = MAX_SLC_BLOCKS

    @pl.when(i == 0)
    def _():
        if nsb < half:
            kcb_ref[...] = jnp.zeros(kcb_ref.shape, f32)
            vcb_ref[...] = jnp.zeros(vcb_ref.shape, f32)
        for src, dst in ((kc_ref, kcb_ref), (vc_ref, vcb_ref)):
            x = src[...].reshape(nsb, SLC_BLOCK, KV_LANES)
            dst[0:nsb, :] = jnp.sum(x[:, :CMP_BLOCK, :], axis=1) * (1.0 / CMP_BLOCK)
            dst[half:half + nsb, :] = jnp.sum(x[:, CMP_BLOCK:, :], axis=1) * (1.0 / CMP_BLOCK)

    lane = lax.broadcasted_iota(jnp.int32, (Q_BLOCK, LANE), 1)
    tq = lax.broadcasted_iota(jnp.int32, (Q_BLOCK, LANE), 0)
    pos = i * Q_BLOCK + tq
    low = lane < NSA_HD

    q = q_ref[...]
    parts = []
    for g in range(NSA_KV):
        for r in range(NSA_REP):
            qr = q[:, r * KV_LANES:(r + 1) * KV_LANES]
            parts.append(jnp.where(low if g == 0 else ~low, qr, 0.0))
    qpad = jnp.concatenate(parts, axis=0).astype(bf16)

    s = _dot_t(qpad, kcb_ref[...].astype(bf16))
    cblk = jnp.where(lane < half, 2 * lane, 2 * lane - (2 * half - 1))
    okc = (cblk * CMP_BLOCK + (CMP_BLOCK - 1)) <= pos
    s3 = s.reshape(NSA_KV * NSA_REP, Q_BLOCK, LANE) + jnp.where(okc, 0.0, NEG_BIG)[None]
    mx = jnp.max(s3, axis=-1, keepdims=True)
    e = jnp.where(okc[None], jnp.exp(s3 - mx), 0.0)
    den = jnp.sum(e, axis=-1, keepdims=True)
    p3 = e / jnp.where(den > 0.0, den, 1.0)
    o_cmp = jnp.dot(p3.reshape(N_ROWS, LANE).astype(bf16), vcb_ref[...].astype(bf16),
                    preferred_element_type=f32)

    pg = p3.reshape(NSA_KV, NSA_REP, Q_BLOCK, LANE).sum(axis=1)
    cur_t = (i * Q_BLOCK + lax.broadcasted_iota(jnp.int32, (half, Q_BLOCK), 1)) // SLC_BLOCK
    selq = []
    for g in range(NSA_KV):
        imp = pg[g] + pltpu.roll(pg[g], half, axis=1)
        sel_t = _select_blocks(imp.T[:half], cur_t, n_top)
        sel_full = jnp.concatenate([sel_t, jnp.zeros_like(sel_t)], axis=0)
        selq.append(sel_full.T[:, :half].astype(bf16))

    m_ref[...] = jnp.full(m_ref.shape, NEG_BIG, f32)
    l_ref[...] = jnp.zeros(l_ref.shape, f32)
    acc_ref[...] = jnp.zeros(acc_ref.shape, f32)
    heads = (NSA_KV, NSA_REP, Q_BLOCK)

    def slc_step(c, carry):
        start = pl.multiple_of(c * SLC_CHUNK, SLC_CHUNK)
        kch = ks_ref[pl.ds(start, SLC_CHUNK), :]
        vch = vs_ref[pl.ds(start, SLC_CHUNK), :]
        sc = _dot_t(qpad, kch)
        kidx = start + lax.broadcasted_iota(jnp.int32, (Q_BLOCK, SLC_CHUNK), 1)
        causal = kidx <= i * Q_BLOCK + lax.broadcasted_iota(jnp.int32, (Q_BLOCK, SLC_CHUNK), 0)
        kblk = (start + lax.broadcasted_iota(jnp.int32, (half, SLC_CHUNK), 1)) // SLC_BLOCK
        expand = jnp.where(kblk == lax.broadcasted_iota(jnp.int32, (half, SLC_CHUNK), 0), 1.0, 0.0).astype(bf16)
        bias = []
        for g in range(NSA_KV):
            picked = jnp.dot(selq[g], expand, preferred_element_type=f32) > 0.5
            bias.append(jnp.where(picked & causal, 0.0, NEG_BIG))
        bias = jnp.stack(bias, axis=0)[:, None]
        s4 = sc.reshape(heads + (SLC_CHUNK,)) + bias
        m_old = m_ref[...].reshape(heads + (1,))
        m_new = jnp.maximum(m_old, jnp.max(s4, axis=-1, keepdims=True))
        alpha = jnp.exp(m_old - m_new)
        ex = jnp.exp(s4 - m_new)
        l_new = alpha * l_ref[...].reshape(heads + (1,)) + jnp.sum(ex, axis=-1, keepdims=True)
        pv = jnp.dot(ex.reshape(N_ROWS, SLC_CHUNK).astype(bf16), vch, preferred_element_type=f32)
        acc_ref[...] = alpha.reshape(N_ROWS, 1) * acc_ref[...] + pv
        l_ref[...] = l_new.reshape(N_ROWS, 1)
        m_ref[...] = m_new.reshape(N_ROWS, 1)
        return carry

    n_chunks = (i * Q_BLOCK + Q_BLOCK + SLC_CHUNK - 1) // SLC_CHUNK
    lax.fori_loop(0, n_chunks, slc_step, 0)
    o_slc = acc_ref[...] / l_ref[...]

    wstart = pl.multiple_of(jnp.maximum(i - WINDOW // Q_BLOCK, 0) * Q_BLOCK, Q_BLOCK)
    kwin = kw_ref[pl.ds(wstart, WIN_KEYS), :]
    vwin = vw_ref[pl.ds(wstart, WIN_KEYS), :]
    sw = _dot_t(qpad, kwin)
    widx = wstart + lax.broadcasted_iota(jnp.int32, (Q_BLOCK, WIN_KEYS), 1)
    wpos = i * Q_BLOCK + lax.broadcasted_iota(jnp.int32, (Q_BLOCK, WIN_KEYS), 0)
    okw = (widx <= wpos) & (wpos - widx < WINDOW)
    sw3 = sw.reshape(NSA_KV * NSA_REP, Q_BLOCK, WIN_KEYS) + jnp.where(okw, 0.0, NEG_BIG)[None]
    ew = jnp.exp(sw3 - jnp.max(sw3, axis=-1, keepdims=True))
    lw = jnp.sum(ew, axis=-1, keepdims=True).reshape(N_ROWS, 1)
    o_win = jnp.dot(ew.reshape(N_ROWS, WIN_KEYS).astype(bf16), vwin, preferred_element_type=f32) / lw

    gate = jax.nn.sigmoid(gate_ref[...])
    n_hd = NSA_KV * NSA_REP
    for r in range(NSA_REP):
        per_g = []
        for g in range(NSA_KV):
            h = g * NSA_REP + r
            rows = slice(h * Q_BLOCK, (h + 1) * Q_BLOCK)
            per_g.append(gate[:, h:h + 1] * o_cmp[rows]
                         + gate[:, n_hd + h:n_hd + h + 1] * o_slc[rows]
                         + gate[:, 2 * n_hd + h:2 * n_hd + h + 1] * o_win[rows])
        o_ref[:, r * KV_LANES:(r + 1) * KV_LANES] = jnp.where(low, per_g[0], per_g[1])


def _nsa_prompt(q, gate, kc, vc, ks, vs, kw, vw):
    B, T, _ = q.shape
    assert T % SLC_CHUNK == 0 and T >= WIN_KEYS and T // SLC_BLOCK <= MAX_SLC_BLOCKS
    n_top = min(N_SEL, T // SLC_BLOCK)
    blk = lambda b, i: (b, i, 0)
    whole = lambda b, i: (b, 0, 0)
    kv_spec = pl.BlockSpec((None, T, KV_LANES), whole)
    return pl.pallas_call(
        functools.partial(_nsa_prompt_kernel, seq_len=T, n_top=n_top),
        out_shape=jax.ShapeDtypeStruct((B, T, NSA_WIDTH), jnp.float32),
        grid=(B, T // Q_BLOCK),
        in_specs=[
            pl.BlockSpec((None, Q_BLOCK, NSA_WIDTH), blk),
            pl.BlockSpec((None, Q_BLOCK, LANE), blk),
            kv_spec, kv_spec, kv_spec, kv_spec, kv_spec, kv_spec,
        ],
        out_specs=pl.BlockSpec((None, Q_BLOCK, NSA_WIDTH), blk),
        scratch_shapes=[
            pltpu.VMEM((LANE, KV_LANES), jnp.float32),
            pltpu.VMEM((LANE, KV_LANES), jnp.float32),
            pltpu.VMEM((N_ROWS, 1), jnp.float32),
            pltpu.VMEM((N_ROWS, 1), jnp.float32),
            pltpu.VMEM((N_ROWS, KV_LANES), jnp.float32),
        ],
        compiler_params=pltpu.CompilerParams(
            dimension_semantics=("parallel", "arbitrary"), vmem_limit_bytes=48 * 1024 * 1024),
        name="nsa_prompt",
    )(q, gate, kc, vc, ks, vs, kw, vw)


def _rmsnorm(x, g):
    xf = x.astype(jnp.float32)
    y = xf * lax.rsqrt(jnp.mean(xf * xf, axis=-1, keepdims=True) + RMS_EPS)
    return y * g.astype(jnp.float32)


def _l2norm(x):
    return x * lax.rsqrt(jnp.sum(x * x, axis=-1, keepdims=True) + L2_EPS)


def _masked_softmax(s, mask):
    s = jnp.where(mask, s.astype(jnp.float32), -1e30)
    p = jax.nn.softmax(s, axis=-1)
    return jnp.where(mask, p, 0.0)


def _rope(x, pos):
    half = ROT_DIM // 2
    inv = ROPE_THETA ** (-jnp.arange(half, dtype=jnp.float32) * 2.0 / ROT_DIM)
    ang = pos.astype(jnp.float32)[:, None] * inv[None, :]
    shape = (1, pos.shape[0]) + (1,) * (x.ndim - 3) + (half,)
    c = jnp.cos(ang).reshape(shape)
    s = jnp.sin(ang).reshape(shape)
    xf = x.astype(jnp.float32)
    x1, x2, rest = xf[..., :half], xf[..., half:ROT_DIM], xf[..., ROT_DIM:]
    return jnp.concatenate([x1 * c - x2 * s, x2 * c + x1 * s, rest], axis=-1)


def _causal_conv(x, buf, w, b=None):
    T = x.shape[1]
    ext = jnp.concatenate([buf.astype(x.dtype), x], axis=1)
    y = ext[:, 0:T] * w[0]
    for j in range(1, CONV_W):
        y = y + ext[:, j:j + T] * w[j]
    if b is not None:
        y = y + b
    return y, ext[:, -(CONV_W - 1):]


def _rg_lru(u, h0, wa, ba, wx, bx, lam):
    f32 = jnp.float32
    B, T, _ = u.shape
    uf = u.astype(f32)
    ub = uf.reshape(B, T, LRU_BLOCKS, LRU_BS)
    r = jax.nn.sigmoid(jnp.einsum('btnc,ncd->btnd', ub, wa.astype(f32)).reshape(B, T, D_RNN) + ba)
    i = jax.nn.sigmoid(jnp.einsum('btnc,ncd->btnd', ub, wx.astype(f32)).reshape(B, T, D_RNN) + bx)
    log_a = LRU_C * r * jax.nn.log_sigmoid(lam.astype(f32))
    a = jnp.exp(log_a)
    b = jnp.sqrt(-jnp.expm1(2.0 * log_a)) * (i * uf)
    b = b.at[:, 0].add(a[:, 0] * h0.astype(f32))

    def combine(left, right):
        a_l, b_l = left
        a_r, b_r = right
        return a_l * a_r, a_r * b_l + b_r

    _, hs = lax.associative_scan(combine, (a, b), axis=1)
    return hs, hs[:, -1]


def _nsa_cmp_slc(q, q_pos, kc_blk, vc_blk, ks_t, vs_t, n_top):
    B, Tq = q.shape[:2]
    ncb = kc_blk.shape[1]
    nsb = ks_t.shape[2] // SLC_BLOCK
    s = jnp.einsum('bqgrd,bngd->bqgrn', q, kc_blk)
    cmp_ok = (jnp.arange(ncb) * CMP_BLOCK + CMP_BLOCK - 1)[None, :] <= q_pos[:, None]
    p = _masked_softmax(s, cmp_ok[None, :, None, None, :])
    o_cmp = jnp.einsum('bqgrn,bngd->bqgrd', p, vc_blk)
    imp = p.sum(axis=3).reshape(B, Tq, NSA_KV, nsb, SLC_BLOCK // CMP_BLOCK).sum(-1)
    blk = jnp.arange(nsb)[None, :]
    cur = (q_pos // SLC_BLOCK)[:, None]
    forced = ((blk == 0) | (blk == cur))[None, :, None, :]
    future = (blk > cur)[None, :, None, :]
    imp = jnp.where(forced, jnp.inf, jnp.where(future, -jnp.inf, imp))
    _, sel = lax.top_k(imp, n_top)
    n_sel_tok = n_top * SLC_BLOCK
    tok = (sel[..., None] * SLC_BLOCK + jnp.arange(SLC_BLOCK)).reshape(B, Tq, NSA_KV, n_sel_tok)
    idx = jnp.transpose(tok, (0, 2, 1, 3)).reshape(B, NSA_KV, Tq * n_sel_tok)
    bi = jnp.arange(B)[:, None, None]
    gi = jnp.arange(NSA_KV)[None, :, None]
    kg = ks_t[bi, gi, idx].reshape(B, NSA_KV, Tq, n_sel_tok, NSA_HD)
    vg = vs_t[bi, gi, idx].reshape(B, NSA_KV, Tq, n_sel_tok, NSA_HD)
    s = jnp.einsum('bqgrd,bgqsd->bqgrs', q, kg)
    ok = (tok <= q_pos[None, :, None, None])[:, :, :, None, :]
    p = _masked_softmax(s, ok)
    o_slc = jnp.einsum('bqgrs,bgqsd->bqgrd', p, vg)
    return o_cmp, o_slc


def _window_attn(q, k, v, n_prev):
    f32 = jnp.float32
    B, Tq = q.shape[:2]
    qb = Q_BLOCK if Tq % Q_BLOCK == 0 else Tq
    nqb = Tq // qb
    nback = -(-WINDOW // qb)
    front = nback * qb - n_prev
    kw_len = (nback + 1) * qb
    pad = ((0, 0), (front, 0), (0, 0), (0, 0))
    kb = jnp.pad(k, pad).reshape(B, nback + nqb, qb, NSA_KV, NSA_HD)
    vb = jnp.pad(v, pad).reshape(B, nback + nqb, qb, NSA_KV, NSA_HD)
    band = jnp.arange(nqb)[:, None] + jnp.arange(nback + 1)[None, :]
    kw = kb[:, band].reshape(B, nqb, kw_len, NSA_KV, NSA_HD).astype(f32)
    vw = vb[:, band].reshape(B, nqb, kw_len, NSA_KV, NSA_HD).astype(f32)
    qq = q.reshape(B, nqb, qb, NSA_KV, NSA_REP, NSA_HD)
    s = jnp.einsum('bnqgrd,bnkgd->bnqgrk', qq, kw)
    kidx = jnp.arange(nqb)[:, None, None] * qb + jnp.arange(kw_len)[None, None, :]
    qidx = nback * qb + jnp.arange(nqb)[:, None, None] * qb + jnp.arange(qb)[None, :, None]
    ok = (kidx >= front) & (kidx <= qidx) & (qidx - kidx < WINDOW)
    p = _masked_softmax(s, ok[None, :, :, None, None, :])
    o = jnp.einsum('bnqgrk,bnkgd->bnqgrd', p, vw)
    return o.reshape(B, Tq, NSA_KV, NSA_REP, NSA_HD)


def _gated_delta(q, k, v, g, beta, S0):
    B, T = q.shape[:2]
    C = DN_CHUNK
    Tp = -(-T // C) * C
    nc = Tp // C

    def chunks(a):
        a = jnp.pad(a, ((0, 0), (0, Tp - T)) + ((0, 0),) * (a.ndim - 2))
        a = a.reshape((B, nc, C) + a.shape[2:])
        return jnp.transpose(a, (1, 0, 3, 2) + tuple(range(4, a.ndim)))

    qc, kc, vc = chunks(q), chunks(k), chunks(v)
    gc, bc = chunks(g), chunks(beta)
    G = jnp.cumsum(gc, axis=-1)
    incl = jnp.tril(jnp.ones((C, C), dtype=bool))
    strict = jnp.tril(jnp.ones((C, C), dtype=bool), -1)
    dmat = jnp.exp(jnp.where(incl, G[..., :, None] - G[..., None, :], -jnp.inf))
    kk = jnp.einsum('nbhcd,nbhed->nbhce', kc, kc)
    A = jnp.where(strict, bc[..., :, None] * kk * dmat, 0.0) + jnp.eye(C, dtype=kk.dtype)
    u = lax.linalg.triangular_solve(A, bc[..., None] * vc, left_side=True, lower=True, unit_diagonal=True)
    w = lax.linalg.triangular_solve(A, bc[..., None] * kc * jnp.exp(G)[..., None],
                                    left_side=True, lower=True, unit_diagonal=True)
    qk = jnp.einsum('nbhcd,nbhed->nbhce', qc, kc) * dmat
    qdec = qc * jnp.exp(G)[..., None]
    kdec = kc * jnp.exp(G[..., -1:] - G)[..., None]
    glast = jnp.exp(G[..., -1])

    def step(S, xs):
        u_, w_, qk_, qdec_, kdec_, gl_ = xs
        vn = u_ - jnp.einsum('bhcd,bhde->bhce', w_, S)
        o = jnp.einsum('bhcd,bhde->bhce', qdec_, S) + jnp.einsum('bhce,bhef->bhcf', qk_, vn)
        S = gl_[..., None, None] * S + jnp.einsum('bhcd,bhce->bhde', kdec_, vn)
        return S, o

    S, o = lax.scan(step, S0, (u, w, qk, qdec, kdec, glast))
    o = jnp.transpose(o, (1, 0, 3, 2, 4)).reshape(B, Tp, DN_HEADS, DN_HD)[:, :T]
    return o, S


def _nsa_jax(q, pos, rows, win_rows, past_kv, win_prev, nsa_g):
    f32 = jnp.float32
    B, T, _ = q.shape
    P = past_kv.shape[1]
    n_prev = win_prev.shape[1]
    qg = q.reshape(B, T, NSA_KV, NSA_REP, NSA_HD)
    full = jnp.concatenate([past_kv.astype(rows.dtype), rows], axis=1)
    L = P + T
    Lp = -(-L // SLC_BLOCK) * SLC_BLOCK
    full = jnp.pad(full, ((0, 0), (0, Lp - L), (0, 0), (0, 0), (0, 0)))
    kc_blk = full[:, :, 0].reshape(B, Lp // CMP_BLOCK, CMP_BLOCK, NSA_KV, NSA_HD).astype(f32).mean(2)
    vc_blk = full[:, :, 1].reshape(B, Lp // CMP_BLOCK, CMP_BLOCK, NSA_KV, NSA_HD).astype(f32).mean(2)
    ks_t = jnp.transpose(full[:, :, 2], (0, 2, 1, 3))
    vs_t = jnp.transpose(full[:, :, 3], (0, 2, 1, 3))
    n_top = min(N_SEL, Lp // SLC_BLOCK)
    qb = Q_BLOCK if T % Q_BLOCK == 0 else T
    nqb = T // qb
    q_blocks = jnp.swapaxes(qg.reshape(B, nqb, qb, NSA_KV, NSA_REP, NSA_HD), 0, 1)
    p_blocks = pos.reshape(nqb, qb)
    o_cmp, o_slc = lax.map(
        lambda a: _nsa_cmp_slc(a[0], a[1], kc_blk, vc_blk, ks_t, vs_t, n_top), (q_blocks, p_blocks))
    o_cmp = jnp.swapaxes(o_cmp, 0, 1).reshape(B, T, NSA_KV, NSA_REP, NSA_HD)
    o_slc = jnp.swapaxes(o_slc, 0, 1).reshape(B, T, NSA_KV, NSA_REP, NSA_HD)
    win_all = jnp.concatenate([win_prev.astype(win_rows.dtype), win_rows], axis=1)
    o_win = _window_attn(qg, win_all[:, :, 0], win_all[:, :, 1], n_prev)
    bg = jax.nn.sigmoid(nsa_g.astype(f32)).reshape(B, T, N_BRANCH, NSA_KV, NSA_REP)[..., None]
    o_nsa = (bg[:, :, 0] * o_cmp + bg[:, :, 1] * o_slc + bg[:, :, 2] * o_win).reshape(B, T, NSA_WIDTH)
    return o_nsa, win_all[:, -min(WINDOW, n_prev + T):]


def _seg_slot(proj, name, B, T):
    s = _SEG_DST[name]
    return proj[:, s:s + LANE].reshape(B, T, LANE)


def _seg(proj, name, B, T):
    s = _SEG_DST[name]
    return proj[:, s:s + _SEG_LEN[name]].reshape(B, T, _SEG_LEN[name])


def _layer(x, past_kv, win_prev, lru_h0, lru_buf, dn_S0, dn_buf,
           norm_g, w_in_pad, lru_conv_w, lru_conv_b, lru_wa, lru_ba, lru_wx, lru_bx, lru_lam,
           q_norm, k_norm, dn_conv_w, dn_A_log, dn_dt_bias, dn_o_norm,
           w_lru_out, w_nsa_out, w_dn_out, w_out):
    f32 = jnp.float32
    B, T, _ = x.shape
    P = past_kv.shape[1]
    n_prev = win_prev.shape[1]
    pos = P + jnp.arange(T, dtype=jnp.int32)
    x2d = x.reshape(B * T, D_MODEL)
    proj = _in_proj(x2d, norm_g, w_in_pad)
    lru_x, nsa_q, nsa_kv, nsa_g, dn_qkv, dn_a, dn_b = (
        _seg(proj, n, B, T) for n in ("lru_x", "nsa_q", "nsa_kv", "nsa_g", "dn_qkv", "dn_a", "dn_b"))

    u, lru_buf_new = _causal_conv(lru_x, lru_buf, lru_conv_w, lru_conv_b)
    lru_seq, lru_h = _rg_lru(u, lru_h0, lru_wa, lru_ba, lru_wx, lru_bx, lru_lam)

    q = _rope(_rmsnorm(nsa_q.reshape(B, T, NSA_HEADS, NSA_HD), q_norm), pos) * (NSA_HD ** -0.5)
    kv = nsa_kv.reshape(B, T, 2 * N_BRANCH, NSA_KV, NSA_HD)
    k = _rope(_rmsnorm(kv[:, :, 0::2], k_norm[:, None, :]), pos).astype(x.dtype)
    v = kv[:, :, 1::2]
    rows = jnp.stack([k[:, :, 0], v[:, :, 0], k[:, :, 1], v[:, :, 1]], axis=2)
    win_rows = jnp.stack([k[:, :, 2], v[:, :, 2]], axis=2)
    if P == 0 and n_prev == 0:
        flat = lambda a: a.reshape(B, T, KV_LANES)
        bf = lambda a: flat(a).astype(jnp.bfloat16)
        o_nsa = _nsa_prompt(q.reshape(B, T, NSA_WIDTH), _seg_slot(proj, "nsa_g", B, T),
                            flat(k[:, :, 0]), flat(v[:, :, 0]), bf(k[:, :, 1]), bf(v[:, :, 1]),
                            bf(k[:, :, 2]), bf(v[:, :, 2]))
        win_state = win_rows[:, -min(WINDOW, T):]
    else:
        o_nsa, win_state = _nsa_jax(_heads_gr(q.reshape(B, T, NSA_WIDTH)), pos, rows, win_rows, past_kv,
                                    win_prev, nsa_g)
        o_nsa = _heads_rg(o_nsa)

    qkv, dn_buf_new = _causal_conv(dn_qkv, dn_buf, dn_conv_w)
    qkv = jax.nn.silu(qkv.astype(f32))
    dq, dk, dv = jnp.split(qkv, 3, axis=-1)
    dq = _l2norm(dq.reshape(B, T, DN_HEADS, DN_HD)) * (DN_HD ** -0.5)
    dk = _l2norm(dk.reshape(B, T, DN_HEADS, DN_HD))
    dv = dv.reshape(B, T, DN_HEADS, DN_HD)
    beta = jax.nn.sigmoid(dn_b.astype(f32))
    g = -jnp.exp(dn_A_log.astype(f32)) * jax.nn.softplus(dn_a.astype(f32) + dn_dt_bias.astype(f32))
    o_dn, dn_S = _gated_delta(dq, dk, dv, g, beta, dn_S0.astype(f32))
    o_dn = _rmsnorm(o_dn, dn_o_norm).reshape(B, T, DN_WIDTH)

    y2d = _out_stage(x2d, proj, lru_seq.reshape(B * T, D_RNN), o_nsa.reshape(B * T, NSA_WIDTH),
                     o_dn.reshape(B * T, DN_WIDTH), w_lru_out, w_nsa_out, w_dn_out, w_out)
    return y2d.reshape(B, T, D_MODEL), (rows, win_state, lru_h, lru_buf_new, dn_S, dn_buf_new)


def kernel(x_prompt, x_sample, cache_nsa_kv, page_table, state_nsa_win, state_lru_h, state_lru_conv,
           state_dn_S, state_dn_conv, norm_gain, w_in, lru_conv_w, lru_conv_b, lru_wa, lru_ba,
           lru_wx, lru_bx, lru_lambda, nsa_q_norm, nsa_k_norm, dn_conv_w, dn_A_log, dn_dt_bias,
           dn_o_norm, w_lru_out, w_nsa_out, w_dn_out, w_out):
    Bp = x_prompt.shape[0]
    Bd = x_sample.shape[0]
    past_len = page_table.shape[1] * PAGE_SIZE
    dt = x_prompt.dtype
    bf16 = jnp.bfloat16
    p_kv0 = jnp.zeros((Bp, 0, 4, NSA_KV, NSA_HD), dt)
    p_win0 = jnp.zeros((Bp, 0, 2, NSA_KV, NSA_HD), dt)
    p_h0 = jnp.zeros((Bp, D_RNN), jnp.float32)
    p_lbuf0 = jnp.zeros((Bp, CONV_W - 1, D_RNN), dt)
    p_S0 = jnp.zeros((Bp, DN_HEADS, DN_HD, DN_HD), jnp.float32)
    p_dbuf0 = jnp.zeros((Bp, CONV_W - 1, 3 * DN_WIDTH), dt)
    weights = (norm_gain, _pad_in_weight(w_in), lru_conv_w, lru_conv_b, lru_wa, lru_ba, lru_wx, lru_bx,
               lru_lambda, nsa_q_norm, nsa_k_norm, dn_conv_w, dn_A_log, dn_dt_bias, dn_o_norm,
               w_lru_out.astype(bf16), _heads_rg(w_nsa_out.swapaxes(1, 2)).swapaxes(1, 2).astype(bf16),
               w_dn_out.astype(bf16), w_out.astype(bf16))
    xp, xs = x_prompt, x_sample
    st_p, st_s = [], []
    for l in range(DEPTH):
        lw = [w[l] for w in weights]
        xp, sp = _layer(xp, p_kv0, p_win0, p_h0, p_lbuf0, p_S0, p_dbuf0, *lw)
        past = cache_nsa_kv[l][page_table].reshape(Bd, past_len, 4, NSA_KV, NSA_HD)
        xs, ss = _layer(xs, past, state_nsa_win[l], state_lru_h[l], state_lru_conv[l],
                        state_dn_S[l], state_dn_conv[l], *lw)
        st_p.append(sp)
        st_s.append(ss)
    kv_p, win_p, lh_p, lc_p, S_p, dc_p = [jnp.stack(a) for a in zip(*st_p)]
    kv_s, win_s, lh_s, lc_s, S_s, dc_s = [jnp.stack(a) for a in zip(*st_s)]
    return (xp, xs, kv_p, kv_s, win_p, win_s, lh_p, lh_s, lc_p, lc_s, S_p, S_s, dc_p, dc_s)
```

```python
import functools
import math

import numpy as np
import jax
import jax.numpy as jnp
from jax import lax
from jax.experimental import pallas as pl
from jax.experimental.pallas import tpu as pltpu

D_MODEL = 1024
DEPTH = 4
PAGE_SIZE = 128
CONV_W = 4
RMS_EPS = 1e-6
L2_EPS = 1e-6
N_BRANCH = 3
D_RNN = D_MODEL // 2
LRU_BLOCKS = 8
LRU_BS = D_RNN // LRU_BLOCKS
LRU_C = 8.0
NSA_HEADS = 8
NSA_HD = 64
NSA_KV = 2
NSA_REP = NSA_HEADS // NSA_KV
NSA_WIDTH = NSA_HEADS * NSA_HD
CMP_BLOCK = 32
SLC_BLOCK = 64
N_SEL = 16
WINDOW = 512
Q_BLOCK = 128
ROT_DIM = NSA_HD // 4
ROPE_THETA = 500000.0
DN_HEADS = 4
DN_HD = 128
DN_WIDTH = DN_HEADS * DN_HD
DN_CHUNK = 64

LANE = 128

_SEG_NAMES = ("lru_x", "lru_z", "nsa_q", "nsa_kv", "nsa_z", "nsa_g", "dn_qkv", "dn_z", "dn_a", "dn_b", "merge_g")
_SEG_SIZES = (D_RNN, D_RNN, NSA_WIDTH, 2 * N_BRANCH * NSA_KV * NSA_HD, NSA_WIDTH, N_BRANCH * NSA_HEADS,
              3 * DN_WIDTH, DN_WIDTH, DN_HEADS, DN_HEADS, N_BRANCH * D_MODEL)
_SEG_SRC = dict(zip(_SEG_NAMES, np.concatenate([[0], np.cumsum(_SEG_SIZES)[:-1]]).tolist()))
_SEG_LEN = dict(zip(_SEG_NAMES, _SEG_SIZES))
_DST_SLOTS = (
    (("lru_x",), D_RNN), (("lru_z",), D_RNN), (("nsa_q",), NSA_WIDTH), (("nsa_z",), NSA_WIDTH),
    (("dn_z",), DN_WIDTH), (("nsa_g",), LANE), (("dn_a", "dn_b"), LANE),
    (("nsa_kv",), 2 * N_BRANCH * NSA_KV * NSA_HD), (("dn_qkv",), 3 * DN_WIDTH), (("merge_g",), N_BRANCH * D_MODEL),
)


def _dst_layout():
    off, dst = 0, {}
    for names, width in _DST_SLOTS:
        o = off
        for n in names:
            dst[n] = o
            o += _SEG_LEN[n]
        off += width
    return dst, off


_SEG_DST, IN_COLS_PAD = _dst_layout()


def _heads_rg(a):
    lead = a.shape[:-1]
    return a.reshape(lead + (NSA_KV, NSA_REP, NSA_HD)).swapaxes(-3, -2).reshape(lead + (NSA_WIDTH,))


def _pad_in_weight(w_in):
    pieces = []
    for names, width in _DST_SLOTS:
        used = 0
        for n in names:
            piece = w_in[:, :, _SEG_SRC[n]:_SEG_SRC[n] + _SEG_LEN[n]]
            pieces.append(_heads_rg(piece) if n in ("nsa_q", "nsa_z") else piece)
            used += _SEG_LEN[n]
        if used < width:
            pieces.append(jnp.zeros(w_in.shape[:2] + (width - used,), w_in.dtype))
    return jnp.concatenate(pieces, axis=-1).astype(jnp.bfloat16)


def _in_proj_kernel(x_ref, g_ref, w_ref, o_ref, h_ref):
    @pl.when(pl.program_id(1) == 0)
    def _():
        x = x_ref[...]
        ms = jnp.mean(x * x, axis=-1, keepdims=True)
        h_ref[...] = (x * lax.rsqrt(ms + RMS_EPS) * g_ref[...]).astype(jnp.bfloat16)

    o_ref[...] = jnp.dot(h_ref[...], w_ref[...], preferred_element_type=jnp.float32)


def _in_proj(x2d, gain, w_pad):
    m = x2d.shape[0]
    tm = min(512, m)
    tn = 1024
    return pl.pallas_call(
        _in_proj_kernel,
        out_shape=jax.ShapeDtypeStruct((m, IN_COLS_PAD), jnp.float32),
        grid=(m // tm, IN_COLS_PAD // tn),
        in_specs=[
            pl.BlockSpec((tm, D_MODEL), lambda i, j: (i, 0)),
            pl.BlockSpec((1, D_MODEL), lambda i, j: (0, 0)),
            pl.BlockSpec((D_MODEL, tn), lambda i, j: (0, j)),
        ],
        out_specs=pl.BlockSpec((tm, tn), lambda i, j: (i, j)),
        scratch_shapes=[pltpu.VMEM((tm, D_MODEL), jnp.bfloat16)],
        compiler_params=pltpu.CompilerParams(dimension_semantics=("parallel", "arbitrary")),
        name="in_proj",
    )(x2d, gain.reshape(1, D_MODEL), w_pad)


def _silu(z):
    return z * jax.nn.sigmoid(z)


def _out_stage_kernel(x_ref, a_lru, z_lru, a_nsa, z_nsa, a_dn, z_dn, mg0, mg1, mg2,
                      w_lru, w_nsa, w_dn, w_out, o_ref):
    def branch(a, z, w):
        y = (a[...] * _silu(z[...])).astype(jnp.bfloat16)
        return jnp.dot(y, w[...], preferred_element_type=jnp.float32)

    merged = jax.nn.sigmoid(mg0[...]) * branch(a_lru, z_lru, w_lru)
    merged = merged + jax.nn.sigmoid(mg1[...]) * branch(a_nsa, z_nsa, w_nsa)
    merged = merged + jax.nn.sigmoid(mg2[...]) * branch(a_dn, z_dn, w_dn)
    y = jnp.dot(merged.astype(jnp.bfloat16), w_out[...], preferred_element_type=jnp.float32)
    o_ref[...] = x_ref[...] + y


def _out_stage(x2d, proj, a_lru, a_nsa, a_dn, w_lru, w_nsa, w_dn, w_out):
    m = x2d.shape[0]
    tm = min(256, m)
    half = D_RNN
    row = lambda i: (i, 0)
    col = lambda c: (lambda i: (i, c))
    full = lambda i: (0, 0)
    mg_blk = _SEG_DST["merge_g"] // D_MODEL
    return pl.pallas_call(
        _out_stage_kernel,
        out_shape=jax.ShapeDtypeStruct((m, D_MODEL), jnp.float32),
        grid=(m // tm,),
        in_specs=[
            pl.BlockSpec((tm, D_MODEL), row),
            pl.BlockSpec((tm, half), row),
            pl.BlockSpec((tm, half), col(_SEG_DST["lru_z"] // half)),
            pl.BlockSpec((tm, half), row),
            pl.BlockSpec((tm, half), col(_SEG_DST["nsa_z"] // half)),
            pl.BlockSpec((tm, half), row),
            pl.BlockSpec((tm, half), col(_SEG_DST["dn_z"] // half)),
            pl.BlockSpec((tm, D_MODEL), col(mg_blk)),
            pl.BlockSpec((tm, D_MODEL), col(mg_blk + 1)),
            pl.BlockSpec((tm, D_MODEL), col(mg_blk + 2)),
            pl.BlockSpec((half, D_MODEL), full),
            pl.BlockSpec((half, D_MODEL), full),
            pl.BlockSpec((half, D_MODEL), full),
            pl.BlockSpec((D_MODEL, D_MODEL), full),
        ],
        out_specs=pl.BlockSpec((tm, D_MODEL), row),
        compiler_params=pltpu.CompilerParams(dimension_semantics=("parallel",)),
        name="out_stage",
    )(x2d, a_lru, proj, a_nsa, proj, a_dn, proj, proj, proj, proj, w_lru, w_nsa, w_dn, w_out)


NEG_BIG = -1e30
SLC_CHUNK = 512
N_ROWS = NSA_KV * NSA_REP * Q_BLOCK
KV_LANES = NSA_KV * NSA_HD
WIN_KEYS = WINDOW + Q_BLOCK
MAX_SLC_BLOCKS = LANE // 2


def _dot_t(a, b):
    return lax.dot_general(a, b, (((1,), (1,)), ((), ())), preferred_element_type=jnp.float32)


def _select_blocks(imp_t, cur, n_top):
    nb = imp_t.shape[0]
    jj = lax.broadcasted_iota(jnp.int32, imp_t.shape, 0)
    v = jnp.where((jj == 0) | (jj == cur), jnp.inf, jnp.where(jj > cur, -jnp.inf, imp_t))
    sub = lax.broadcasted_iota(jnp.int32, (8, imp_t.shape[1]), 0)
    ranks = []
    for a in range(nb // 8):
        va = v[8 * a:8 * a + 8]
        rank = jnp.zeros(va.shape, jnp.float32)
        for j in range(nb):
            row = v[j:j + 1]
            ge = jnp.where(row >= va, 1.0, 0.0)
            gt = jnp.where(row > va, 1.0, 0.0)
            if j < 8 * a:
                rank = rank + ge
            elif j >= 8 * a + 8:
                rank = rank + gt
            else:
                rank = rank + jnp.where(sub > (j - 8 * a), ge, gt)
        ranks.append(rank)
    rank = jnp.concatenate(ranks, axis=0)
    return jnp.where((rank < n_top) & (jj <= cur), 1.0, 0.0)


def _nsa_prompt_kernel(q_ref, gate_ref, kc_ref, vc_ref, ks_ref, vs_ref, kw_ref, vw_ref, o_ref,
                       kcb_ref, vcb_ref, m_ref, l_ref, acc_ref, *, seq_len, n_top):
    f32, bf16 = jnp.float32, jnp.bfloat16
    i = pl.program_id(1)
    nsb = seq_len // SLC_BLOCK
    half = MAX_SLC_BLOCKS

    @pl.when(i == 0)
    def _():
        if nsb < half:
            kcb_ref[...] = jnp.zeros(kcb_ref.shape, f32)
            vcb_ref[...] = jnp.zeros(vcb_ref.shape, f32)
        for src, dst in ((kc_ref, kcb_ref), (vc_ref, vcb_ref)):
            x = src[...].reshape(nsb, SLC_BLOCK, KV_LANES)
            dst[0:nsb, :] = jnp.sum(x[:, :CMP_BLOCK, :], axis=1) * (1.0 / CMP_BLOCK)
            dst[half:half + nsb, :] = jnp.sum(x[:, CMP_BLOCK:, :], axis=1) * (1.0 / CMP_BLOCK)

    lane = lax.broadcasted_iota(jnp.int32, (Q_BLOCK, LANE), 1)
    tq = lax.broadcasted_iota(jnp.int32, (Q_BLOCK, LANE), 0)
    pos = i * Q_BLOCK + tq
    low = lane < NSA_HD

    q = q_ref[...]
    parts = []
    for g in range(NSA_KV):
        for r in range(NSA_REP):
            qr = q[:, r * KV_LANES:(r + 1) * KV_LANES]
            parts.append(jnp.where(low if g == 0 else ~low, qr, 0.0))
    qpad = jnp.concatenate(parts, axis=0).astype(bf16)

    s = _dot_t(qpad, kcb_ref[...].astype(bf16))
    cblk = jnp.where(lane < half, 2 * lane, 2 * lane - (2 * half - 1))
    okc = (cblk * CMP_BLOCK + (CMP_BLOCK - 1)) <= pos
    s3 = s.reshape(NSA_KV * NSA_REP, Q_BLOCK, LANE) + jnp.where(okc, 0.0, NEG_BIG)[None]
    mx = jnp.max(s3, axis=-1, keepdims=True)
    e = jnp.where(okc[None], jnp.exp(s3 - mx), 0.0)
    den = jnp.sum(e, axis=-1, keepdims=True)
    p3 = e / jnp.where(den > 0.0, den, 1.0)
    o_cmp = jnp.dot(p3.reshape(N_ROWS, LANE).astype(bf16), vcb_ref[...].astype(bf16),
                    preferred_element_type=f32)

    pg = p3.reshape(NSA_KV, NSA_REP, Q_BLOCK, LANE).sum(axis=1)
    cur_t = (i * Q_BLOCK + lax.broadcasted_iota(jnp.int32, (half, Q_BLOCK), 1)) // SLC_BLOCK
    selq = []
    for g in range(NSA_KV):
        imp = pg[g] + pltpu.roll(pg[g], half, axis=1)
        sel_t = _select_blocks(imp.T[:half], cur_t, n_top)
        sel_full = jnp.concatenate([sel_t, jnp.zeros_like(sel_t)], axis=0)
        selq.append(sel_full.T[:, :half].astype(bf16))

    m_ref[...] = jnp.full(m_ref.shape, NEG_BIG, f32)
    l_ref[...] = jnp.zeros(l_ref.shape, f32)
    acc_ref[...] = jnp.zeros(acc_ref.shape, f32)
    heads = (NSA_KV, NSA_REP, Q_BLOCK)

    def slc_step(c, carry):
        start = pl.multiple_of(c * SLC_CHUNK, SLC_CHUNK)
        kch = ks_ref[pl.ds(start, SLC_CHUNK), :]
        vch = vs_ref[pl.ds(start, SLC_CHUNK), :]
        sc = _dot_t(qpad, kch)
        kidx = start + lax.broadcasted_iota(jnp.int32, (Q_BLOCK, SLC_CHUNK), 1)
        causal = kidx <= i * Q_BLOCK + lax.broadcasted_iota(jnp.int32, (Q_BLOCK, SLC_CHUNK), 0)
        kblk = (start + lax.broadcasted_iota(jnp.int32, (half, SLC_CHUNK), 1)) // SLC_BLOCK
        expand = jnp.where(kblk == lax.broadcasted_iota(jnp.int32, (half, SLC_CHUNK), 0), 1.0, 0.0).astype(bf16)
        bias = []
        for g in range(NSA_KV):
            picked = jnp.dot(selq[g], expand, preferred_element_type=f32) > 0.5
            bias.append(jnp.where(picked & causal, 0.0, NEG_BIG))
        bias = jnp.stack(bias, axis=0)[:, None]
        s4 = sc.reshape(heads + (SLC_CHUNK,)) + bias
        m_old = m_ref[...].reshape(heads + (1,))
        m_new = jnp.maximum(m_old, jnp.max(s4, axis=-1, keepdims=True))
        alpha = jnp.exp(m_old - m_new)
        ex = jnp.exp(s4 - m_new)
        l_new = alpha * l_ref[...].reshape(heads + (1,)) + jnp.sum(ex, axis=-1, keepdims=True)
        pv = jnp.dot(ex.reshape(N_ROWS, SLC_CHUNK).astype(bf16), vch, preferred_element_type=f32)
        acc_ref[...] = alpha.reshape(N_ROWS, 1) * acc_ref[...] + pv
        l_ref[...] = l_new.reshape(N_ROWS, 1)
        m_ref[...] = m_new.reshape(N_ROWS, 1)
        return carry

    n_chunks = (i * Q_BLOCK + Q_BLOCK + SLC_CHUNK - 1) // SLC_CHUNK
    lax.fori_loop(0, n_chunks, slc_step, 0)
    o_slc = acc_ref[...] / l_ref[...]

    wstart = pl.multiple_of(jnp.maximum(i - WINDOW // Q_BLOCK, 0) * Q_BLOCK, Q_BLOCK)
    kwin = kw_ref[pl.ds(wstart, WIN_KEYS), :]
    vwin = vw_ref[pl.ds(wstart, WIN_KEYS), :]
    sw = _dot_t(qpad, kwin)
    widx = wstart + lax.broadcasted_iota(jnp.int32, (Q_BLOCK, WIN_KEYS), 1)
    wpos = i * Q_BLOCK + lax.broadcasted_iota(jnp.int32, (Q_BLOCK, WIN_KEYS), 0)
    okw = (widx <= wpos) & (wpos - widx < WINDOW)
    sw3 = sw.reshape(NSA_KV * NSA_REP, Q_BLOCK, WIN_KEYS) + jnp.where(okw, 0.0, NEG_BIG)[None]
    ew = jnp.exp(sw3 - jnp.max(sw3, axis=-1, keepdims=True))
    lw = jnp.sum(ew, axis=-1, keepdims=True).reshape(N_ROWS, 1)
    o_win = jnp.dot(ew.reshape(N_ROWS, WIN_KEYS).astype(bf16), vwin, preferred_element_type=f32) / lw

    gate = jax.nn.sigmoid(gate_ref[...])
    n_hd = NSA_KV * NSA_REP
    for r in range(NSA_REP):
        per_g = []
        for g in range(NSA_KV):
            h = g * NSA_REP + r
            rows = slice(h * Q_BLOCK, (h + 1) * Q_BLOCK)
            per_g.append(gate[:, h:h + 1] * o_cmp[rows]
                         + gate[:, n_hd + h:n_hd + h + 1] * o_slc[rows]
                         + gate[:, 2 * n_hd + h:2 * n_hd + h + 1] * o_win[rows])
        o_ref[:, r * KV_LANES:(r + 1) * KV_LANES] = jnp.where(low, per_g[0], per_g[1])


def _nsa_prompt(q, gate, kc, vc, ks, vs, kw, vw):
    B, T, _ = q.shape
    assert T % SLC_CHUNK == 0 and T >= WIN_KEYS and T // SLC_BLOCK <= MAX_SLC_BLOCKS
    n_top = min(N_SEL, T // SLC_BLOCK)
    blk = lambda b, i: (b, i, 0)
    whole = lambda b, i: (b, 0, 0)
    kv_spec = pl.BlockSpec((None, T, KV_LANES), whole)
    return pl.pallas_call(
        functools.partial(_nsa_prompt_kernel, seq_len=T, n_top=n_top),
        out_shape=jax.ShapeDtypeStruct((B, T, NSA_WIDTH), jnp.float32),
        grid=(B, T // Q_BLOCK),
        in_specs=[
            pl.BlockSpec((None, Q_BLOCK, NSA_WIDTH), blk),
            pl.BlockSpec((None, Q_BLOCK, LANE), blk),
            kv_spec, kv_spec, kv_spec, kv_spec, kv_spec, kv_spec,
        ],
        out_specs=pl.BlockSpec((None, Q_BLOCK, NSA_WIDTH), blk),
        scratch_shapes=[
            pltpu.VMEM((LANE, KV_LANES), jnp.float32),
            pltpu.VMEM((LANE, KV_LANES), jnp.float32),
            pltpu.VMEM((N_ROWS, 1), jnp.float32),
            pltpu.VMEM((N_ROWS, 1), jnp.float32),
            pltpu.VMEM((N_ROWS, KV_LANES), jnp.float32),
        ],
        compiler_params=pltpu.CompilerParams(
            dimension_semantics=("parallel", "arbitrary"), vmem_limit_bytes=48 * 1024 * 1024),
        name="nsa_prompt",
    )(q, gate, kc, vc, ks, vs, kw, vw)


PAGES_PER_STEP = 4
NEW_PAD = 128


def _qpad_rows(q, nq):
    low = lax.broadcasted_iota(jnp.int32, (nq, LANE), 1) < NSA_HD
    parts = []
    for g in range(NSA_KV):
        for r in range(NSA_REP):
            qr = q[:, r * KV_LANES:(r + 1) * KV_LANES]
            parts.append(jnp.where(low if g == 0 else ~low, qr, 0.0))
    return jnp.concatenate(parts, axis=0).astype(jnp.bfloat16)


def _nsa_sample_kernel(pt_ref, *refs, n_q, past_len, n_top):
    f32, bf16 = jnp.float32, jnp.bfloat16
    pages = refs[:PAGES_PER_STEP]
    (q_ref, gate_ref, ksn_ref, vsn_ref, win_ref, kwn_ref, vwn_ref, o_ref,
     kcb_ref, vcb_ref, sel_ref, ocmp_ref, m_ref, l_ref, acc_ref) = refs[PAGES_PER_STEP:]
    ph = pl.program_id(1)
    p = pl.program_id(2)
    last = pl.num_programs(2) - 1
    n_rows = NSA_KV * NSA_REP * n_q
    n_blk = past_len // SLC_BLOCK
    cmp_per_step = PAGES_PER_STEP * PAGE_SIZE // CMP_BLOCK
    keys_per_step = PAGES_PER_STEP * PAGE_SIZE
    heads = (NSA_KV, NSA_REP, n_q)
    qpad = _qpad_rows(q_ref[...], n_q)

    @pl.when(ph == 0)
    def _():
        means = [pg[...].reshape(PAGE_SIZE // CMP_BLOCK, CMP_BLOCK, 2 * KV_LANES).sum(axis=1) * (1.0 / CMP_BLOCK)
                 for pg in pages]
        means = jnp.concatenate(means, axis=0)
        row0 = pl.multiple_of(p * cmp_per_step, cmp_per_step)
        kcb_ref[pl.ds(row0, cmp_per_step), :] = means[:, :KV_LANES]
        vcb_ref[pl.ds(row0, cmp_per_step), :] = means[:, KV_LANES:]

    @pl.when((ph == 0) & (p == last))
    def _():
        halves = []
        for par in range(2):
            kc = kcb_ref[pl.ds(par, n_blk, stride=2), :].astype(bf16)
            halves.append(_dot_t(qpad, kc))
        mx = jnp.maximum(jnp.max(halves[0], axis=-1, keepdims=True), jnp.max(halves[1], axis=-1, keepdims=True))
        e0, e1 = jnp.exp(halves[0] - mx), jnp.exp(halves[1] - mx)
        inv = 1.0 / (jnp.sum(e0, axis=-1, keepdims=True) + jnp.sum(e1, axis=-1, keepdims=True))
        p0, p1 = e0 * inv, e1 * inv
        oc = jnp.zeros((n_rows, KV_LANES), f32)
        for par, pp in ((0, p0), (1, p1)):
            vc = vcb_ref[pl.ds(par, n_blk, stride=2), :].astype(bf16)
            oc = oc + jnp.dot(pp.astype(bf16), vc, preferred_element_type=f32)
        ocmp_ref[...] = oc
        imp = (p0 + p1).reshape(heads + (n_blk,)).sum(axis=1).reshape(NSA_KV * n_q, n_blk)
        lane = lax.broadcasted_iota(jnp.int32, imp.shape, 1)
        v = jnp.where(lane == 0, jnp.inf, imp)
        rank = jnp.zeros(imp.shape, f32)
        for j in range(n_blk):
            col = v[:, j:j + 1]
            tie = jnp.where(lane > j, 1.0, 0.0)
            rank = rank + jnp.where(col > v, 1.0, jnp.where(col == v, tie, 0.0))
        sel_ref[...] = jnp.where(rank < n_top - 1, 1.0, 0.0)
        m_ref[...] = jnp.full(m_ref.shape, NEG_BIG, f32)
        l_ref[...] = jnp.zeros(l_ref.shape, f32)
        acc_ref[...] = jnp.zeros(acc_ref.shape, f32)

    def online(s4, vals):
        m_old = m_ref[...].reshape(heads + (1,))
        m_new = jnp.maximum(m_old, jnp.max(s4, axis=-1, keepdims=True))
        alpha = jnp.exp(m_old - m_new)
        ex = jnp.exp(s4 - m_new)
        l_new = alpha * l_ref[...].reshape(heads + (1,)) + jnp.sum(ex, axis=-1, keepdims=True)
        pv = jnp.dot(ex.reshape(n_rows, s4.shape[-1]).astype(bf16), vals, preferred_element_type=f32)
        acc_ref[...] = alpha.reshape(n_rows, 1) * acc_ref[...] + pv
        l_ref[...] = l_new.reshape(n_rows, 1)
        m_ref[...] = m_new.reshape(n_rows, 1)

    @pl.when(ph == 1)
    def _():
        xs = [pg[...] for pg in pages]
        sc = jnp.concatenate([_dot_t(qpad, x[:, :KV_LANES].astype(bf16)) for x in xs], axis=-1)
        vals = jnp.concatenate([x[:, KV_LANES:].astype(bf16) for x in xs], axis=0)
        kblk = (p * keys_per_step + lax.broadcasted_iota(jnp.int32, (n_blk, keys_per_step), 1)) // SLC_BLOCK
        expand = jnp.where(kblk == lax.broadcasted_iota(jnp.int32, (n_blk, keys_per_step), 0), 1.0, 0.0).astype(bf16)
        picked = jnp.dot(sel_ref[...].astype(bf16), expand, preferred_element_type=f32) > 0.5
        bias = jnp.where(picked, 0.0, NEG_BIG).reshape(NSA_KV, 1, n_q, keys_per_step)
        online(sc.reshape(heads + (keys_per_step,)) + bias, vals)

    @pl.when((ph == 1) & (p == last))
    def _():
        tq = lax.broadcasted_iota(jnp.int32, (n_q, NEW_PAD), 0)
        tk = lax.broadcasted_iota(jnp.int32, (n_q, NEW_PAD), 1)
        new_bias = jnp.where(tk <= tq, 0.0, NEG_BIG)[None, None]
        s_new = _dot_t(qpad, ksn_ref[...]).reshape(heads + (NEW_PAD,)) + new_bias
        online(s_new, vsn_ref[...])
        o_slc = acc_ref[...] / l_ref[...]

        win = win_ref[...]
        n_prev = win.shape[0]
        s_prev = _dot_t(qpad, win[:, :KV_LANES].astype(bf16))
        pk = lax.broadcasted_iota(jnp.int32, (n_q, n_prev), 1)
        pq = lax.broadcasted_iota(jnp.int32, (n_q, n_prev), 0)
        s_prev = s_prev.reshape(heads + (n_prev,)) + jnp.where(pk > pq + (n_prev - WINDOW), 0.0, NEG_BIG)[None, None]
        s_wnew = _dot_t(qpad, kwn_ref[...]).reshape(heads + (NEW_PAD,)) + new_bias
        mw = jnp.maximum(jnp.max(s_prev, axis=-1, keepdims=True), jnp.max(s_wnew, axis=-1, keepdims=True))
        e_prev, e_new = jnp.exp(s_prev - mw), jnp.exp(s_wnew - mw)
        lw = jnp.sum(e_prev, axis=-1, keepdims=True) + jnp.sum(e_new, axis=-1, keepdims=True)
        o_win = (jnp.dot(e_prev.reshape(n_rows, n_prev).astype(bf16), win[:, KV_LANES:].astype(bf16),
                         preferred_element_type=f32)
                 + jnp.dot(e_new.reshape(n_rows, NEW_PAD).astype(bf16), vwn_ref[...], preferred_element_type=f32))
        o_win = o_win / lw.reshape(n_rows, 1)

        gate = jax.nn.sigmoid(gate_ref[...])
        low = lax.broadcasted_iota(jnp.int32, (n_q, LANE), 1) < NSA_HD
        o_cmp = ocmp_ref[...]
        n_hd = NSA_KV * NSA_REP
        for r in range(NSA_REP):
            per_g = []
            for g in range(NSA_KV):
                h = g * NSA_REP + r
                rows = slice(h * n_q, (h + 1) * n_q)
                per_g.append(gate[:, h:h + 1] * o_cmp[rows]
                             + gate[:, n_hd + h:n_hd + h + 1] * o_slc[rows]
                             + gate[:, 2 * n_hd + h:2 * n_hd + h + 1] * o_win[rows])
            o_ref[:, r * KV_LANES:(r + 1) * KV_LANES] = jnp.where(low, per_g[0], per_g[1])


def _nsa_sample(page_table, cache, layer, q, gate, ks_new, vs_new, win_prev, kw_new, vw_new):
    B, T, _ = q.shape
    n_pages = page_table.shape[1]
    past_len = n_pages * PAGE_SIZE
    n_prev = win_prev.shape[1]
    n_blk = past_len // SLC_BLOCK
    assert T % 8 == 0 and T < CMP_BLOCK and n_pages % PAGES_PER_STEP == 0 and n_blk % LANE == 0
    assert n_blk <= LANE and n_prev == WINDOW
    n_top = min(N_SEL, n_blk + 1)
    n_rows = NSA_KV * NSA_REP * T
    per_b = lambda b, ph, p, pt: (b, 0, 0)

    def page_spec(k):
        return pl.BlockSpec((None, None, PAGE_SIZE, 2 * KV_LANES),
                            lambda b, ph, p, pt: (layer, pt[b, p * PAGES_PER_STEP + k], 0, ph))

    new_spec = pl.BlockSpec((None, NEW_PAD, KV_LANES), per_b)
    grid_spec = pltpu.PrefetchScalarGridSpec(
        num_scalar_prefetch=1,
        grid=(B, 2, n_pages // PAGES_PER_STEP),
        in_specs=[page_spec(k) for k in range(PAGES_PER_STEP)] + [
            pl.BlockSpec((None, T, NSA_WIDTH), per_b),
            pl.BlockSpec((None, T, LANE), per_b),
            new_spec, new_spec,
            pl.BlockSpec((None, n_prev, 2 * KV_LANES), per_b),
            new_spec, new_spec,
        ],
        out_specs=pl.BlockSpec((None, T, NSA_WIDTH), per_b),
        scratch_shapes=[
            pltpu.VMEM((past_len // CMP_BLOCK, KV_LANES), jnp.float32),
            pltpu.VMEM((past_len // CMP_BLOCK, KV_LANES), jnp.float32),
            pltpu.VMEM((NSA_KV * T, n_blk), jnp.float32),
            pltpu.VMEM((n_rows, KV_LANES), jnp.float32),
            pltpu.VMEM((n_rows, 1), jnp.float32),
            pltpu.VMEM((n_rows, 1), jnp.float32),
            pltpu.VMEM((n_rows, KV_LANES), jnp.float32),
        ],
    )
    return pl.pallas_call(
        functools.partial(_nsa_sample_kernel, n_q=T, past_len=past_len, n_top=n_top),
        out_shape=jax.ShapeDtypeStruct((B, T, NSA_WIDTH), jnp.float32),
        grid_spec=grid_spec,
        compiler_params=pltpu.CompilerParams(dimension_semantics=("parallel", "arbitrary", "arbitrary")),
        name="nsa_sample",
    )(page_table, *([cache] * PAGES_PER_STEP), q, gate, ks_new, vs_new, win_prev, kw_new, vw_new)


LRU_TILE = 512
SUBLANES = 8


EXPM1_SERIES_RANGE = 0.35
_EXPM1_COEFFS = tuple(1.0 / math.factorial(n) for n in range(9, 1, -1))


def _one_minus_exp(y):
    acc = jnp.full_like(y, _EXPM1_COEFFS[0])
    for c in _EXPM1_COEFFS[1:]:
        acc = acc * y + c
    series = -(y * (1.0 + y * acc))
    return jnp.where(y > -EXPM1_SERIES_RANGE, series, 1.0 - jnp.exp(y))


def _lru_kernel(x_ref, buf_ref, h0_ref, cw_ref, cb_ref, wa_ref, ba_ref, wx_ref, bx_ref, lsl_ref,
                hs_ref, hlast_ref, cbuf_ref, xe_ref, h_ref, *, tile):
    f32 = jnp.float32
    t = pl.program_id(1)
    tail = CONV_W - 1

    @pl.when(t == 0)
    def _():
        xe_ref[0:SUBLANES, :] = jnp.zeros((SUBLANES, D_RNN), f32)
        xe_ref[SUBLANES - tail:SUBLANES, :] = buf_ref[...]
        h_ref[...] = h0_ref[...]

    xe_ref[SUBLANES:SUBLANES + tile, :] = x_ref[...]
    cw = cw_ref[...]
    u = cb_ref[...]
    for j in range(CONV_W):
        u = u + cw[j:j + 1] * xe_ref[SUBLANES - tail + j:SUBLANES - tail + j + tile, :]

    @pl.when(t == pl.num_programs(1) - 1)
    def _():
        cbuf_ref[...] = xe_ref[SUBLANES + tile - tail:SUBLANES + tile, :]

    xe_ref[0:SUBLANES, :] = xe_ref[tile:tile + SUBLANES, :]

    ub = u.astype(jnp.bfloat16)
    r = jax.nn.sigmoid(jnp.dot(ub, wa_ref[...], preferred_element_type=f32) + ba_ref[...])
    i = jax.nn.sigmoid(jnp.dot(ub, wx_ref[...], preferred_element_type=f32) + bx_ref[...])
    log_a = LRU_C * r * lsl_ref[...]
    a = jnp.exp(log_a)
    b = jnp.sqrt(_one_minus_exp(2.0 * log_a)) * (i * u)

    row = lax.broadcasted_iota(jnp.int32, (SUBLANES, D_RNN), 0)
    h = h_ref[...]
    for k in range(tile // SUBLANES):
        ak = a[k * SUBLANES:(k + 1) * SUBLANES]
        bk = b[k * SUBLANES:(k + 1) * SUBLANES]
        for s in (1, 2, 4):
            a_prev = jnp.where(row >= s, pltpu.roll(ak, s, axis=0), 1.0)
            b_prev = jnp.where(row >= s, pltpu.roll(bk, s, axis=0), 0.0)
            bk = ak * b_prev + bk
            ak = ak * a_prev
        hk = ak * h + bk
        hs_ref[k * SUBLANES:(k + 1) * SUBLANES, :] = hk
        h = hk[SUBLANES - 1:SUBLANES]
    h_ref[...] = h
    hlast_ref[...] = h


def _block_diag(w):
    n, c, d = w.shape
    return jnp.einsum('ncd,nm->ncmd', w, jnp.eye(n, dtype=w.dtype)).reshape(n * c, n * d)


def _lru(proj, B, T, buf, h0, conv_w, conv_b, wa_bd, ba, wx_bd, bx, lam):
    tile = min(LRU_TILE, T)
    assert T % tile == 0 and tile % SUBLANES == 0 and _SEG_DST["lru_x"] == 0
    row = lambda a: a.reshape(1, D_RNN)
    per_b = lambda b, t: (b, 0, 0)
    const = lambda b, t: (0, 0)
    vec = pl.BlockSpec((1, D_RNN), const)
    mat = pl.BlockSpec((D_RNN, D_RNN), const)
    hs, hlast, cbuf = pl.pallas_call(
        functools.partial(_lru_kernel, tile=tile),
        out_shape=(jax.ShapeDtypeStruct((B, T, D_RNN), jnp.float32),
                   jax.ShapeDtypeStruct((B, 1, D_RNN), jnp.float32),
                   jax.ShapeDtypeStruct((B, CONV_W - 1, D_RNN), jnp.float32)),
        grid=(B, T // tile),
        in_specs=[
            pl.BlockSpec((None, tile, D_RNN), lambda b, t: (b, t, 0)),
            pl.BlockSpec((None, CONV_W - 1, D_RNN), per_b),
            pl.BlockSpec((None, 1, D_RNN), per_b),
            pl.BlockSpec((CONV_W, D_RNN), const),
            vec, mat, vec, mat, vec, vec,
        ],
        out_specs=(pl.BlockSpec((None, tile, D_RNN), lambda b, t: (b, t, 0)),
                   pl.BlockSpec((None, 1, D_RNN), per_b),
                   pl.BlockSpec((None, CONV_W - 1, D_RNN), per_b)),
        scratch_shapes=[pltpu.VMEM((tile + SUBLANES, D_RNN), jnp.float32),
                        pltpu.VMEM((1, D_RNN), jnp.float32)],
        compiler_params=pltpu.CompilerParams(dimension_semantics=("parallel", "arbitrary")),
        name="rg_lru",
    )(proj.reshape(B, T, IN_COLS_PAD), buf, h0.reshape(B, 1, D_RNN), conv_w, row(conv_b),
      wa_bd, row(ba), wx_bd, row(bx), row(jax.nn.log_sigmoid(lam)))
    return hs, hlast.reshape(B, D_RNN), cbuf


def _rmsnorm(x, g):
    xf = x.astype(jnp.float32)
    y = xf * lax.rsqrt(jnp.mean(xf * xf, axis=-1, keepdims=True) + RMS_EPS)
    return y * g.astype(jnp.float32)


def _l2norm(x):
    return x * lax.rsqrt(jnp.sum(x * x, axis=-1, keepdims=True) + L2_EPS)


def _rope(x, pos):
    half = ROT_DIM // 2
    inv = ROPE_THETA ** (-jnp.arange(half, dtype=jnp.float32) * 2.0 / ROT_DIM)
    ang = pos.astype(jnp.float32)[:, None] * inv[None, :]
    shape = (1, pos.shape[0]) + (1,) * (x.ndim - 3) + (half,)
    c = jnp.cos(ang).reshape(shape)
    s = jnp.sin(ang).reshape(shape)
    xf = x.astype(jnp.float32)
    x1, x2, rest = xf[..., :half], xf[..., half:ROT_DIM], xf[..., ROT_DIM:]
    return jnp.concatenate([x1 * c - x2 * s, x2 * c + x1 * s, rest], axis=-1)


def _causal_conv(x, buf, w, b=None):
    T = x.shape[1]
    ext = jnp.concatenate([buf.astype(x.dtype), x], axis=1)
    y = ext[:, 0:T] * w[0]
    for j in range(1, CONV_W):
        y = y + ext[:, j:j + T] * w[j]
    if b is not None:
        y = y + b
    return y, ext[:, -(CONV_W - 1):]


def _gated_delta(q, k, v, g, beta, S0):
    B, T = q.shape[:2]
    C = DN_CHUNK
    Tp = -(-T // C) * C
    nc = Tp // C

    def chunks(a):
        a = jnp.pad(a, ((0, 0), (0, Tp - T)) + ((0, 0),) * (a.ndim - 2))
        a = a.reshape((B, nc, C) + a.shape[2:])
        return jnp.transpose(a, (1, 0, 3, 2) + tuple(range(4, a.ndim)))

    qc, kc, vc = chunks(q), chunks(k), chunks(v)
    gc, bc = chunks(g), chunks(beta)
    G = jnp.cumsum(gc, axis=-1)
    incl = jnp.tril(jnp.ones((C, C), dtype=bool))
    strict = jnp.tril(jnp.ones((C, C), dtype=bool), -1)
    dmat = jnp.exp(jnp.where(incl, G[..., :, None] - G[..., None, :], -jnp.inf))
    kk = jnp.einsum('nbhcd,nbhed->nbhce', kc, kc)
    A = jnp.where(strict, bc[..., :, None] * kk * dmat, 0.0) + jnp.eye(C, dtype=kk.dtype)
    u = lax.linalg.triangular_solve(A, bc[..., None] * vc, left_side=True, lower=True, unit_diagonal=True)
    w = lax.linalg.triangular_solve(A, bc[..., None] * kc * jnp.exp(G)[..., None],
                                    left_side=True, lower=True, unit_diagonal=True)
    qk = jnp.einsum('nbhcd,nbhed->nbhce', qc, kc) * dmat
    qdec = qc * jnp.exp(G)[..., None]
    kdec = kc * jnp.exp(G[..., -1:] - G)[..., None]
    glast = jnp.exp(G[..., -1])

    def step(S, xs):
        u_, w_, qk_, qdec_, kdec_, gl_ = xs
        vn = u_ - jnp.einsum('bhcd,bhde->bhce', w_, S)
        o = jnp.einsum('bhcd,bhde->bhce', qdec_, S) + jnp.einsum('bhce,bhef->bhcf', qk_, vn)
        S = gl_[..., None, None] * S + jnp.einsum('bhcd,bhce->bhde', kdec_, vn)
        return S, o

    S, o = lax.scan(step, S0, (u, w, qk, qdec, kdec, glast))
    o = jnp.transpose(o, (1, 0, 3, 2, 4)).reshape(B, Tp, DN_HEADS, DN_HD)[:, :T]
    return o, S


def _seg_slot(proj, name, B, T):
    s = _SEG_DST[name]
    return proj[:, s:s + LANE].reshape(B, T, LANE)


def _seg(proj, name, B, T):
    s = _SEG_DST[name]
    return proj[:, s:s + _SEG_LEN[name]].reshape(B, T, _SEG_LEN[name])


def _layer(x, paged, lru_h0, lru_buf, dn_S0, dn_buf,
           norm_g, w_in_pad, lru_conv_w, lru_conv_b, lru_wa, lru_ba, lru_wx, lru_bx, lru_lam,
           q_norm, k_norm, dn_conv_w, dn_A_log, dn_dt_bias, dn_o_norm,
           w_lru_out, w_nsa_out, w_dn_out, w_out):
    f32 = jnp.float32
    B, T, _ = x.shape
    P = 0 if paged is None else paged[0].shape[1] * PAGE_SIZE
    pos = P + jnp.arange(T, dtype=jnp.int32)
    x2d = x.reshape(B * T, D_MODEL)
    proj = _in_proj(x2d, norm_g, w_in_pad)
    nsa_q, nsa_kv, dn_qkv, dn_a, dn_b = (
        _seg(proj, n, B, T) for n in ("nsa_q", "nsa_kv", "dn_qkv", "dn_a", "dn_b"))

    lru_seq, lru_h, lru_buf_new = _lru(proj, B, T, lru_buf, lru_h0, lru_conv_w, lru_conv_b,
                                       lru_wa, lru_ba, lru_wx, lru_bx, lru_lam)

    q = _rope(_rmsnorm(nsa_q.reshape(B, T, NSA_HEADS, NSA_HD), q_norm), pos) * (NSA_HD ** -0.5)
    kv = nsa_kv.reshape(B, T, 2 * N_BRANCH, NSA_KV, NSA_HD)
    k = _rope(_rmsnorm(kv[:, :, 0::2], k_norm[:, None, :]), pos).astype(x.dtype)
    v = kv[:, :, 1::2]
    rows = jnp.stack([k[:, :, 0], v[:, :, 0], k[:, :, 1], v[:, :, 1]], axis=2)
    win_rows = jnp.stack([k[:, :, 2], v[:, :, 2]], axis=2)
    flat = lambda a: a.reshape(B, T, KV_LANES)
    bf = lambda a: flat(a).astype(jnp.bfloat16)
    gate = _seg_slot(proj, "nsa_g", B, T)
    if paged is None:
        o_nsa = _nsa_prompt(q.reshape(B, T, NSA_WIDTH), gate,
                            flat(k[:, :, 0]), flat(v[:, :, 0]), bf(k[:, :, 1]), bf(v[:, :, 1]),
                            bf(k[:, :, 2]), bf(v[:, :, 2]))
        win_state = win_rows[:, -min(WINDOW, T):]
    else:
        page_table, cache, layer, win_prev = paged
        new = lambda a: jnp.pad(bf(a), ((0, 0), (0, NEW_PAD - T), (0, 0)))
        o_nsa = _nsa_sample(page_table, cache, layer, q.reshape(B, T, NSA_WIDTH), gate,
                            new(k[:, :, 1]), new(v[:, :, 1]),
                            win_prev.reshape(B, win_prev.shape[1], 2 * KV_LANES),
                            new(k[:, :, 2]), new(v[:, :, 2]))
        win_state = jnp.concatenate([win_prev.astype(win_rows.dtype), win_rows], axis=1)[:, -WINDOW:]

    qkv, dn_buf_new = _causal_conv(dn_qkv, dn_buf, dn_conv_w)
    qkv = jax.nn.silu(qkv.astype(f32))
    dq, dk, dv = jnp.split(qkv, 3, axis=-1)
    dq = _l2norm(dq.reshape(B, T, DN_HEADS, DN_HD)) * (DN_HD ** -0.5)
    dk = _l2norm(dk.reshape(B, T, DN_HEADS, DN_HD))
    dv = dv.reshape(B, T, DN_HEADS, DN_HD)
    beta = jax.nn.sigmoid(dn_b.astype(f32))
    g = -jnp.exp(dn_A_log.astype(f32)) * jax.nn.softplus(dn_a.astype(f32) + dn_dt_bias.astype(f32))
    o_dn, dn_S = _gated_delta(dq, dk, dv, g, beta, dn_S0.astype(f32))
    o_dn = _rmsnorm(o_dn, dn_o_norm).reshape(B, T, DN_WIDTH)

    y2d = _out_stage(x2d, proj, lru_seq.reshape(B * T, D_RNN), o_nsa.reshape(B * T, NSA_WIDTH),
                     o_dn.reshape(B * T, DN_WIDTH), w_lru_out, w_nsa_out, w_dn_out, w_out)
    return y2d.reshape(B, T, D_MODEL), (rows, win_state, lru_h, lru_buf_new, dn_S, dn_buf_new)


def kernel(x_prompt, x_sample, cache_nsa_kv, page_table, state_nsa_win, state_lru_h, state_lru_conv,
           state_dn_S, state_dn_conv, norm_gain, w_in, lru_conv_w, lru_conv_b, lru_wa, lru_ba,
           lru_wx, lru_bx, lru_lambda, nsa_q_norm, nsa_k_norm, dn_conv_w, dn_A_log, dn_dt_bias,
           dn_o_norm, w_lru_out, w_nsa_out, w_dn_out, w_out):
    Bp = x_prompt.shape[0]
    dt = x_prompt.dtype
    bf16 = jnp.bfloat16
    cache = cache_nsa_kv.reshape(cache_nsa_kv.shape[:2] + (PAGE_SIZE, 4 * KV_LANES))
    p_h0 = jnp.zeros((Bp, D_RNN), jnp.float32)
    p_lbuf0 = jnp.zeros((Bp, CONV_W - 1, D_RNN), dt)
    p_S0 = jnp.zeros((Bp, DN_HEADS, DN_HD, DN_HD), jnp.float32)
    p_dbuf0 = jnp.zeros((Bp, CONV_W - 1, 3 * DN_WIDTH), dt)
    wa_bd = jax.vmap(_block_diag)(lru_wa).astype(bf16)
    wx_bd = jax.vmap(_block_diag)(lru_wx).astype(bf16)
    weights = (norm_gain, _pad_in_weight(w_in), lru_conv_w, lru_conv_b, wa_bd, lru_ba, wx_bd, lru_bx,
               lru_lambda, nsa_q_norm, nsa_k_norm, dn_conv_w, dn_A_log, dn_dt_bias, dn_o_norm,
               w_lru_out.astype(bf16), _heads_rg(w_nsa_out.swapaxes(1, 2)).swapaxes(1, 2).astype(bf16),
               w_dn_out.astype(bf16), w_out.astype(bf16))
    xp, xs = x_prompt, x_sample
    st_p, st_s = [], []
    for l in range(DEPTH):
        lw = [w[l] for w in weights]
        xp, sp = _layer(xp, None, p_h0, p_lbuf0, p_S0, p_dbuf0, *lw)
        xs, ss = _layer(xs, (page_table, cache, l, state_nsa_win[l]), state_lru_h[l], state_lru_conv[l],
                        state_dn_S[l], state_dn_conv[l], *lw)
        st_p.append(sp)
        st_s.append(ss)
    kv_p, win_p, lh_p, lc_p, S_p, dc_p = [jnp.stack(a) for a in zip(*st_p)]
    kv_s, win_s, lh_s, lc_s, S_s, dc_s = [jnp.stack(a) for a in zip(*st_s)]
    return (xp, xs, kv_p, kv_s, win_p, win_s, lh_p, lh_s, lc_p, lc_s, S_p, S_s, dc_p, dc_s)
```

```python
import functools
import math

import numpy as np
import jax
import jax.numpy as jnp
from jax import lax
from jax.experimental import pallas as pl
from jax.experimental.pallas import tpu as pltpu

D_MODEL = 1024
DEPTH = 4
PAGE_SIZE = 128
CONV_W = 4
RMS_EPS = 1e-6
L2_EPS = 1e-6
N_BRANCH = 3
D_RNN = D_MODEL // 2
LRU_BLOCKS = 8
LRU_BS = D_RNN // LRU_BLOCKS
LRU_C = 8.0
NSA_HEADS = 8
NSA_HD = 64
NSA_KV = 2
NSA_REP = NSA_HEADS // NSA_KV
NSA_WIDTH = NSA_HEADS * NSA_HD
CMP_BLOCK = 32
SLC_BLOCK = 64
N_SEL = 16
WINDOW = 512
Q_BLOCK = 128
ROT_DIM = NSA_HD // 4
ROPE_THETA = 500000.0
DN_HEADS = 4
DN_HD = 128
DN_WIDTH = DN_HEADS * DN_HD
DN_CHUNK = 64

LANE = 128

_SEG_NAMES = ("lru_x", "lru_z", "nsa_q", "nsa_kv", "nsa_z", "nsa_g", "dn_qkv", "dn_z", "dn_a", "dn_b", "merge_g")
_SEG_SIZES = (D_RNN, D_RNN, NSA_WIDTH, 2 * N_BRANCH * NSA_KV * NSA_HD, NSA_WIDTH, N_BRANCH * NSA_HEADS,
              3 * DN_WIDTH, DN_WIDTH, DN_HEADS, DN_HEADS, N_BRANCH * D_MODEL)
_SEG_SRC = dict(zip(_SEG_NAMES, np.concatenate([[0], np.cumsum(_SEG_SIZES)[:-1]]).tolist()))
_SEG_LEN = dict(zip(_SEG_NAMES, _SEG_SIZES))
_DST_SLOTS = (
    (("lru_x",), D_RNN), (("lru_z",), D_RNN), (("nsa_q",), NSA_WIDTH), (("nsa_z",), NSA_WIDTH),
    (("dn_z",), DN_WIDTH), (("nsa_g",), LANE), (("dn_a", "dn_b"), LANE),
    (("nsa_kv",), 2 * N_BRANCH * NSA_KV * NSA_HD), (("dn_qkv",), 3 * DN_WIDTH), (("merge_g",), N_BRANCH * D_MODEL),
)


def _dst_layout():
    off, dst = 0, {}
    for names, width in _DST_SLOTS:
        o = off
        for n in names:
            dst[n] = o
            o += _SEG_LEN[n]
        off += width
    return dst, off


_SEG_DST, IN_COLS_PAD = _dst_layout()


def _heads_rg(a):
    lead = a.shape[:-1]
    return a.reshape(lead + (NSA_KV, NSA_REP, NSA_HD)).swapaxes(-3, -2).reshape(lead + (NSA_WIDTH,))


def _pad_in_weight(w_in):
    pieces = []
    for names, width in _DST_SLOTS:
        used = 0
        for n in names:
            piece = w_in[:, :, _SEG_SRC[n]:_SEG_SRC[n] + _SEG_LEN[n]]
            pieces.append(_heads_rg(piece) if n in ("nsa_q", "nsa_z") else piece)
            used += _SEG_LEN[n]
        if used < width:
            pieces.append(jnp.zeros(w_in.shape[:2] + (width - used,), w_in.dtype))
    return jnp.concatenate(pieces, axis=-1).astype(jnp.bfloat16)


def _in_proj_kernel(x_ref, g_ref, w_ref, o_ref, h_ref):
    @pl.when(pl.program_id(1) == 0)
    def _():
        x = x_ref[...]
        ms = jnp.mean(x * x, axis=-1, keepdims=True)
        h_ref[...] = (x * lax.rsqrt(ms + RMS_EPS) * g_ref[...]).astype(jnp.bfloat16)

    o_ref[...] = jnp.dot(h_ref[...], w_ref[...], preferred_element_type=jnp.float32)


def _in_proj(x2d, gain, w_pad):
    m = x2d.shape[0]
    tm = min(512, m)
    tn = 1024
    return pl.pallas_call(
        _in_proj_kernel,
        out_shape=jax.ShapeDtypeStruct((m, IN_COLS_PAD), jnp.float32),
        grid=(m // tm, IN_COLS_PAD // tn),
        in_specs=[
            pl.BlockSpec((tm, D_MODEL), lambda i, j: (i, 0)),
            pl.BlockSpec((1, D_MODEL), lambda i, j: (0, 0)),
            pl.BlockSpec((D_MODEL, tn), lambda i, j: (0, j)),
        ],
        out_specs=pl.BlockSpec((tm, tn), lambda i, j: (i, j)),
        scratch_shapes=[pltpu.VMEM((tm, D_MODEL), jnp.bfloat16)],
        compiler_params=pltpu.CompilerParams(dimension_semantics=("parallel", "arbitrary")),
        name="in_proj",
    )(x2d, gain.reshape(1, D_MODEL), w_pad)


def _silu(z):
    return z * jax.nn.sigmoid(z)


def _out_stage_kernel(x_ref, a_lru, z_lru, a_nsa, z_nsa, a_dn, z_dn, mg0, mg1, mg2,
                      w_lru, w_nsa, w_dn, w_out, o_ref):
    def branch(a, z, w):
        y = (a[...] * _silu(z[...])).astype(jnp.bfloat16)
        return jnp.dot(y, w[...], preferred_element_type=jnp.float32)

    merged = jax.nn.sigmoid(mg0[...]) * branch(a_lru, z_lru, w_lru)
    merged = merged + jax.nn.sigmoid(mg1[...]) * branch(a_nsa, z_nsa, w_nsa)
    merged = merged + jax.nn.sigmoid(mg2[...]) * branch(a_dn, z_dn, w_dn)
    y = jnp.dot(merged.astype(jnp.bfloat16), w_out[...], preferred_element_type=jnp.float32)
    o_ref[...] = x_ref[...] + y


def _out_stage(x2d, proj, a_lru, a_nsa, a_dn, w_lru, w_nsa, w_dn, w_out):
    m = x2d.shape[0]
    tm = min(256, m)
    half = D_RNN
    row = lambda i: (i, 0)
    col = lambda c: (lambda i: (i, c))
    full = lambda i: (0, 0)
    mg_blk = _SEG_DST["merge_g"] // D_MODEL
    return pl.pallas_call(
        _out_stage_kernel,
        out_shape=jax.ShapeDtypeStruct((m, D_MODEL), jnp.float32),
        grid=(m // tm,),
        in_specs=[
            pl.BlockSpec((tm, D_MODEL), row),
            pl.BlockSpec((tm, half), row),
            pl.BlockSpec((tm, half), col(_SEG_DST["lru_z"] // half)),
            pl.BlockSpec((tm, half), row),
            pl.BlockSpec((tm, half), col(_SEG_DST["nsa_z"] // half)),
            pl.BlockSpec((tm, half), row),
            pl.BlockSpec((tm, half), col(_SEG_DST["dn_z"] // half)),
            pl.BlockSpec((tm, D_MODEL), col(mg_blk)),
            pl.BlockSpec((tm, D_MODEL), col(mg_blk + 1)),
            pl.BlockSpec((tm, D_MODEL), col(mg_blk + 2)),
            pl.BlockSpec((half, D_MODEL), full),
            pl.BlockSpec((half, D_MODEL), full),
            pl.BlockSpec((half, D_MODEL), full),
            pl.BlockSpec((D_MODEL, D_MODEL), full),
        ],
        out_specs=pl.BlockSpec((tm, D_MODEL), row),
        compiler_params=pltpu.CompilerParams(dimension_semantics=("parallel",)),
        name="out_stage",
    )(x2d, a_lru, proj, a_nsa, proj, a_dn, proj, proj, proj, proj, w_lru, w_nsa, w_dn, w_out)


NEG_BIG = -1e30
SLC_CHUNK = 512
N_ROWS = NSA_KV * NSA_REP * Q_BLOCK
KV_LANES = NSA_KV * NSA_HD
WIN_KEYS = WINDOW + Q_BLOCK
MAX_SLC_BLOCKS = LANE // 2


def _dot_t(a, b):
    return lax.dot_general(a, b, (((1,), (1,)), ((), ())), preferred_element_type=jnp.float32)


def _select_blocks(imp_t, cur, n_top):
    nb = imp_t.shape[0]
    jj = lax.broadcasted_iota(jnp.int32, imp_t.shape, 0)
    v = jnp.where((jj == 0) | (jj == cur), jnp.inf, jnp.where(jj > cur, -jnp.inf, imp_t))
    sub = lax.broadcasted_iota(jnp.int32, (8, imp_t.shape[1]), 0)
    ranks = []
    for a in range(nb // 8):
        va = v[8 * a:8 * a + 8]
        rank = jnp.zeros(va.shape, jnp.float32)
        for j in range(nb):
            row = v[j:j + 1]
            ge = jnp.where(row >= va, 1.0, 0.0)
            gt = jnp.where(row > va, 1.0, 0.0)
            if j < 8 * a:
                rank = rank + ge
            elif j >= 8 * a + 8:
                rank = rank + gt
            else:
                rank = rank + jnp.where(sub > (j - 8 * a), ge, gt)
        ranks.append(rank)
    rank = jnp.concatenate(ranks, axis=0)
    return jnp.where((rank < n_top) & (jj <= cur), 1.0, 0.0)


def _nsa_prompt_kernel(q_ref, gate_ref, kc_ref, vc_ref, ks_ref, vs_ref, kw_ref, vw_ref, o_ref,
                       kcb_ref, vcb_ref, m_ref, l_ref, acc_ref, *, seq_len, n_top):
    f32, bf16 = jnp.float32, jnp.bfloat16
    i = pl.program_id(1)
    nsb = seq_len // SLC_BLOCK
    half = MAX_SLC_BLOCKS

    @pl.when(i == 0)
    def _():
        if nsb < half:
            kcb_ref[...] = jnp.zeros(kcb_ref.shape, f32)
            vcb_ref[...] = jnp.zeros(vcb_ref.shape, f32)
        for src, dst in ((kc_ref, kcb_ref), (vc_ref, vcb_ref)):
            x = src[...].reshape(nsb, SLC_BLOCK, KV_LANES)
            dst[0:nsb, :] = jnp.sum(x[:, :CMP_BLOCK, :], axis=1) * (1.0 / CMP_BLOCK)
            dst[half:half + nsb, :] = jnp.sum(x[:, CMP_BLOCK:, :], axis=1) * (1.0 / CMP_BLOCK)

    lane = lax.broadcasted_iota(jnp.int32, (Q_BLOCK, LANE), 1)
    tq = lax.broadcasted_iota(jnp.int32, (Q_BLOCK, LANE), 0)
    pos = i * Q_BLOCK + tq
    low = lane < NSA_HD

    q = q_ref[...]
    parts = []
    for g in range(NSA_KV):
        for r in range(NSA_REP):
            qr = q[:, r * KV_LANES:(r + 1) * KV_LANES]
            parts.append(jnp.where(low if g == 0 else ~low, qr, 0.0))
    qpad = jnp.concatenate(parts, axis=0).astype(bf16)

    s = _dot_t(qpad, kcb_ref[...].astype(bf16))
    cblk = jnp.where(lane < half, 2 * lane, 2 * lane - (2 * half - 1))
    okc = (cblk * CMP_BLOCK + (CMP_BLOCK - 1)) <= pos
    s3 = s.reshape(NSA_KV * NSA_REP, Q_BLOCK, LANE) + jnp.where(okc, 0.0, NEG_BIG)[None]
    mx = jnp.max(s3, axis=-1, keepdims=True)
    e = jnp.where(okc[None], jnp.exp(s3 - mx), 0.0)
    den = jnp.sum(e, axis=-1, keepdims=True)
    p3 = e / jnp.where(den > 0.0, den, 1.0)
    o_cmp = jnp.dot(p3.reshape(N_ROWS, LANE).astype(bf16), vcb_ref[...].astype(bf16),
                    preferred_element_type=f32)

    pg = p3.reshape(NSA_KV, NSA_REP, Q_BLOCK, LANE).sum(axis=1)
    cur_t = (i * Q_BLOCK + lax.broadcasted_iota(jnp.int32, (half, Q_BLOCK), 1)) // SLC_BLOCK
    selq = []
    for g in range(NSA_KV):
        imp = pg[g] + pltpu.roll(pg[g], half, axis=1)
        sel_t = _select_blocks(imp.T[:half], cur_t, n_top)
        sel_full = jnp.concatenate([sel_t, jnp.zeros_like(sel_t)], axis=0)
        selq.append(sel_full.T[:, :half].astype(bf16))

    m_ref[...] = jnp.full(m_ref.shape, NEG_BIG, f32)
    l_ref[...] = jnp.zeros(l_ref.shape, f32)
    acc_ref[...] = jnp.zeros(acc_ref.shape, f32)
    heads = (NSA_KV, NSA_REP, Q_BLOCK)

    def slc_step(c, carry):
        start = pl.multiple_of(c * SLC_CHUNK, SLC_CHUNK)
        kch = ks_ref[pl.ds(start, SLC_CHUNK), :]
        vch = vs_ref[pl.ds(start, SLC_CHUNK), :]
        sc = _dot_t(qpad, kch)
        kidx = start + lax.broadcasted_iota(jnp.int32, (Q_BLOCK, SLC_CHUNK), 1)
        causal = kidx <= i * Q_BLOCK + lax.broadcasted_iota(jnp.int32, (Q_BLOCK, SLC_CHUNK), 0)
        kblk = (start + lax.broadcasted_iota(jnp.int32, (half, SLC_CHUNK), 1)) // SLC_BLOCK
        expand = jnp.where(kblk == lax.broadcasted_iota(jnp.int32, (half, SLC_CHUNK), 0), 1.0, 0.0).astype(bf16)
        bias = []
        for g in range(NSA_KV):
            picked = jnp.dot(selq[g], expand, preferred_element_type=f32) > 0.5
            bias.append(jnp.where(picked & causal, 0.0, NEG_BIG))
        bias = jnp.stack(bias, axis=0)[:, None]
        s4 = sc.reshape(heads + (SLC_CHUNK,)) + bias
        m_old = m_ref[...].reshape(heads + (1,))
        m_new = jnp.maximum(m_old, jnp.max(s4, axis=-1, keepdims=True))
        alpha = jnp.exp(m_old - m_new)
        ex = jnp.exp(s4 - m_new)
        l_new = alpha * l_ref[...].reshape(heads + (1,)) + jnp.sum(ex, axis=-1, keepdims=True)
        pv = jnp.dot(ex.reshape(N_ROWS, SLC_CHUNK).astype(bf16), vch, preferred_element_type=f32)
        acc_ref[...] = alpha.reshape(N_ROWS, 1) * acc_ref[...] + pv
        l_ref[...] = l_new.reshape(N_ROWS, 1)
        m_ref[...] = m_new.reshape(N_ROWS, 1)
        return carry

    n_chunks = (i * Q_BLOCK + Q_BLOCK + SLC_CHUNK - 1) // SLC_CHUNK
    lax.fori_loop(0, n_chunks, slc_step, 0)
    o_slc = acc_ref[...] / l_ref[...]

    wstart = pl.multiple_of(jnp.maximum(i - WINDOW // Q_BLOCK, 0) * Q_BLOCK, Q_BLOCK)
    kwin = kw_ref[pl.ds(wstart, WIN_KEYS), :]
    vwin = vw_ref[pl.ds(wstart, WIN_KEYS), :]
    sw = _dot_t(qpad, kwin)
    widx = wstart + lax.broadcasted_iota(jnp.int32, (Q_BLOCK, WIN_KEYS), 1)
    wpos = i * Q_BLOCK + lax.broadcasted_iota(jnp.int32, (Q_BLOCK, WIN_KEYS), 0)
    okw = (widx <= wpos) & (wpos - widx < WINDOW)
    sw3 = sw.reshape(NSA_KV * NSA_REP, Q_BLOCK, WIN_KEYS) + jnp.where(okw, 0.0, NEG_BIG)[None]
    ew = jnp.exp(sw3 - jnp.max(sw3, axis=-1, keepdims=True))
    lw = jnp.sum(ew, axis=-1, keepdims=True).reshape(N_ROWS, 1)
    o_win = jnp.dot(ew.reshape(N_ROWS, WIN_KEYS).astype(bf16), vwin, preferred_element_type=f32) / lw

    gate = jax.nn.sigmoid(gate_ref[...])
    n_hd = NSA_KV * NSA_REP
    for r in range(NSA_REP):
        per_g = []
        for g in range(NSA_KV):
            h = g * NSA_REP + r
            rows = slice(h * Q_BLOCK, (h + 1) * Q_BLOCK)
            per_g.append(gate[:, h:h + 1] * o_cmp[rows]
                         + gate[:, n_hd + h:n_hd + h + 1] * o_slc[rows]
                         + gate[:, 2 * n_hd + h:2 * n_hd + h + 1] * o_win[rows])
        o_ref[:, r * KV_LANES:(r + 1) * KV_LANES] = jnp.where(low, per_g[0], per_g[1])


def _nsa_prompt(q, gate, kc, vc, ks, vs, kw, vw):
    B, T, _ = q.shape
    assert T % SLC_CHUNK == 0 and T >= WIN_KEYS and T // SLC_BLOCK <= MAX_SLC_BLOCKS
    n_top = min(N_SEL, T // SLC_BLOCK)
    blk = lambda b, i: (b, i, 0)
    whole = lambda b, i: (b, 0, 0)
    kv_spec = pl.BlockSpec((None, T, KV_LANES), whole)
    return pl.pallas_call(
        functools.partial(_nsa_prompt_kernel, seq_len=T, n_top=n_top),
        out_shape=jax.ShapeDtypeStruct((B, T, NSA_WIDTH), jnp.float32),
        grid=(B, T // Q_BLOCK),
        in_specs=[
            pl.BlockSpec((None, Q_BLOCK, NSA_WIDTH), blk),
            pl.BlockSpec((None, Q_BLOCK, LANE), blk),
            kv_spec, kv_spec, kv_spec, kv_spec, kv_spec, kv_spec,
        ],
        out_specs=pl.BlockSpec((None, Q_BLOCK, NSA_WIDTH), blk),
        scratch_shapes=[
            pltpu.VMEM((LANE, KV_LANES), jnp.float32),
            pltpu.VMEM((LANE, KV_LANES), jnp.float32),
            pltpu.VMEM((N_ROWS, 1), jnp.float32),
            pltpu.VMEM((N_ROWS, 1), jnp.float32),
            pltpu.VMEM((N_ROWS, KV_LANES), jnp.float32),
        ],
        compiler_params=pltpu.CompilerParams(
            dimension_semantics=("parallel", "arbitrary"), vmem_limit_bytes=48 * 1024 * 1024),
        name="nsa_prompt",
    )(q, gate, kc, vc, ks, vs, kw, vw)


PAGES_PER_STEP = 8
NEW_PAD = 128


def _qpad_rows(q, nq):
    low = lax.broadcasted_iota(jnp.int32, (nq, LANE), 1) < NSA_HD
    parts = []
    for g in range(NSA_KV):
        for r in range(NSA_REP):
            qr = q[:, r * KV_LANES:(r + 1) * KV_LANES]
            parts.append(jnp.where(low if g == 0 else ~low, qr, 0.0))
    return jnp.concatenate(parts, axis=0).astype(jnp.bfloat16)


def _nsa_sample_kernel(pt_ref, *refs, n_q, past_len, n_top):
    f32, bf16 = jnp.float32, jnp.bfloat16
    pages = refs[:PAGES_PER_STEP]
    (q_ref, gate_ref, ksn_ref, vsn_ref, win_ref, kwn_ref, vwn_ref, o_ref,
     kcb_ref, vcb_ref, sel_ref, ocmp_ref, m_ref, l_ref, acc_ref) = refs[PAGES_PER_STEP:]
    ph = pl.program_id(1)
    p = pl.program_id(2)
    last = pl.num_programs(2) - 1
    n_rows = NSA_KV * NSA_REP * n_q
    n_blk = past_len // SLC_BLOCK
    cmp_per_step = PAGES_PER_STEP * PAGE_SIZE // CMP_BLOCK
    keys_per_step = PAGES_PER_STEP * PAGE_SIZE
    heads = (NSA_KV, NSA_REP, n_q)
    qpad = _qpad_rows(q_ref[...], n_q)

    @pl.when(ph == 0)
    def _():
        means = [pg[...].reshape(PAGE_SIZE // CMP_BLOCK, CMP_BLOCK, 2 * KV_LANES).sum(axis=1) * (1.0 / CMP_BLOCK)
                 for pg in pages]
        means = jnp.concatenate(means, axis=0)
        row0 = pl.multiple_of(p * cmp_per_step, cmp_per_step)
        kcb_ref[pl.ds(row0, cmp_per_step), :] = means[:, :KV_LANES]
        vcb_ref[pl.ds(row0, cmp_per_step), :] = means[:, KV_LANES:]

    @pl.when((ph == 0) & (p == last))
    def _():
        halves = []
        for par in range(2):
            kc = kcb_ref[pl.ds(par, n_blk, stride=2), :].astype(bf16)
            halves.append(_dot_t(qpad, kc))
        mx = jnp.maximum(jnp.max(halves[0], axis=-1, keepdims=True), jnp.max(halves[1], axis=-1, keepdims=True))
        e0, e1 = jnp.exp(halves[0] - mx), jnp.exp(halves[1] - mx)
        inv = 1.0 / (jnp.sum(e0, axis=-1, keepdims=True) + jnp.sum(e1, axis=-1, keepdims=True))
        p0, p1 = e0 * inv, e1 * inv
        oc = jnp.zeros((n_rows, KV_LANES), f32)
        for par, pp in ((0, p0), (1, p1)):
            vc = vcb_ref[pl.ds(par, n_blk, stride=2), :].astype(bf16)
            oc = oc + jnp.dot(pp.astype(bf16), vc, preferred_element_type=f32)
        ocmp_ref[...] = oc
        imp = (p0 + p1).reshape(heads + (n_blk,)).sum(axis=1).reshape(NSA_KV * n_q, n_blk)
        lane = lax.broadcasted_iota(jnp.int32, imp.shape, 1)
        v = jnp.where(lane == 0, jnp.inf, imp)
        rank = jnp.zeros(imp.shape, f32)
        for j in range(n_blk):
            col = v[:, j:j + 1]
            tie = jnp.where(lane > j, 1.0, 0.0)
            rank = rank + jnp.where(col > v, 1.0, jnp.where(col == v, tie, 0.0))
        sel_ref[...] = jnp.where(rank < n_top - 1, 1.0, 0.0)
        m_ref[...] = jnp.full(m_ref.shape, NEG_BIG, f32)
        l_ref[...] = jnp.zeros(l_ref.shape, f32)
        acc_ref[...] = jnp.zeros(acc_ref.shape, f32)

    def online(s4, vals):
        m_old = m_ref[...].reshape(heads + (1,))
        m_new = jnp.maximum(m_old, jnp.max(s4, axis=-1, keepdims=True))
        alpha = jnp.exp(m_old - m_new)
        ex = jnp.exp(s4 - m_new)
        l_new = alpha * l_ref[...].reshape(heads + (1,)) + jnp.sum(ex, axis=-1, keepdims=True)
        pv = jnp.dot(ex.reshape(n_rows, s4.shape[-1]).astype(bf16), vals, preferred_element_type=f32)
        acc_ref[...] = alpha.reshape(n_rows, 1) * acc_ref[...] + pv
        l_ref[...] = l_new.reshape(n_rows, 1)
        m_ref[...] = m_new.reshape(n_rows, 1)

    @pl.when(ph == 1)
    def _():
        xs = [pg[...] for pg in pages]
        sc = jnp.concatenate([_dot_t(qpad, x[:, :KV_LANES].astype(bf16)) for x in xs], axis=-1)
        vals = jnp.concatenate([x[:, KV_LANES:].astype(bf16) for x in xs], axis=0)
        kblk = (p * keys_per_step + lax.broadcasted_iota(jnp.int32, (n_blk, keys_per_step), 1)) // SLC_BLOCK
        expand = jnp.where(kblk == lax.broadcasted_iota(jnp.int32, (n_blk, keys_per_step), 0), 1.0, 0.0).astype(bf16)
        picked = jnp.dot(sel_ref[...].astype(bf16), expand, preferred_element_type=f32) > 0.5
        bias = jnp.where(picked, 0.0, NEG_BIG).reshape(NSA_KV, 1, n_q, keys_per_step)
        online(sc.reshape(heads + (keys_per_step,)) + bias, vals)

    @pl.when((ph == 1) & (p == last))
    def _():
        tq = lax.broadcasted_iota(jnp.int32, (n_q, NEW_PAD), 0)
        tk = lax.broadcasted_iota(jnp.int32, (n_q, NEW_PAD), 1)
        new_bias = jnp.where(tk <= tq, 0.0, NEG_BIG)[None, None]
        s_new = _dot_t(qpad, ksn_ref[...]).reshape(heads + (NEW_PAD,)) + new_bias
        online(s_new, vsn_ref[...])
        o_slc = acc_ref[...] / l_ref[...]

        win = win_ref[...]
        n_prev = win.shape[0]
        s_prev = _dot_t(qpad, win[:, :KV_LANES].astype(bf16))
        pk = lax.broadcasted_iota(jnp.int32, (n_q, n_prev), 1)
        pq = lax.broadcasted_iota(jnp.int32, (n_q, n_prev), 0)
        s_prev = s_prev.reshape(heads + (n_prev,)) + jnp.where(pk > pq + (n_prev - WINDOW), 0.0, NEG_BIG)[None, None]
        s_wnew = _dot_t(qpad, kwn_ref[...]).reshape(heads + (NEW_PAD,)) + new_bias
        mw = jnp.maximum(jnp.max(s_prev, axis=-1, keepdims=True), jnp.max(s_wnew, axis=-1, keepdims=True))
        e_prev, e_new = jnp.exp(s_prev - mw), jnp.exp(s_wnew - mw)
        lw = jnp.sum(e_prev, axis=-1, keepdims=True) + jnp.sum(e_new, axis=-1, keepdims=True)
        o_win = (jnp.dot(e_prev.reshape(n_rows, n_prev).astype(bf16), win[:, KV_LANES:].astype(bf16),
                         preferred_element_type=f32)
                 + jnp.dot(e_new.reshape(n_rows, NEW_PAD).astype(bf16), vwn_ref[...], preferred_element_type=f32))
        o_win = o_win / lw.reshape(n_rows, 1)

        gate = jax.nn.sigmoid(gate_ref[...])
        low = lax.broadcasted_iota(jnp.int32, (n_q, LANE), 1) < NSA_HD
        o_cmp = ocmp_ref[...]
        n_hd = NSA_KV * NSA_REP
        for r in range(NSA_REP):
            per_g = []
            for g in range(NSA_KV):
                h = g * NSA_REP + r
                rows = slice(h * n_q, (h + 1) * n_q)
                per_g.append(gate[:, h:h + 1] * o_cmp[rows]
                             + gate[:, n_hd + h:n_hd + h + 1] * o_slc[rows]
                             + gate[:, 2 * n_hd + h:2 * n_hd + h + 1] * o_win[rows])
            o_ref[:, r * KV_LANES:(r + 1) * KV_LANES] = jnp.where(low, per_g[0], per_g[1])


def _nsa_sample(page_table, cache, layer, q, gate, ks_new, vs_new, win_prev, kw_new, vw_new):
    B, T, _ = q.shape
    n_pages = page_table.shape[1]
    past_len = n_pages * PAGE_SIZE
    n_prev = win_prev.shape[1]
    n_blk = past_len // SLC_BLOCK
    assert T % 8 == 0 and T < CMP_BLOCK and n_pages % PAGES_PER_STEP == 0 and n_blk % LANE == 0
    assert n_blk <= LANE and n_prev == WINDOW
    n_top = min(N_SEL, n_blk + 1)
    n_rows = NSA_KV * NSA_REP * T
    per_b = lambda b, ph, p, pt: (b, 0, 0)

    def page_spec(k):
        return pl.BlockSpec((None, None, PAGE_SIZE, 2 * KV_LANES),
                            lambda b, ph, p, pt: (layer, pt[b, p * PAGES_PER_STEP + k], 0, ph))

    new_spec = pl.BlockSpec((None, NEW_PAD, KV_LANES), per_b)
    grid_spec = pltpu.PrefetchScalarGridSpec(
        num_scalar_prefetch=1,
        grid=(B, 2, n_pages // PAGES_PER_STEP),
        in_specs=[page_spec(k) for k in range(PAGES_PER_STEP)] + [
            pl.BlockSpec((None, T, NSA_WIDTH), per_b),
            pl.BlockSpec((None, T, LANE), per_b),
            new_spec, new_spec,
            pl.BlockSpec((None, n_prev, 2 * KV_LANES), per_b),
            new_spec, new_spec,
        ],
        out_specs=pl.BlockSpec((None, T, NSA_WIDTH), per_b),
        scratch_shapes=[
            pltpu.VMEM((past_len // CMP_BLOCK, KV_LANES), jnp.float32),
            pltpu.VMEM((past_len // CMP_BLOCK, KV_LANES), jnp.float32),
            pltpu.VMEM((NSA_KV * T, n_blk), jnp.float32),
            pltpu.VMEM((n_rows, KV_LANES), jnp.float32),
            pltpu.VMEM((n_rows, 1), jnp.float32),
            pltpu.VMEM((n_rows, 1), jnp.float32),
            pltpu.VMEM((n_rows, KV_LANES), jnp.float32),
        ],
    )
    return pl.pallas_call(
        functools.partial(_nsa_sample_kernel, n_q=T, past_len=past_len, n_top=n_top),
        out_shape=jax.ShapeDtypeStruct((B, T, NSA_WIDTH), jnp.float32),
        grid_spec=grid_spec,
        compiler_params=pltpu.CompilerParams(dimension_semantics=("parallel", "arbitrary", "arbitrary")),
        name="nsa_sample",
    )(page_table, *([cache] * PAGES_PER_STEP), q, gate, ks_new, vs_new, win_prev, kw_new, vw_new)


LRU_TILE = 512
SUBLANES = 8


EXPM1_SERIES_RANGE = 0.35
_EXPM1_COEFFS = tuple(1.0 / math.factorial(n) for n in range(9, 1, -1))


def _one_minus_exp(y):
    acc = jnp.full_like(y, _EXPM1_COEFFS[0])
    for c in _EXPM1_COEFFS[1:]:
        acc = acc * y + c
    series = -(y * (1.0 + y * acc))
    return jnp.where(y > -EXPM1_SERIES_RANGE, series, 1.0 - jnp.exp(y))


def _lru_kernel(x_ref, buf_ref, h0_ref, cw_ref, cb_ref, wa_ref, ba_ref, wx_ref, bx_ref, lsl_ref,
                hs_ref, hlast_ref, cbuf_ref, xe_ref, h_ref, *, tile):
    f32 = jnp.float32
    t = pl.program_id(1)
    tail = CONV_W - 1

    @pl.when(t == 0)
    def _():
        xe_ref[0:SUBLANES, :] = jnp.zeros((SUBLANES, D_RNN), f32)
        xe_ref[SUBLANES - tail:SUBLANES, :] = buf_ref[...]
        h_ref[...] = h0_ref[...]

    xe_ref[SUBLANES:SUBLANES + tile, :] = x_ref[...]
    cw = cw_ref[...]
    u = cb_ref[...]
    for j in range(CONV_W):
        u = u + cw[j:j + 1] * xe_ref[SUBLANES - tail + j:SUBLANES - tail + j + tile, :]

    @pl.when(t == pl.num_programs(1) - 1)
    def _():
        cbuf_ref[...] = xe_ref[SUBLANES + tile - tail:SUBLANES + tile, :]

    xe_ref[0:SUBLANES, :] = xe_ref[tile:tile + SUBLANES, :]

    ub = u.astype(jnp.bfloat16)
    r = jax.nn.sigmoid(jnp.dot(ub, wa_ref[...], preferred_element_type=f32) + ba_ref[...])
    i = jax.nn.sigmoid(jnp.dot(ub, wx_ref[...], preferred_element_type=f32) + bx_ref[...])
    log_a = LRU_C * r * lsl_ref[...]
    a = jnp.exp(log_a)
    b = jnp.sqrt(_one_minus_exp(2.0 * log_a)) * (i * u)

    row = lax.broadcasted_iota(jnp.int32, (SUBLANES, D_RNN), 0)
    h = h_ref[...]
    for k in range(tile // SUBLANES):
        ak = a[k * SUBLANES:(k + 1) * SUBLANES]
        bk = b[k * SUBLANES:(k + 1) * SUBLANES]
        for s in (1, 2, 4):
            a_prev = jnp.where(row >= s, pltpu.roll(ak, s, axis=0), 1.0)
            b_prev = jnp.where(row >= s, pltpu.roll(bk, s, axis=0), 0.0)
            bk = ak * b_prev + bk
            ak = ak * a_prev
        hk = ak * h + bk
        hs_ref[k * SUBLANES:(k + 1) * SUBLANES, :] = hk
        h = hk[SUBLANES - 1:SUBLANES]
    h_ref[...] = h
    hlast_ref[...] = h


def _block_diag(w):
    n, c, d = w.shape
    return jnp.einsum('ncd,nm->ncmd', w, jnp.eye(n, dtype=w.dtype)).reshape(n * c, n * d)


def _lru(proj, B, T, buf, h0, conv_w, conv_b, wa_bd, ba, wx_bd, bx, lam):
    tile = min(LRU_TILE, T)
    assert T % tile == 0 and tile % SUBLANES == 0 and _SEG_DST["lru_x"] == 0
    row = lambda a: a.reshape(1, D_RNN)
    per_b = lambda b, t: (b, 0, 0)
    const = lambda b, t: (0, 0)
    vec = pl.BlockSpec((1, D_RNN), const)
    mat = pl.BlockSpec((D_RNN, D_RNN), const)
    hs, hlast, cbuf = pl.pallas_call(
        functools.partial(_lru_kernel, tile=tile),
        out_shape=(jax.ShapeDtypeStruct((B, T, D_RNN), jnp.float32),
                   jax.ShapeDtypeStruct((B, 1, D_RNN), jnp.float32),
                   jax.ShapeDtypeStruct((B, CONV_W - 1, D_RNN), jnp.float32)),
        grid=(B, T // tile),
        in_specs=[
            pl.BlockSpec((None, tile, D_RNN), lambda b, t: (b, t, 0)),
            pl.BlockSpec((None, CONV_W - 1, D_RNN), per_b),
            pl.BlockSpec((None, 1, D_RNN), per_b),
            pl.BlockSpec((CONV_W, D_RNN), const),
            vec, mat, vec, mat, vec, vec,
        ],
        out_specs=(pl.BlockSpec((None, tile, D_RNN), lambda b, t: (b, t, 0)),
                   pl.BlockSpec((None, 1, D_RNN), per_b),
                   pl.BlockSpec((None, CONV_W - 1, D_RNN), per_b)),
        scratch_shapes=[pltpu.VMEM((tile + SUBLANES, D_RNN), jnp.float32),
                        pltpu.VMEM((1, D_RNN), jnp.float32)],
        compiler_params=pltpu.CompilerParams(dimension_semantics=("parallel", "arbitrary")),
        name="rg_lru",
    )(proj.reshape(B, T, IN_COLS_PAD), buf, h0.reshape(B, 1, D_RNN), conv_w, row(conv_b),
      wa_bd, row(ba), wx_bd, row(bx), row(jax.nn.log_sigmoid(lam)))
    return hs, hlast.reshape(B, D_RNN), cbuf


def _split2(a):
    hi = a.astype(jnp.bfloat16)
    return hi, (a - hi.astype(jnp.float32)).astype(jnp.bfloat16)


def _dot_hi(a, b):
    a1, a2 = _split2(a)
    b1, b2 = _split2(b)
    d = lambda x, y: jnp.dot(x, y, preferred_element_type=jnp.float32)
    return d(a1, b1) + (d(a1, b2) + d(a2, b1))


def _unit_lower_solve(n_mats, rhss):
    C, n = rhss[0].shape
    f32 = jnp.float32
    ci = lax.broadcasted_iota(jnp.int32, (C, C), 0)
    ei = lax.broadcasted_iota(jnp.int32, (C, C), 1)
    same = (ci // SUBLANES) == (ei // SUBLANES)
    eye = jnp.where(ci == ei, 1.0, 0.0)
    diags = [jnp.where(same, m, 0.0) for m in n_mats]
    offs = [jnp.where(same, 0.0, m) for m in n_mats]
    d2 = [_dot_hi(d, d) for d in diags]
    d4 = [_dot_hi(d, d) for d in d2]
    part = [_dot_hi(eye - d, eye + s) for d, s in zip(diags, d2)]
    dinv = [_dot_hi(p, eye + s) for p, s in zip(part, d4)]
    solved = [[] for _ in rhss]
    for i in range(C // SUBLANES):
        rows = slice(i * SUBLANES, (i + 1) * SUBLANES)
        ys = [r[rows] for r in rhss]
        if i:
            below = jnp.zeros((C - i * SUBLANES, n), f32)
            ys = [y - _dot_hi(off[rows], jnp.concatenate(done + [below], axis=0))
                  for y, off, done in zip(ys, offs, solved)]
        above = [jnp.zeros((i * SUBLANES, n), f32)] if i else []
        rest = [jnp.zeros((C - (i + 1) * SUBLANES, n), f32)] if (i + 1) * SUBLANES < C else []
        xs = [_dot_hi(inv[rows], jnp.concatenate(above + [y] + rest, axis=0)) for inv, y in zip(dinv, ys)]
        for done, x in zip(solved, xs):
            done.append(x)
    return [jnp.concatenate(done, axis=0) for done in solved]


def _dn_kernel(xq_ref, xk_ref, xv_ref, ab_ref, buf_ref, s0_ref, cw_ref, alog_ref, dtb_ref, onorm_ref,
               o_ref, s_out_ref, cbuf_ref, xe_ref, s_ref, *, n_in):
    f32, bf16 = jnp.float32, jnp.bfloat16
    C = DN_CHUNK
    n = pl.program_id(1)
    last = pl.num_programs(1) - 1
    tail = CONV_W - 1
    width = 3 * DN_WIDTH

    @pl.when(n == 0)
    def _():
        xe_ref[...] = jnp.zeros(xe_ref.shape, f32)
        xe_ref[SUBLANES - tail:SUBLANES, :] = buf_ref[...]
        s_ref[...] = s0_ref[...]

    xe_ref[SUBLANES:SUBLANES + n_in, 0:DN_WIDTH] = xq_ref[...]
    xe_ref[SUBLANES:SUBLANES + n_in, DN_WIDTH:2 * DN_WIDTH] = xk_ref[...]
    xe_ref[SUBLANES:SUBLANES + n_in, 2 * DN_WIDTH:width] = xv_ref[...]
    cw = cw_ref[...]
    y = jnp.zeros((C, width), f32)
    for j in range(CONV_W):
        y = y + cw[j:j + 1] * xe_ref[SUBLANES - tail + j:SUBLANES - tail + j + C, :]

    @pl.when(n == last)
    def _():
        cbuf_ref[...] = xe_ref[SUBLANES + n_in - tail:SUBLANES + n_in, :]

    xe_ref[0:SUBLANES, :] = xe_ref[n_in:n_in + SUBLANES, :]

    valid = lax.broadcasted_iota(jnp.int32, (C, 1), 0) < n_in
    y = jnp.where(valid, y * jax.nn.sigmoid(y), 0.0)
    ab = jnp.zeros((C, LANE), f32)
    if n_in == C:
        ab = ab_ref[...]
    else:
        ab = jnp.concatenate([ab_ref[...], jnp.zeros((C - n_in, LANE), f32)], axis=0)
    g_all = jnp.where(valid, -jnp.exp(alog_ref[...]) * jax.nn.softplus(ab + dtb_ref[...]), 0.0)
    beta_all = jnp.where(valid, jax.nn.sigmoid(ab), 0.0)

    ci = lax.broadcasted_iota(jnp.int32, (C, C), 0)
    ei = lax.broadcasted_iota(jnp.int32, (C, C), 1)
    tri = jnp.where(ei <= ci, 1.0, 0.0).astype(bf16)
    g1 = g_all.astype(bf16)
    r1 = g_all - g1.astype(f32)
    g2 = r1.astype(bf16)
    g3 = (r1 - g2.astype(f32)).astype(bf16)
    csum = lambda x: jnp.dot(tri, x, preferred_element_type=f32)
    G = csum(g1) + (csum(g2) + csum(g3))
    G_t = G.T
    g_last = G[C - 1:C, :]
    e_g = jnp.exp(G)
    e_rest = jnp.exp(g_last - G)
    e_last = jnp.exp(g_last)

    heads = range(DN_HEADS)
    mm = lambda a, b: jnp.dot(a, b, preferred_element_type=f32)
    col = lambda a, h: a[:, h:h + 1]
    head_lanes = lambda base, h: y[:, base + h * DN_HD:base + (h + 1) * DN_HD]
    qs = [head_lanes(0, h) for h in heads]
    ks = [head_lanes(DN_WIDTH, h) for h in heads]
    vs = [head_lanes(2 * DN_WIDTH, h) for h in heads]
    qs = [q * lax.rsqrt(jnp.sum(q * q, axis=-1, keepdims=True) + L2_EPS) * (DN_HD ** -0.5) for q in qs]
    ks = [k * lax.rsqrt(jnp.sum(k * k, axis=-1, keepdims=True) + L2_EPS) for k in ks]
    decays = [jnp.where(ei <= ci, jnp.exp(col(G, h) - G_t[h:h + 1, :]), 0.0) for h in heads]
    betas = [col(beta_all, DN_HEADS + h) for h in heads]
    kbs = [k.astype(bf16) for k in ks]
    kks = [_dot_t(kb, kb) for kb in kbs]
    qks = [_dot_t(q.astype(bf16), kb) for q, kb in zip(qs, kbs)]
    n_mats = [jnp.where(ei < ci, b * kk * d, 0.0) for b, kk, d in zip(betas, kks, decays)]
    qks = [(qk * d).astype(bf16) for qk, d in zip(qks, decays)]
    rhss = [jnp.concatenate([betas[h] * vs[h], betas[h] * ks[h] * col(e_g, h)], axis=1) for h in heads]
    xs = _unit_lower_solve(n_mats, rhss)
    states = [s_ref[h] for h in heads]
    sbs = [S.astype(bf16) for S in states]
    v_news = [x[:, :DN_HD] - mm(x[:, DN_HD:].astype(bf16), sb) for x, sb in zip(xs, sbs)]
    vbs = [v.astype(bf16) for v in v_news]
    o_state = [mm((qs[h] * col(e_g, h)).astype(bf16), sbs[h]) for h in heads]
    o_local = [mm(qk, vb) for qk, vb in zip(qks, vbs)]
    grow = [lax.dot_general((ks[h] * col(e_rest, h)).astype(bf16), vbs[h], (((0,), (0,)), ((), ())),
                            preferred_element_type=f32) for h in heads]
    o_gain = onorm_ref[...]
    for h in heads:
        s_ref[h] = col(e_last, h) * states[h] + grow[h]
        o = o_state[h] + o_local[h]
        o = o * lax.rsqrt(jnp.mean(o * o, axis=-1, keepdims=True) + RMS_EPS) * o_gain
        o_ref[:, h * DN_HD:(h + 1) * DN_HD] = o[:n_in]

    @pl.when(n == last)
    def _():
        s_out_ref[...] = s_ref[...]


def _gated_delta_net(proj, B, T, buf, S0, conv_w, A_log, dt_bias, o_norm):
    n_in = min(DN_CHUNK, T)
    assert T % n_in == 0 and n_in % SUBLANES == 0
    qkv0 = _SEG_DST["dn_qkv"] // DN_WIDTH
    assert _SEG_DST["dn_qkv"] % DN_WIDTH == 0 and _SEG_DST["dn_b"] == _SEG_DST["dn_a"] + DN_HEADS
    ab_blk = _SEG_DST["dn_a"] // LANE
    proj3 = proj.reshape(B, T, IN_COLS_PAD)
    col = lambda c: (lambda b, n: (b, n, c))
    per_b = lambda b, n: (b, 0, 0)
    const = lambda b, n: (0, 0)
    pad_row = lambda a, off: jnp.zeros((1, LANE), jnp.float32).at[0, off:off + DN_HEADS].set(a)
    o, s_out, cbuf = pl.pallas_call(
        functools.partial(_dn_kernel, n_in=n_in),
        out_shape=(jax.ShapeDtypeStruct((B, T, DN_WIDTH), jnp.float32),
                   jax.ShapeDtypeStruct((B, DN_HEADS, DN_HD, DN_HD), jnp.float32),
                   jax.ShapeDtypeStruct((B, CONV_W - 1, 3 * DN_WIDTH), jnp.float32)),
        grid=(B, T // n_in),
        in_specs=[
            pl.BlockSpec((None, n_in, DN_WIDTH), col(qkv0)),
            pl.BlockSpec((None, n_in, DN_WIDTH), col(qkv0 + 1)),
            pl.BlockSpec((None, n_in, DN_WIDTH), col(qkv0 + 2)),
            pl.BlockSpec((None, n_in, LANE), col(ab_blk)),
            pl.BlockSpec((None, CONV_W - 1, 3 * DN_WIDTH), per_b),
            pl.BlockSpec((None, DN_HEADS, DN_HD, DN_HD), lambda b, n: (b, 0, 0, 0)),
            pl.BlockSpec((CONV_W, 3 * DN_WIDTH), const),
            pl.BlockSpec((1, LANE), const),
            pl.BlockSpec((1, LANE), const),
            pl.BlockSpec((1, DN_HD), const),
        ],
        out_specs=(pl.BlockSpec((None, n_in, DN_WIDTH), lambda b, n: (b, n, 0)),
                   pl.BlockSpec((None, DN_HEADS, DN_HD, DN_HD), lambda b, n: (b, 0, 0, 0)),
                   pl.BlockSpec((None, CONV_W - 1, 3 * DN_WIDTH), per_b)),
        scratch_shapes=[pltpu.VMEM((DN_CHUNK + 2 * SUBLANES, 3 * DN_WIDTH), jnp.float32),
                        pltpu.VMEM((DN_HEADS, DN_HD, DN_HD), jnp.float32)],
        compiler_params=pltpu.CompilerParams(dimension_semantics=("parallel", "arbitrary")),
        name="gated_delta",
    )(proj3, proj3, proj3, proj3, buf, S0, conv_w, pad_row(A_log, 0), pad_row(dt_bias, 0),
      o_norm.reshape(1, DN_HD))
    return o, s_out, cbuf


def _rmsnorm(x, g):
    xf = x.astype(jnp.float32)
    y = xf * lax.rsqrt(jnp.mean(xf * xf, axis=-1, keepdims=True) + RMS_EPS)
    return y * g.astype(jnp.float32)


def _rope(x, pos):
    half = ROT_DIM // 2
    inv = ROPE_THETA ** (-jnp.arange(half, dtype=jnp.float32) * 2.0 / ROT_DIM)
    ang = pos.astype(jnp.float32)[:, None] * inv[None, :]
    shape = (1, pos.shape[0]) + (1,) * (x.ndim - 3) + (half,)
    c = jnp.cos(ang).reshape(shape)
    s = jnp.sin(ang).reshape(shape)
    xf = x.astype(jnp.float32)
    x1, x2, rest = xf[..., :half], xf[..., half:ROT_DIM], xf[..., ROT_DIM:]
    return jnp.concatenate([x1 * c - x2 * s, x2 * c + x1 * s, rest], axis=-1)


def _seg_slot(proj, name, B, T):
    s = _SEG_DST[name]
    return proj[:, s:s + LANE].reshape(B, T, LANE)


def _seg(proj, name, B, T):
    s = _SEG_DST[name]
    return proj[:, s:s + _SEG_LEN[name]].reshape(B, T, _SEG_LEN[name])


def _layer(x, paged, lru_h0, lru_buf, dn_S0, dn_buf,
           norm_g, w_in_pad, lru_conv_w, lru_conv_b, lru_wa, lru_ba, lru_wx, lru_bx, lru_lam,
           q_norm, k_norm, dn_conv_w, dn_A_log, dn_dt_bias, dn_o_norm,
           w_lru_out, w_nsa_out, w_dn_out, w_out):
    f32 = jnp.float32
    B, T, _ = x.shape
    P = 0 if paged is None else paged[0].shape[1] * PAGE_SIZE
    pos = P + jnp.arange(T, dtype=jnp.int32)
    x2d = x.reshape(B * T, D_MODEL)
    proj = _in_proj(x2d, norm_g, w_in_pad)
    nsa_q, nsa_kv = _seg(proj, "nsa_q", B, T), _seg(proj, "nsa_kv", B, T)

    lru_seq, lru_h, lru_buf_new = _lru(proj, B, T, lru_buf, lru_h0, lru_conv_w, lru_conv_b,
                                       lru_wa, lru_ba, lru_wx, lru_bx, lru_lam)

    q = _rope(_rmsnorm(nsa_q.reshape(B, T, NSA_HEADS, NSA_HD), q_norm), pos) * (NSA_HD ** -0.5)
    kv = nsa_kv.reshape(B, T, 2 * N_BRANCH, NSA_KV, NSA_HD)
    k = _rope(_rmsnorm(kv[:, :, 0::2], k_norm[:, None, :]), pos).astype(x.dtype)
    v = kv[:, :, 1::2]
    rows = jnp.stack([k[:, :, 0], v[:, :, 0], k[:, :, 1], v[:, :, 1]], axis=2)
    win_rows = jnp.stack([k[:, :, 2], v[:, :, 2]], axis=2)
    flat = lambda a: a.reshape(B, T, KV_LANES)
    bf = lambda a: flat(a).astype(jnp.bfloat16)
    gate = _seg_slot(proj, "nsa_g", B, T)
    if paged is None:
        o_nsa = _nsa_prompt(q.reshape(B, T, NSA_WIDTH), gate,
                            flat(k[:, :, 0]), flat(v[:, :, 0]), bf(k[:, :, 1]), bf(v[:, :, 1]),
                            bf(k[:, :, 2]), bf(v[:, :, 2]))
        win_state = win_rows[:, -min(WINDOW, T):]
    else:
        page_table, cache, layer, win_prev = paged
        new = lambda a: jnp.pad(bf(a), ((0, 0), (0, NEW_PAD - T), (0, 0)))
        o_nsa = _nsa_sample(page_table, cache, layer, q.reshape(B, T, NSA_WIDTH), gate,
                            new(k[:, :, 1]), new(v[:, :, 1]),
                            win_prev.reshape(B, win_prev.shape[1], 2 * KV_LANES),
                            new(k[:, :, 2]), new(v[:, :, 2]))
        win_state = jnp.concatenate([win_prev.astype(win_rows.dtype), win_rows], axis=1)[:, -WINDOW:]

    o_dn, dn_S, dn_buf_new = _gated_delta_net(proj, B, T, dn_buf, dn_S0, dn_conv_w, dn_A_log, dn_dt_bias,
                                              dn_o_norm)

    y2d = _out_stage(x2d, proj, lru_seq.reshape(B * T, D_RNN), o_nsa.reshape(B * T, NSA_WIDTH),
                     o_dn.reshape(B * T, DN_WIDTH), w_lru_out, w_nsa_out, w_dn_out, w_out)
    return y2d.reshape(B, T, D_MODEL), (rows, win_state, lru_h, lru_buf_new, dn_S, dn_buf_new)


def kernel(x_prompt, x_sample, cache_nsa_kv, page_table, state_nsa_win, state_lru_h, state_lru_conv,
           state_dn_S, state_dn_conv, norm_gain, w_in, lru_conv_w, lru_conv_b, lru_wa, lru_ba,
           lru_wx, lru_bx, lru_lambda, nsa_q_norm, nsa_k_norm, dn_conv_w, dn_A_log, dn_dt_bias,
           dn_o_norm, w_lru_out, w_nsa_out, w_dn_out, w_out):
    Bp = x_prompt.shape[0]
    dt = x_prompt.dtype
    bf16 = jnp.bfloat16
    cache = cache_nsa_kv.reshape(cache_nsa_kv.shape[:2] + (PAGE_SIZE, 4 * KV_LANES))
    p_h0 = jnp.zeros((Bp, D_RNN), jnp.float32)
    p_lbuf0 = jnp.zeros((Bp, CONV_W - 1, D_RNN), dt)
    p_S0 = jnp.zeros((Bp, DN_HEADS, DN_HD, DN_HD), jnp.float32)
    p_dbuf0 = jnp.zeros((Bp, CONV_W - 1, 3 * DN_WIDTH), dt)
    wa_bd = jax.vmap(_block_diag)(lru_wa).astype(bf16)
    wx_bd = jax.vmap(_block_diag)(lru_wx).astype(bf16)
    weights = (norm_gain, _pad_in_weight(w_in), lru_conv_w, lru_conv_b, wa_bd, lru_ba, wx_bd, lru_bx,
               lru_lambda, nsa_q_norm, nsa_k_norm, dn_conv_w, dn_A_log, dn_dt_bias, dn_o_norm,
               w_lru_out.astype(bf16), _heads_rg(w_nsa_out.swapaxes(1, 2)).swapaxes(1, 2).astype(bf16),
               w_dn_out.astype(bf16), w_out.astype(bf16))
    xp, xs = x_prompt, x_sample
    st_p, st_s = [], []
    for l in range(DEPTH):
        lw = [w[l] for w in weights]
        xp, sp = _layer(xp, None, p_h0, p_lbuf0, p_S0, p_dbuf0, *lw)
        xs, ss = _layer(xs, (page_table, cache, l, state_nsa_win[l]), state_lru_h[l], state_lru_conv[l],
                        state_dn_S[l], state_dn_conv[l], *lw)
        st_p.append(sp)
        st_s.append(ss)
    kv_p, win_p, lh_p, lc_p, S_p, dc_p = [jnp.stack(a) for a in zip(*st_p)]
    kv_s, win_s, lh_s, lc_s, S_s, dc_s = [jnp.stack(a) for a in zip(*st_s)]
    return (xp, xs, kv_p, kv_s, win_p, win_s, lh_p, lh_s, lc_p, lc_s, S_p, S_s, dc_p, dc_s)
```

```python
import functools
import math

import numpy as np
import jax
import jax.numpy as jnp
from jax import lax
from jax.experimental import pallas as pl
from jax.experimental.pallas import tpu as pltpu

D_MODEL = 1024
DEPTH = 4
PAGE_SIZE = 128
CONV_W = 4
RMS_EPS = 1e-6
L2_EPS = 1e-6
N_BRANCH = 3
D_RNN = D_MODEL // 2
LRU_BLOCKS = 8
LRU_BS = D_RNN // LRU_BLOCKS
LRU_C = 8.0
NSA_HEADS = 8
NSA_HD = 64
NSA_KV = 2
NSA_REP = NSA_HEADS // NSA_KV
NSA_WIDTH = NSA_HEADS * NSA_HD
CMP_BLOCK = 32
SLC_BLOCK = 64
N_SEL = 16
WINDOW = 512
Q_BLOCK = 128
ROT_DIM = NSA_HD // 4
ROPE_THETA = 500000.0
DN_HEADS = 4
DN_HD = 128
DN_WIDTH = DN_HEADS * DN_HD
DN_CHUNK = 64

LANE = 128

_SEG_NAMES = ("lru_x", "lru_z", "nsa_q", "nsa_kv", "nsa_z", "nsa_g", "dn_qkv", "dn_z", "dn_a", "dn_b", "merge_g")
_SEG_SIZES = (D_RNN, D_RNN, NSA_WIDTH, 2 * N_BRANCH * NSA_KV * NSA_HD, NSA_WIDTH, N_BRANCH * NSA_HEADS,
              3 * DN_WIDTH, DN_WIDTH, DN_HEADS, DN_HEADS, N_BRANCH * D_MODEL)
_SEG_SRC = dict(zip(_SEG_NAMES, np.concatenate([[0], np.cumsum(_SEG_SIZES)[:-1]]).tolist()))
_SEG_LEN = dict(zip(_SEG_NAMES, _SEG_SIZES))
_DST_SLOTS = (
    (("lru_x",), D_RNN), (("lru_z",), D_RNN), (("nsa_q",), NSA_WIDTH), (("nsa_z",), NSA_WIDTH),
    (("dn_z",), DN_WIDTH), (("nsa_g",), LANE), (("dn_a", "dn_b"), LANE),
    (("nsa_kv",), 2 * N_BRANCH * NSA_KV * NSA_HD), (("dn_qkv",), 3 * DN_WIDTH), (("merge_g",), N_BRANCH * D_MODEL),
)


def _dst_layout():
    off, dst = 0, {}
    for names, width in _DST_SLOTS:
        o = off
        for n in names:
            dst[n] = o
            o += _SEG_LEN[n]
        off += width
    return dst, off


_SEG_DST, IN_COLS_PAD = _dst_layout()


def _heads_rg(a):
    lead = a.shape[:-1]
    return a.reshape(lead + (NSA_KV, NSA_REP, NSA_HD)).swapaxes(-3, -2).reshape(lead + (NSA_WIDTH,))


def _pad_in_weight(w_in):
    pieces = []
    for names, width in _DST_SLOTS:
        used = 0
        for n in names:
            piece = w_in[:, :, _SEG_SRC[n]:_SEG_SRC[n] + _SEG_LEN[n]]
            pieces.append(_heads_rg(piece) if n in ("nsa_q", "nsa_z") else piece)
            used += _SEG_LEN[n]
        if used < width:
            pieces.append(jnp.zeros(w_in.shape[:2] + (width - used,), w_in.dtype))
    return jnp.concatenate(pieces, axis=-1).astype(jnp.bfloat16)


IN_PROJ_ROWS = 2048
IN_PROJ_COLS = 512


def _in_proj_kernel(x_ref, g_ref, w_ref, o_ref, h_ref):
    @pl.when(pl.program_id(1) == 0)
    def _():
        x = x_ref[...]
        ms = jnp.mean(x * x, axis=-1, keepdims=True)
        h_ref[...] = (x * lax.rsqrt(ms + RMS_EPS) * g_ref[...]).astype(jnp.bfloat16)

    o_ref[...] = jnp.dot(h_ref[...], w_ref[...], preferred_element_type=jnp.float32)


def _in_proj(x2d, gain, w_pad):
    m = x2d.shape[0]
    tm = min(IN_PROJ_ROWS, m)
    tn = IN_PROJ_COLS
    vmem = 2 * (tm * D_MODEL * 4 + D_MODEL * tn * 2 + tm * tn * 4) + tm * D_MODEL * 2 + (4 << 20)
    return pl.pallas_call(
        _in_proj_kernel,
        out_shape=jax.ShapeDtypeStruct((m, IN_COLS_PAD), jnp.float32),
        grid=(m // tm, IN_COLS_PAD // tn),
        in_specs=[
            pl.BlockSpec((tm, D_MODEL), lambda i, j: (i, 0)),
            pl.BlockSpec((1, D_MODEL), lambda i, j: (0, 0)),
            pl.BlockSpec((D_MODEL, tn), lambda i, j: (0, j)),
        ],
        out_specs=pl.BlockSpec((tm, tn), lambda i, j: (i, j)),
        scratch_shapes=[pltpu.VMEM((tm, D_MODEL), jnp.bfloat16)],
        compiler_params=pltpu.CompilerParams(dimension_semantics=("parallel", "arbitrary"),
                                             vmem_limit_bytes=vmem),
        name="in_proj",
    )(x2d, gain.reshape(1, D_MODEL), w_pad)


def _silu(z):
    return z * jax.nn.sigmoid(z)


def _out_stage_kernel(x_ref, a_lru, z_lru, a_nsa, z_nsa, a_dn, z_dn, mg0, mg1, mg2,
                      w_lru, w_nsa, w_dn, w_out, o_ref):
    def branch(a, z, w):
        y = (a[...] * _silu(z[...])).astype(jnp.bfloat16)
        return jnp.dot(y, w[...], preferred_element_type=jnp.float32)

    merged = jax.nn.sigmoid(mg0[...]) * branch(a_lru, z_lru, w_lru)
    merged = merged + jax.nn.sigmoid(mg1[...]) * branch(a_nsa, z_nsa, w_nsa)
    merged = merged + jax.nn.sigmoid(mg2[...]) * branch(a_dn, z_dn, w_dn)
    y = jnp.dot(merged.astype(jnp.bfloat16), w_out[...], preferred_element_type=jnp.float32)
    o_ref[...] = x_ref[...] + y


def _out_stage(x2d, proj, a_lru, a_nsa, a_dn, w_lru, w_nsa, w_dn, w_out):
    m = x2d.shape[0]
    tm = min(256, m)
    half = D_RNN
    row = lambda i: (i, 0)
    col = lambda c: (lambda i: (i, c))
    full = lambda i: (0, 0)
    mg_blk = _SEG_DST["merge_g"] // D_MODEL
    return pl.pallas_call(
        _out_stage_kernel,
        out_shape=jax.ShapeDtypeStruct((m, D_MODEL), jnp.float32),
        grid=(m // tm,),
        in_specs=[
            pl.BlockSpec((tm, D_MODEL), row),
            pl.BlockSpec((tm, half), row),
            pl.BlockSpec((tm, half), col(_SEG_DST["lru_z"] // half)),
            pl.BlockSpec((tm, half), row),
            pl.BlockSpec((tm, half), col(_SEG_DST["nsa_z"] // half)),
            pl.BlockSpec((tm, half), row),
            pl.BlockSpec((tm, half), col(_SEG_DST["dn_z"] // half)),
            pl.BlockSpec((tm, D_MODEL), col(mg_blk)),
            pl.BlockSpec((tm, D_MODEL), col(mg_blk + 1)),
            pl.BlockSpec((tm, D_MODEL), col(mg_blk + 2)),
            pl.BlockSpec((half, D_MODEL), full),
            pl.BlockSpec((half, D_MODEL), full),
            pl.BlockSpec((half, D_MODEL), full),
            pl.BlockSpec((D_MODEL, D_MODEL), full),
        ],
        out_specs=pl.BlockSpec((tm, D_MODEL), row),
        compiler_params=pltpu.CompilerParams(dimension_semantics=("parallel",)),
        name="out_stage",
    )(x2d, a_lru, proj, a_nsa, proj, a_dn, proj, proj, proj, proj, w_lru, w_nsa, w_dn, w_out)


NEG_BIG = -1e30
SLC_CHUNK = 512
N_ROWS = NSA_KV * NSA_REP * Q_BLOCK
KV_LANES = NSA_KV * NSA_HD
WIN_KEYS = WINDOW + Q_BLOCK
MAX_SLC_BLOCKS = LANE // 2


def _dot_t(a, b):
    return lax.dot_general(a, b, (((1,), (1,)), ((), ())), preferred_element_type=jnp.float32)


def _select_blocks(imp_t, cur, n_top):
    nb = imp_t.shape[0]
    jj = lax.broadcasted_iota(jnp.int32, imp_t.shape, 0)
    v = jnp.where((jj == 0) | (jj == cur), jnp.inf, jnp.where(jj > cur, -jnp.inf, imp_t))
    sub = lax.broadcasted_iota(jnp.int32, (8, imp_t.shape[1]), 0)
    ranks = []
    for a in range(nb // 8):
        va = v[8 * a:8 * a + 8]
        rank = jnp.zeros(va.shape, jnp.float32)
        for j in range(nb):
            row = v[j:j + 1]
            ge = jnp.where(row >= va, 1.0, 0.0)
            gt = jnp.where(row > va, 1.0, 0.0)
            if j < 8 * a:
                rank = rank + ge
            elif j >= 8 * a + 8:
                rank = rank + gt
            else:
                rank = rank + jnp.where(sub > (j - 8 * a), ge, gt)
        ranks.append(rank)
    rank = jnp.concatenate(ranks, axis=0)
    return jnp.where((rank < n_top) & (jj <= cur), 1.0, 0.0)


def _nsa_prompt_kernel(q_ref, gate_ref, kc_ref, vc_ref, ks_ref, vs_ref, kw_ref, vw_ref, o_ref,
                       kcb_ref, vcb_ref, m_ref, l_ref, acc_ref, *, seq_len, n_top):
    f32, bf16 = jnp.float32, jnp.bfloat16
    i = pl.program_id(1)
    nsb = seq_len // SLC_BLOCK
    half = MAX_SLC_BLOCKS

    @pl.when(i == 0)
    def _():
        if nsb < half:
            kcb_ref[...] = jnp.zeros(kcb_ref.shape, f32)
            vcb_ref[...] = jnp.zeros(vcb_ref.shape, f32)
        for src, dst in ((kc_ref, kcb_ref), (vc_ref, vcb_ref)):
            x = src[...].reshape(nsb, SLC_BLOCK, KV_LANES)
            dst[0:nsb, :] = jnp.sum(x[:, :CMP_BLOCK, :], axis=1) * (1.0 / CMP_BLOCK)
            dst[half:half + nsb, :] = jnp.sum(x[:, CMP_BLOCK:, :], axis=1) * (1.0 / CMP_BLOCK)

    lane = lax.broadcasted_iota(jnp.int32, (Q_BLOCK, LANE), 1)
    tq = lax.broadcasted_iota(jnp.int32, (Q_BLOCK, LANE), 0)
    pos = i * Q_BLOCK + tq
    low = lane < NSA_HD

    q = q_ref[...]
    parts = []
    for g in range(NSA_KV):
        for r in range(NSA_REP):
            qr = q[:, r * KV_LANES:(r + 1) * KV_LANES]
            parts.append(jnp.where(low if g == 0 else ~low, qr, 0.0))
    qpad = jnp.concatenate(parts, axis=0).astype(bf16)

    s = _dot_t(qpad, kcb_ref[...].astype(bf16))
    cblk = jnp.where(lane < half, 2 * lane, 2 * lane - (2 * half - 1))
    okc = (cblk * CMP_BLOCK + (CMP_BLOCK - 1)) <= pos
    s3 = s.reshape(NSA_KV * NSA_REP, Q_BLOCK, LANE) + jnp.where(okc, 0.0, NEG_BIG)[None]
    mx = jnp.max(s3, axis=-1, keepdims=True)
    e = jnp.where(okc[None], jnp.exp(s3 - mx), 0.0)
    den = jnp.sum(e, axis=-1, keepdims=True)
    p3 = e / jnp.where(den > 0.0, den, 1.0)
    o_cmp = jnp.dot(p3.reshape(N_ROWS, LANE).astype(bf16), vcb_ref[...].astype(bf16),
                    preferred_element_type=f32)

    pg = p3.reshape(NSA_KV, NSA_REP, Q_BLOCK, LANE).sum(axis=1)
    cur_t = (i * Q_BLOCK + lax.broadcasted_iota(jnp.int32, (half, Q_BLOCK), 1)) // SLC_BLOCK
    selq = []
    for g in range(NSA_KV):
        imp = pg[g] + pltpu.roll(pg[g], half, axis=1)
        sel_t = _select_blocks(imp.T[:half], cur_t, n_top)
        sel_full = jnp.concatenate([sel_t, jnp.zeros_like(sel_t)], axis=0)
        selq.append(sel_full.T[:, :half].astype(bf16))

    m_ref[...] = jnp.full(m_ref.shape, NEG_BIG, f32)
    l_ref[...] = jnp.zeros(l_ref.shape, f32)
    acc_ref[...] = jnp.zeros(acc_ref.shape, f32)
    heads = (NSA_KV, NSA_REP, Q_BLOCK)

    def slc_step(c, carry):
        start = pl.multiple_of(c * SLC_CHUNK, SLC_CHUNK)
        kch = ks_ref[pl.ds(start, SLC_CHUNK), :]
        vch = vs_ref[pl.ds(start, SLC_CHUNK), :]
        sc = _dot_t(qpad, kch)
        kidx = start + lax.broadcasted_iota(jnp.int32, (Q_BLOCK, SLC_CHUNK), 1)
        causal = kidx <= i * Q_BLOCK + lax.broadcasted_iota(jnp.int32, (Q_BLOCK, SLC_CHUNK), 0)
        kblk = (start + lax.broadcasted_iota(jnp.int32, (half, SLC_CHUNK), 1)) // SLC_BLOCK
        expand = jnp.where(kblk == lax.broadcasted_iota(jnp.int32, (half, SLC_CHUNK), 0), 1.0, 0.0).astype(bf16)
        bias = []
        for g in range(NSA_KV):
            picked = jnp.dot(selq[g], expand, preferred_element_type=f32) > 0.5
            bias.append(jnp.where(picked & causal, 0.0, NEG_BIG))
        bias = jnp.stack(bias, axis=0)[:, None]
        s4 = sc.reshape(heads + (SLC_CHUNK,)) + bias
        m_old = m_ref[...].reshape(heads + (1,))
        m_new = jnp.maximum(m_old, jnp.max(s4, axis=-1, keepdims=True))
        alpha = jnp.exp(m_old - m_new)
        ex = jnp.exp(s4 - m_new)
        l_new = alpha * l_ref[...].reshape(heads + (1,)) + jnp.sum(ex, axis=-1, keepdims=True)
        pv = jnp.dot(ex.reshape(N_ROWS, SLC_CHUNK).astype(bf16), vch, preferred_element_type=f32)
        acc_ref[...] = alpha.reshape(N_ROWS, 1) * acc_ref[...] + pv
        l_ref[...] = l_new.reshape(N_ROWS, 1)
        m_ref[...] = m_new.reshape(N_ROWS, 1)
        return carry

    n_chunks = (i * Q_BLOCK + Q_BLOCK + SLC_CHUNK - 1) // SLC_CHUNK
    lax.fori_loop(0, n_chunks, slc_step, 0)
    o_slc = acc_ref[...] / l_ref[...]

    wstart = pl.multiple_of(jnp.maximum(i - WINDOW // Q_BLOCK, 0) * Q_BLOCK, Q_BLOCK)
    kwin = kw_ref[pl.ds(wstart, WIN_KEYS), :]
    vwin = vw_ref[pl.ds(wstart, WIN_KEYS), :]
    sw = _dot_t(qpad, kwin)
    widx = wstart + lax.broadcasted_iota(jnp.int32, (Q_BLOCK, WIN_KEYS), 1)
    wpos = i * Q_BLOCK + lax.broadcasted_iota(jnp.int32, (Q_BLOCK, WIN_KEYS), 0)
    okw = (widx <= wpos) & (wpos - widx < WINDOW)
    sw3 = sw.reshape(NSA_KV * NSA_REP, Q_BLOCK, WIN_KEYS) + jnp.where(okw, 0.0, NEG_BIG)[None]
    ew = jnp.exp(sw3 - jnp.max(sw3, axis=-1, keepdims=True))
    lw = jnp.sum(ew, axis=-1, keepdims=True).reshape(N_ROWS, 1)
    o_win = jnp.dot(ew.reshape(N_ROWS, WIN_KEYS).astype(bf16), vwin, preferred_element_type=f32) / lw

    gate = jax.nn.sigmoid(gate_ref[...])
    n_hd = NSA_KV * NSA_REP
    for r in range(NSA_REP):
        per_g = []
        for g in range(NSA_KV):
            h = g * NSA_REP + r
            rows = slice(h * Q_BLOCK, (h + 1) * Q_BLOCK)
            per_g.append(gate[:, h:h + 1] * o_cmp[rows]
                         + gate[:, n_hd + h:n_hd + h + 1] * o_slc[rows]
                         + gate[:, 2 * n_hd + h:2 * n_hd + h + 1] * o_win[rows])
        o_ref[:, r * KV_LANES:(r + 1) * KV_LANES] = jnp.where(low, per_g[0], per_g[1])


def _nsa_prompt(q, gate, kc, vc, ks, vs, kw, vw):
    B, T, _ = q.shape
    assert T % SLC_CHUNK == 0 and T >= WIN_KEYS and T // SLC_BLOCK <= MAX_SLC_BLOCKS
    n_top = min(N_SEL, T // SLC_BLOCK)
    blk = lambda b, i: (b, i, 0)
    whole = lambda b, i: (b, 0, 0)
    kv_spec = pl.BlockSpec((None, T, KV_LANES), whole)
    return pl.pallas_call(
        functools.partial(_nsa_prompt_kernel, seq_len=T, n_top=n_top),
        out_shape=jax.ShapeDtypeStruct((B, T, NSA_WIDTH), jnp.float32),
        grid=(B, T // Q_BLOCK),
        in_specs=[
            pl.BlockSpec((None, Q_BLOCK, NSA_WIDTH), blk),
            pl.BlockSpec((None, Q_BLOCK, LANE), blk),
            kv_spec, kv_spec, kv_spec, kv_spec, kv_spec, kv_spec,
        ],
        out_specs=pl.BlockSpec((None, Q_BLOCK, NSA_WIDTH), blk),
        scratch_shapes=[
            pltpu.VMEM((LANE, KV_LANES), jnp.float32),
            pltpu.VMEM((LANE, KV_LANES), jnp.float32),
            pltpu.VMEM((N_ROWS, 1), jnp.float32),
            pltpu.VMEM((N_ROWS, 1), jnp.float32),
            pltpu.VMEM((N_ROWS, KV_LANES), jnp.float32),
        ],
        compiler_params=pltpu.CompilerParams(
            dimension_semantics=("parallel", "arbitrary"), vmem_limit_bytes=48 * 1024 * 1024),
        name="nsa_prompt",
    )(q, gate, kc, vc, ks, vs, kw, vw)


PAGES_PER_STEP = 8
NEW_PAD = 128


def _qpad_rows(q, nq):
    low = lax.broadcasted_iota(jnp.int32, (nq, LANE), 1) < NSA_HD
    parts = []
    for g in range(NSA_KV):
        for r in range(NSA_REP):
            qr = q[:, r * KV_LANES:(r + 1) * KV_LANES]
            parts.append(jnp.where(low if g == 0 else ~low, qr, 0.0))
    return jnp.concatenate(parts, axis=0).astype(jnp.bfloat16)


def _nsa_sample_kernel(pt_ref, *refs, n_q, past_len, n_top):
    f32, bf16 = jnp.float32, jnp.bfloat16
    pages = refs[:PAGES_PER_STEP]
    (q_ref, gate_ref, ksn_ref, vsn_ref, win_ref, kwn_ref, vwn_ref, o_ref,
     kcb_ref, vcb_ref, sel_ref, ocmp_ref, m_ref, l_ref, acc_ref) = refs[PAGES_PER_STEP:]
    ph = pl.program_id(1)
    p = pl.program_id(2)
    last = pl.num_programs(2) - 1
    n_rows = NSA_KV * NSA_REP * n_q
    n_blk = past_len // SLC_BLOCK
    cmp_per_step = PAGES_PER_STEP * PAGE_SIZE // CMP_BLOCK
    keys_per_step = PAGES_PER_STEP * PAGE_SIZE
    heads = (NSA_KV, NSA_REP, n_q)
    qpad = _qpad_rows(q_ref[...], n_q)

    @pl.when(ph == 0)
    def _():
        means = [pg[...].reshape(PAGE_SIZE // CMP_BLOCK, CMP_BLOCK, 2 * KV_LANES).sum(axis=1) * (1.0 / CMP_BLOCK)
                 for pg in pages]
        means = jnp.concatenate(means, axis=0)
        row0 = pl.multiple_of(p * cmp_per_step, cmp_per_step)
        kcb_ref[pl.ds(row0, cmp_per_step), :] = means[:, :KV_LANES]
        vcb_ref[pl.ds(row0, cmp_per_step), :] = means[:, KV_LANES:]

    @pl.when((ph == 0) & (p == last))
    def _():
        halves = []
        for par in range(2):
            kc = kcb_ref[pl.ds(par, n_blk, stride=2), :].astype(bf16)
            halves.append(_dot_t(qpad, kc))
        mx = jnp.maximum(jnp.max(halves[0], axis=-1, keepdims=True), jnp.max(halves[1], axis=-1, keepdims=True))
        e0, e1 = jnp.exp(halves[0] - mx), jnp.exp(halves[1] - mx)
        inv = 1.0 / (jnp.sum(e0, axis=-1, keepdims=True) + jnp.sum(e1, axis=-1, keepdims=True))
        p0, p1 = e0 * inv, e1 * inv
        oc = jnp.zeros((n_rows, KV_LANES), f32)
        for par, pp in ((0, p0), (1, p1)):
            vc = vcb_ref[pl.ds(par, n_blk, stride=2), :].astype(bf16)
            oc = oc + jnp.dot(pp.astype(bf16), vc, preferred_element_type=f32)
        ocmp_ref[...] = oc
        imp = (p0 + p1).reshape(heads + (n_blk,)).sum(axis=1).reshape(NSA_KV * n_q, n_blk)
        lane = lax.broadcasted_iota(jnp.int32, imp.shape, 1)
        v = jnp.where(lane == 0, jnp.inf, imp)
        rank = jnp.zeros(imp.shape, f32)
        for j in range(n_blk):
            col = v[:, j:j + 1]
            tie = jnp.where(lane > j, 1.0, 0.0)
            rank = rank + jnp.where(col > v, 1.0, jnp.where(col == v, tie, 0.0))
        sel_ref[...] = jnp.where(rank < n_top - 1, 1.0, 0.0)
        m_ref[...] = jnp.full(m_ref.shape, NEG_BIG, f32)
        l_ref[...] = jnp.zeros(l_ref.shape, f32)
        acc_ref[...] = jnp.zeros(acc_ref.shape, f32)

    def online(s4, vals):
        m_old = m_ref[...].reshape(heads + (1,))
        m_new = jnp.maximum(m_old, jnp.max(s4, axis=-1, keepdims=True))
        alpha = jnp.exp(m_old - m_new)
        ex = jnp.exp(s4 - m_new)
        l_new = alpha * l_ref[...].reshape(heads + (1,)) + jnp.sum(ex, axis=-1, keepdims=True)
        pv = jnp.dot(ex.reshape(n_rows, s4.shape[-1]).astype(bf16), vals, preferred_element_type=f32)
        acc_ref[...] = alpha.reshape(n_rows, 1) * acc_ref[...] + pv
        l_ref[...] = l_new.reshape(n_rows, 1)
        m_ref[...] = m_new.reshape(n_rows, 1)

    @pl.when(ph == 1)
    def _():
        xs = [pg[...] for pg in pages]
        sc = jnp.concatenate([_dot_t(qpad, x[:, :KV_LANES].astype(bf16)) for x in xs], axis=-1)
        vals = jnp.concatenate([x[:, KV_LANES:].astype(bf16) for x in xs], axis=0)
        kblk = (p * keys_per_step + lax.broadcasted_iota(jnp.int32, (n_blk, keys_per_step), 1)) // SLC_BLOCK
        expand = jnp.where(kblk == lax.broadcasted_iota(jnp.int32, (n_blk, keys_per_step), 0), 1.0, 0.0).astype(bf16)
        picked = jnp.dot(sel_ref[...].astype(bf16), expand, preferred_element_type=f32) > 0.5
        bias = jnp.where(picked, 0.0, NEG_BIG).reshape(NSA_KV, 1, n_q, keys_per_step)
        online(sc.reshape(heads + (keys_per_step,)) + bias, vals)

    @pl.when((ph == 1) & (p == last))
    def _():
        tq = lax.broadcasted_iota(jnp.int32, (n_q, NEW_PAD), 0)
        tk = lax.broadcasted_iota(jnp.int32, (n_q, NEW_PAD), 1)
        new_bias = jnp.where(tk <= tq, 0.0, NEG_BIG)[None, None]
        s_new = _dot_t(qpad, ksn_ref[...]).reshape(heads + (NEW_PAD,)) + new_bias
        online(s_new, vsn_ref[...])
        o_slc = acc_ref[...] / l_ref[...]

        win = win_ref[...]
        n_prev = win.shape[0]
        s_prev = _dot_t(qpad, win[:, :KV_LANES].astype(bf16))
        pk = lax.broadcasted_iota(jnp.int32, (n_q, n_prev), 1)
        pq = lax.broadcasted_iota(jnp.int32, (n_q, n_prev), 0)
        s_prev = s_prev.reshape(heads + (n_prev,)) + jnp.where(pk > pq + (n_prev - WINDOW), 0.0, NEG_BIG)[None, None]
        s_wnew = _dot_t(qpad, kwn_ref[...]).reshape(heads + (NEW_PAD,)) + new_bias
        mw = jnp.maximum(jnp.max(s_prev, axis=-1, keepdims=True), jnp.max(s_wnew, axis=-1, keepdims=True))
        e_prev, e_new = jnp.exp(s_prev - mw), jnp.exp(s_wnew - mw)
        lw = jnp.sum(e_prev, axis=-1, keepdims=True) + jnp.sum(e_new, axis=-1, keepdims=True)
        o_win = (jnp.dot(e_prev.reshape(n_rows, n_prev).astype(bf16), win[:, KV_LANES:].astype(bf16),
                         preferred_element_type=f32)
                 + jnp.dot(e_new.reshape(n_rows, NEW_PAD).astype(bf16), vwn_ref[...], preferred_element_type=f32))
        o_win = o_win / lw.reshape(n_rows, 1)

        gate = jax.nn.sigmoid(gate_ref[...])
        low = lax.broadcasted_iota(jnp.int32, (n_q, LANE), 1) < NSA_HD
        o_cmp = ocmp_ref[...]
        n_hd = NSA_KV * NSA_REP
        for r in range(NSA_REP):
            per_g = []
            for g in range(NSA_KV):
                h = g * NSA_REP + r
                rows = slice(h * n_q, (h + 1) * n_q)
                per_g.append(gate[:, h:h + 1] * o_cmp[rows]
                             + gate[:, n_hd + h:n_hd + h + 1] * o_slc[rows]
                             + gate[:, 2 * n_hd + h:2 * n_hd + h + 1] * o_win[rows])
            o_ref[:, r * KV_LANES:(r + 1) * KV_LANES] = jnp.where(low, per_g[0], per_g[1])


def _nsa_sample(page_table, cache, layer, q, gate, ks_new, vs_new, win_prev, kw_new, vw_new):
    B, T, _ = q.shape
    n_pages = page_table.shape[1]
    past_len = n_pages * PAGE_SIZE
    n_prev = win_prev.shape[1]
    n_blk = past_len // SLC_BLOCK
    assert T % 8 == 0 and T < CMP_BLOCK and n_pages % PAGES_PER_STEP == 0 and n_blk % LANE == 0
    assert n_blk <= LANE and n_prev == WINDOW
    n_top = min(N_SEL, n_blk + 1)
    n_rows = NSA_KV * NSA_REP * T
    per_b = lambda b, ph, p, pt: (b, 0, 0)

    def page_spec(k):
        return pl.BlockSpec((None, None, PAGE_SIZE, 2 * KV_LANES),
                            lambda b, ph, p, pt: (layer, pt[b, p * PAGES_PER_STEP + k], 0, ph))

    new_spec = pl.BlockSpec((None, NEW_PAD, KV_LANES), per_b)
    grid_spec = pltpu.PrefetchScalarGridSpec(
        num_scalar_prefetch=1,
        grid=(B, 2, n_pages // PAGES_PER_STEP),
        in_specs=[page_spec(k) for k in range(PAGES_PER_STEP)] + [
            pl.BlockSpec((None, T, NSA_WIDTH), per_b),
            pl.BlockSpec((None, T, LANE), per_b),
            new_spec, new_spec,
            pl.BlockSpec((None, n_prev, 2 * KV_LANES), per_b),
            new_spec, new_spec,
        ],
        out_specs=pl.BlockSpec((None, T, NSA_WIDTH), per_b),
        scratch_shapes=[
            pltpu.VMEM((past_len // CMP_BLOCK, KV_LANES), jnp.float32),
            pltpu.VMEM((past_len // CMP_BLOCK, KV_LANES), jnp.float32),
            pltpu.VMEM((NSA_KV * T, n_blk), jnp.float32),
            pltpu.VMEM((n_rows, KV_LANES), jnp.float32),
            pltpu.VMEM((n_rows, 1), jnp.float32),
            pltpu.VMEM((n_rows, 1), jnp.float32),
            pltpu.VMEM((n_rows, KV_LANES), jnp.float32),
        ],
    )
    return pl.pallas_call(
        functools.partial(_nsa_sample_kernel, n_q=T, past_len=past_len, n_top=n_top),
        out_shape=jax.ShapeDtypeStruct((B, T, NSA_WIDTH), jnp.float32),
        grid_spec=grid_spec,
        compiler_params=pltpu.CompilerParams(dimension_semantics=("parallel", "arbitrary", "arbitrary")),
        name="nsa_sample",
    )(page_table, *([cache] * PAGES_PER_STEP), q, gate, ks_new, vs_new, win_prev, kw_new, vw_new)


LRU_TILE = 512
SUBLANES = 8


EXPM1_SERIES_RANGE = 0.35
_EXPM1_COEFFS = tuple(1.0 / math.factorial(n) for n in range(9, 1, -1))


def _one_minus_exp(y):
    acc = jnp.full_like(y, _EXPM1_COEFFS[0])
    for c in _EXPM1_COEFFS[1:]:
        acc = acc * y + c
    series = -(y * (1.0 + y * acc))
    return jnp.where(y > -EXPM1_SERIES_RANGE, series, 1.0 - jnp.exp(y))


def _lru_kernel(x_ref, buf_ref, h0_ref, cw_ref, cb_ref, wa_ref, ba_ref, wx_ref, bx_ref, lsl_ref,
                hs_ref, hlast_ref, cbuf_ref, xe_ref, h_ref, *, tile):
    f32 = jnp.float32
    t = pl.program_id(1)
    tail = CONV_W - 1

    @pl.when(t == 0)
    def _():
        xe_ref[0:SUBLANES, :] = jnp.zeros((SUBLANES, D_RNN), f32)
        xe_ref[SUBLANES - tail:SUBLANES, :] = buf_ref[...]
        h_ref[...] = h0_ref[...]

    xe_ref[SUBLANES:SUBLANES + tile, :] = x_ref[...]
    cw = cw_ref[...]
    u = cb_ref[...]
    for j in range(CONV_W):
        u = u + cw[j:j + 1] * xe_ref[SUBLANES - tail + j:SUBLANES - tail + j + tile, :]

    @pl.when(t == pl.num_programs(1) - 1)
    def _():
        cbuf_ref[...] = xe_ref[SUBLANES + tile - tail:SUBLANES + tile, :]

    xe_ref[0:SUBLANES, :] = xe_ref[tile:tile + SUBLANES, :]

    ub = u.astype(jnp.bfloat16)
    r = jax.nn.sigmoid(jnp.dot(ub, wa_ref[...], preferred_element_type=f32) + ba_ref[...])
    i = jax.nn.sigmoid(jnp.dot(ub, wx_ref[...], preferred_element_type=f32) + bx_ref[...])
    log_a = LRU_C * r * lsl_ref[...]
    a = jnp.exp(log_a)
    b = jnp.sqrt(_one_minus_exp(2.0 * log_a)) * (i * u)

    row = lax.broadcasted_iota(jnp.int32, (SUBLANES, D_RNN), 0)
    h = h_ref[...]
    for k in range(tile // SUBLANES):
        ak = a[k * SUBLANES:(k + 1) * SUBLANES]
        bk = b[k * SUBLANES:(k + 1) * SUBLANES]
        for s in (1, 2, 4):
            a_prev = jnp.where(row >= s, pltpu.roll(ak, s, axis=0), 1.0)
            b_prev = jnp.where(row >= s, pltpu.roll(bk, s, axis=0), 0.0)
            bk = ak * b_prev + bk
            ak = ak * a_prev
        hk = ak * h + bk
        hs_ref[k * SUBLANES:(k + 1) * SUBLANES, :] = hk
        h = hk[SUBLANES - 1:SUBLANES]
    h_ref[...] = h
    hlast_ref[...] = h


def _block_diag(w):
    n, c, d = w.shape
    return jnp.einsum('ncd,nm->ncmd', w, jnp.eye(n, dtype=w.dtype)).reshape(n * c, n * d)


def _lru(proj, B, T, buf, h0, conv_w, conv_b, wa_bd, ba, wx_bd, bx, lam):
    tile = min(LRU_TILE, T)
    assert T % tile == 0 and tile % SUBLANES == 0 and _SEG_DST["lru_x"] == 0
    row = lambda a: a.reshape(1, D_RNN)
    per_b = lambda b, t: (b, 0, 0)
    const = lambda b, t: (0, 0)
    vec = pl.BlockSpec((1, D_RNN), const)
    mat = pl.BlockSpec((D_RNN, D_RNN), const)
    hs, hlast, cbuf = pl.pallas_call(
        functools.partial(_lru_kernel, tile=tile),
        out_shape=(jax.ShapeDtypeStruct((B, T, D_RNN), jnp.float32),
                   jax.ShapeDtypeStruct((B, 1, D_RNN), jnp.float32),
                   jax.ShapeDtypeStruct((B, CONV_W - 1, D_RNN), jnp.float32)),
        grid=(B, T // tile),
        in_specs=[
            pl.BlockSpec((None, tile, D_RNN), lambda b, t: (b, t, 0)),
            pl.BlockSpec((None, CONV_W - 1, D_RNN), per_b),
            pl.BlockSpec((None, 1, D_RNN), per_b),
            pl.BlockSpec((CONV_W, D_RNN), const),
            vec, mat, vec, mat, vec, vec,
        ],
        out_specs=(pl.BlockSpec((None, tile, D_RNN), lambda b, t: (b, t, 0)),
                   pl.BlockSpec((None, 1, D_RNN), per_b),
                   pl.BlockSpec((None, CONV_W - 1, D_RNN), per_b)),
        scratch_shapes=[pltpu.VMEM((tile + SUBLANES, D_RNN), jnp.float32),
                        pltpu.VMEM((1, D_RNN), jnp.float32)],
        compiler_params=pltpu.CompilerParams(dimension_semantics=("parallel", "arbitrary")),
        name="rg_lru",
    )(proj.reshape(B, T, IN_COLS_PAD), buf, h0.reshape(B, 1, D_RNN), conv_w, row(conv_b),
      wa_bd, row(ba), wx_bd, row(bx), row(jax.nn.log_sigmoid(lam)))
    return hs, hlast.reshape(B, D_RNN), cbuf


DN_CHUNKS_PER_STEP = 4


def _split2(a):
    hi = a.astype(jnp.bfloat16)
    return hi, (a - hi.astype(jnp.float32)).astype(jnp.bfloat16)


def _dot_hi(a, b):
    a1, a2 = _split2(a)
    b1, b2 = _split2(b)
    d = lambda x, y: jnp.dot(x, y, preferred_element_type=jnp.float32)
    return d(a1, b1) + (d(a1, b2) + d(a2, b1))


def _unit_lower_solve(n_mats, rhss):
    C, n = rhss[0].shape
    f32 = jnp.float32
    ci = lax.broadcasted_iota(jnp.int32, (C, C), 0)
    ei = lax.broadcasted_iota(jnp.int32, (C, C), 1)
    same = (ci // SUBLANES) == (ei // SUBLANES)
    eye = jnp.where(ci == ei, 1.0, 0.0)
    diags = [jnp.where(same, m, 0.0) for m in n_mats]
    offs = [jnp.where(same, 0.0, m) for m in n_mats]
    d2 = [_dot_hi(d, d) for d in diags]
    d4 = [_dot_hi(d, d) for d in d2]
    part = [_dot_hi(eye - d, eye + s) for d, s in zip(diags, d2)]
    dinv = [_dot_hi(p, eye + s) for p, s in zip(part, d4)]
    solved = [[] for _ in rhss]
    for i in range(C // SUBLANES):
        rows = slice(i * SUBLANES, (i + 1) * SUBLANES)
        ys = [r[rows] for r in rhss]
        if i:
            below = jnp.zeros((C - i * SUBLANES, n), f32)
            ys = [y - _dot_hi(off[rows], jnp.concatenate(done + [below], axis=0))
                  for y, off, done in zip(ys, offs, solved)]
        above = [jnp.zeros((i * SUBLANES, n), f32)] if i else []
        rest = [jnp.zeros((C - (i + 1) * SUBLANES, n), f32)] if (i + 1) * SUBLANES < C else []
        xs = [_dot_hi(inv[rows], jnp.concatenate(above + [y] + rest, axis=0)) for inv, y in zip(dinv, ys)]
        for done, x in zip(solved, xs):
            done.append(x)
    return [jnp.concatenate(done, axis=0) for done in solved]


def _dn_kernel(xq_ref, xk_ref, xv_ref, ab_ref, buf_ref, s0_ref, cw_ref, alog_ref, dtb_ref, onorm_ref,
               o_ref, s_out_ref, cbuf_ref, xe_ref, s_ref, *, n_in):
    f32, bf16 = jnp.float32, jnp.bfloat16
    C = DN_CHUNK
    n = pl.program_id(1)
    last = pl.num_programs(1) - 1
    tail = CONV_W - 1
    width = 3 * DN_WIDTH

    @pl.when(n == 0)
    def _():
        xe_ref[...] = jnp.zeros(xe_ref.shape, f32)
        xe_ref[SUBLANES - tail:SUBLANES, :] = buf_ref[...]
        s_ref[...] = s0_ref[...]

    xe_ref[SUBLANES:SUBLANES + n_in, 0:DN_WIDTH] = xq_ref[...]
    xe_ref[SUBLANES:SUBLANES + n_in, DN_WIDTH:2 * DN_WIDTH] = xk_ref[...]
    xe_ref[SUBLANES:SUBLANES + n_in, 2 * DN_WIDTH:width] = xv_ref[...]
    n_chunks = -(-n_in // C)
    R = n_chunks * C
    cw = cw_ref[...]
    y = jnp.zeros((R, width), f32)
    for j in range(CONV_W):
        y = y + cw[j:j + 1] * xe_ref[SUBLANES - tail + j:SUBLANES - tail + j + R, :]

    @pl.when(n == last)
    def _():
        cbuf_ref[...] = xe_ref[SUBLANES + n_in - tail:SUBLANES + n_in, :]

    xe_ref[0:SUBLANES, :] = xe_ref[n_in:n_in + SUBLANES, :]

    valid = lax.broadcasted_iota(jnp.int32, (R, 1), 0) < n_in
    y = jnp.where(valid, y * jax.nn.sigmoid(y), 0.0)
    ab = ab_ref[...]
    if n_in < R:
        ab = jnp.concatenate([ab, jnp.zeros((R - n_in, LANE), f32)], axis=0)
    g_all = jnp.where(valid, -jnp.exp(alog_ref[...]) * jax.nn.softplus(ab + dtb_ref[...]), 0.0)
    beta_all = jnp.where(valid, jax.nn.sigmoid(ab), 0.0)

    ci = lax.broadcasted_iota(jnp.int32, (C, C), 0)
    ei = lax.broadcasted_iota(jnp.int32, (C, C), 1)
    tri = jnp.where(ei <= ci, 1.0, 0.0).astype(bf16)
    csum = lambda x: jnp.dot(tri, x, preferred_element_type=f32)
    heads = range(DN_HEADS)
    mm = lambda a, b: jnp.dot(a, b, preferred_element_type=f32)
    col = lambda a, h: a[:, h:h + 1]

    q_dec, qks, k_dec, e_last, n_mats, rhss = [], [], [], [], [], []
    for c in range(n_chunks):
        rows = slice(c * C, (c + 1) * C)
        g_c = g_all[rows]
        g1 = g_c.astype(bf16)
        r1 = g_c - g1.astype(f32)
        g2 = r1.astype(bf16)
        g3 = (r1 - g2.astype(f32)).astype(bf16)
        G = csum(g1) + (csum(g2) + csum(g3))
        G_t = G.T
        g_last = G[C - 1:C, :]
        e_g = jnp.exp(G)
        e_rest = jnp.exp(g_last - G)
        e_last.append(jnp.exp(g_last))
        head_lanes = lambda base, h: y[rows, base + h * DN_HD:base + (h + 1) * DN_HD]
        qs = [head_lanes(0, h) for h in heads]
        ks = [head_lanes(DN_WIDTH, h) for h in heads]
        vs = [head_lanes(2 * DN_WIDTH, h) for h in heads]
        qs = [q * lax.rsqrt(jnp.sum(q * q, axis=-1, keepdims=True) + L2_EPS) * (DN_HD ** -0.5) for q in qs]
        ks = [k * lax.rsqrt(jnp.sum(k * k, axis=-1, keepdims=True) + L2_EPS) for k in ks]
        decays = [jnp.where(ei <= ci, jnp.exp(col(G, h) - G_t[h:h + 1, :]), 0.0) for h in heads]
        betas = [col(beta_all[rows], DN_HEADS + h) for h in heads]
        kbs = [k.astype(bf16) for k in ks]
        kks = [_dot_t(kb, kb) for kb in kbs]
        qk = [_dot_t(q.astype(bf16), kb) for q, kb in zip(qs, kbs)]
        n_mats += [jnp.where(ei < ci, b * kk * d, 0.0) for b, kk, d in zip(betas, kks, decays)]
        qks.append([(a * d).astype(bf16) for a, d in zip(qk, decays)])
        rhss += [jnp.concatenate([betas[h] * vs[h], betas[h] * ks[h] * col(e_g, h)], axis=1) for h in heads]
        q_dec.append([(qs[h] * col(e_g, h)).astype(bf16) for h in heads])
        k_dec.append([(ks[h] * col(e_rest, h)).astype(bf16) for h in heads])
    xs = _unit_lower_solve(n_mats, rhss)

    states = [s_ref[h] for h in heads]
    o_gain = onorm_ref[...]
    for c in range(n_chunks):
        xc = xs[c * DN_HEADS:(c + 1) * DN_HEADS]
        sbs = [S.astype(bf16) for S in states]
        v_news = [x[:, :DN_HD] - mm(x[:, DN_HD:].astype(bf16), sb) for x, sb in zip(xc, sbs)]
        vbs = [v.astype(bf16) for v in v_news]
        o_state = [mm(qd, sb) for qd, sb in zip(q_dec[c], sbs)]
        o_local = [mm(qk, vb) for qk, vb in zip(qks[c], vbs)]
        grow = [lax.dot_general(kd, vb, (((0,), (0,)), ((), ())), preferred_element_type=f32)
                for kd, vb in zip(k_dec[c], vbs)]
        states = [col(e_last[c], h) * states[h] + grow[h] for h in heads]
        n_out = min(C, n_in)
        for h in heads:
            o = o_state[h] + o_local[h]
            o = o * lax.rsqrt(jnp.mean(o * o, axis=-1, keepdims=True) + RMS_EPS) * o_gain
            o_ref[c * C:c * C + n_out, h * DN_HD:(h + 1) * DN_HD] = o[:n_out]
    for h in heads:
        s_ref[h] = states[h]

    @pl.when(n == last)
    def _():
        s_out_ref[...] = s_ref[...]


def _gated_delta_net(proj, B, T, buf, S0, conv_w, A_log, dt_bias, o_norm):
    n_in = min(DN_CHUNKS_PER_STEP * DN_CHUNK, T)
    assert T % n_in == 0 and n_in % SUBLANES == 0 and (n_in <= DN_CHUNK or n_in % DN_CHUNK == 0)
    rows_pad = -(-n_in // DN_CHUNK) * DN_CHUNK
    qkv0 = _SEG_DST["dn_qkv"] // DN_WIDTH
    assert _SEG_DST["dn_qkv"] % DN_WIDTH == 0 and _SEG_DST["dn_b"] == _SEG_DST["dn_a"] + DN_HEADS
    ab_blk = _SEG_DST["dn_a"] // LANE
    proj3 = proj.reshape(B, T, IN_COLS_PAD)
    col = lambda c: (lambda b, n: (b, n, c))
    per_b = lambda b, n: (b, 0, 0)
    const = lambda b, n: (0, 0)
    pad_row = lambda a, off: jnp.zeros((1, LANE), jnp.float32).at[0, off:off + DN_HEADS].set(a)
    o, s_out, cbuf = pl.pallas_call(
        functools.partial(_dn_kernel, n_in=n_in),
        out_shape=(jax.ShapeDtypeStruct((B, T, DN_WIDTH), jnp.float32),
                   jax.ShapeDtypeStruct((B, DN_HEADS, DN_HD, DN_HD), jnp.float32),
                   jax.ShapeDtypeStruct((B, CONV_W - 1, 3 * DN_WIDTH), jnp.float32)),
        grid=(B, T // n_in),
        in_specs=[
            pl.BlockSpec((None, n_in, DN_WIDTH), col(qkv0)),
            pl.BlockSpec((None, n_in, DN_WIDTH), col(qkv0 + 1)),
            pl.BlockSpec((None, n_in, DN_WIDTH), col(qkv0 + 2)),
            pl.BlockSpec((None, n_in, LANE), col(ab_blk)),
            pl.BlockSpec((None, CONV_W - 1, 3 * DN_WIDTH), per_b),
            pl.BlockSpec((None, DN_HEADS, DN_HD, DN_HD), lambda b, n: (b, 0, 0, 0)),
            pl.BlockSpec((CONV_W, 3 * DN_WIDTH), const),
            pl.BlockSpec((1, LANE), const),
            pl.BlockSpec((1, LANE), const),
            pl.BlockSpec((1, DN_HD), const),
        ],
        out_specs=(pl.BlockSpec((None, n_in, DN_WIDTH), lambda b, n: (b, n, 0)),
                   pl.BlockSpec((None, DN_HEADS, DN_HD, DN_HD), lambda b, n: (b, 0, 0, 0)),
                   pl.BlockSpec((None, CONV_W - 1, 3 * DN_WIDTH), per_b)),
        scratch_shapes=[pltpu.VMEM((rows_pad + 2 * SUBLANES, 3 * DN_WIDTH), jnp.float32),
                        pltpu.VMEM((DN_HEADS, DN_HD, DN_HD), jnp.float32)],
        compiler_params=pltpu.CompilerParams(dimension_semantics=("parallel", "arbitrary")),
        name="gated_delta",
    )(proj3, proj3, proj3, proj3, buf, S0, conv_w, pad_row(A_log, 0), pad_row(dt_bias, 0),
      o_norm.reshape(1, DN_HD))
    return o, s_out, cbuf


def _rmsnorm(x, g):
    xf = x.astype(jnp.float32)
    y = xf * lax.rsqrt(jnp.mean(xf * xf, axis=-1, keepdims=True) + RMS_EPS)
    return y * g.astype(jnp.float32)


def _rope(x, pos):
    half = ROT_DIM // 2
    inv = ROPE_THETA ** (-jnp.arange(half, dtype=jnp.float32) * 2.0 / ROT_DIM)
    ang = pos.astype(jnp.float32)[:, None] * inv[None, :]
    shape = (1, pos.shape[0]) + (1,) * (x.ndim - 3) + (half,)
    c = jnp.cos(ang).reshape(shape)
    s = jnp.sin(ang).reshape(shape)
    xf = x.astype(jnp.float32)
    x1, x2, rest = xf[..., :half], xf[..., half:ROT_DIM], xf[..., ROT_DIM:]
    return jnp.concatenate([x1 * c - x2 * s, x2 * c + x1 * s, rest], axis=-1)


def _seg_slot(proj, name, B, T):
    s = _SEG_DST[name]
    return proj[:, s:s + LANE].reshape(B, T, LANE)


def _seg(proj, name, B, T):
    s = _SEG_DST[name]
    return proj[:, s:s + _SEG_LEN[name]].reshape(B, T, _SEG_LEN[name])


def _layer(x, paged, lru_h0, lru_buf, dn_S0, dn_buf,
           norm_g, w_in_pad, lru_conv_w, lru_conv_b, lru_wa, lru_ba, lru_wx, lru_bx, lru_lam,
           q_norm, k_norm, dn_conv_w, dn_A_log, dn_dt_bias, dn_o_norm,
           w_lru_out, w_nsa_out, w_dn_out, w_out):
    f32 = jnp.float32
    B, T, _ = x.shape
    P = 0 if paged is None else paged[0].shape[1] * PAGE_SIZE
    pos = P + jnp.arange(T, dtype=jnp.int32)
    x2d = x.reshape(B * T, D_MODEL)
    proj = _in_proj(x2d, norm_g, w_in_pad)
    nsa_q, nsa_kv = _seg(proj, "nsa_q", B, T), _seg(proj, "nsa_kv", B, T)

    lru_seq, lru_h, lru_buf_new = _lru(proj, B, T, lru_buf, lru_h0, lru_conv_w, lru_conv_b,
                                       lru_wa, lru_ba, lru_wx, lru_bx, lru_lam)

    q = _rope(_rmsnorm(nsa_q.reshape(B, T, NSA_HEADS, NSA_HD), q_norm), pos) * (NSA_HD ** -0.5)
    kv = nsa_kv.reshape(B, T, 2 * N_BRANCH, NSA_KV, NSA_HD)
    k = _rope(_rmsnorm(kv[:, :, 0::2], k_norm[:, None, :]), pos).astype(x.dtype)
    v = kv[:, :, 1::2]
    rows = jnp.stack([k[:, :, 0], v[:, :, 0], k[:, :, 1], v[:, :, 1]], axis=2)
    win_rows = jnp.stack([k[:, :, 2], v[:, :, 2]], axis=2)
    flat = lambda a: a.reshape(B, T, KV_LANES)
    bf = lambda a: flat(a).astype(jnp.bfloat16)
    gate = _seg_slot(proj, "nsa_g", B, T)
    if paged is None:
        o_nsa = _nsa_prompt(q.reshape(B, T, NSA_WIDTH), gate,
                            flat(k[:, :, 0]), flat(v[:, :, 0]), bf(k[:, :, 1]), bf(v[:, :, 1]),
                            bf(k[:, :, 2]), bf(v[:, :, 2]))
        win_state = win_rows[:, -min(WINDOW, T):]
    else:
        page_table, cache, layer, win_prev = paged
        new = lambda a: jnp.pad(bf(a), ((0, 0), (0, NEW_PAD - T), (0, 0)))
        o_nsa = _nsa_sample(page_table, cache, layer, q.reshape(B, T, NSA_WIDTH), gate,
                            new(k[:, :, 1]), new(v[:, :, 1]),
                            win_prev.reshape(B, win_prev.shape[1], 2 * KV_LANES),
                            new(k[:, :, 2]), new(v[:, :, 2]))
        win_state = jnp.concatenate([win_prev.astype(win_rows.dtype), win_rows], axis=1)[:, -WINDOW:]

    o_dn, dn_S, dn_buf_new = _gated_delta_net(proj, B, T, dn_buf, dn_S0, dn_conv_w, dn_A_log, dn_dt_bias,
                                              dn_o_norm)

    y2d = _out_stage(x2d, proj, lru_seq.reshape(B * T, D_RNN), o_nsa.reshape(B * T, NSA_WIDTH),
                     o_dn.reshape(B * T, DN_WIDTH), w_lru_out, w_nsa_out, w_dn_out, w_out)
    return y2d.reshape(B, T, D_MODEL), (rows, win_state, lru_h, lru_buf_new, dn_S, dn_buf_new)


def kernel(x_prompt, x_sample, cache_nsa_kv, page_table, state_nsa_win, state_lru_h, state_lru_conv,
           state_dn_S, state_dn_conv, norm_gain, w_in, lru_conv_w, lru_conv_b, lru_wa, lru_ba,
           lru_wx, lru_bx, lru_lambda, nsa_q_norm, nsa_k_norm, dn_conv_w, dn_A_log, dn_dt_bias,
           dn_o_norm, w_lru_out, w_nsa_out, w_dn_out, w_out):
    Bp = x_prompt.shape[0]
    dt = x_prompt.dtype
    bf16 = jnp.bfloat16
    cache = cache_nsa_kv.reshape(cache_nsa_kv.shape[:2] + (PAGE_SIZE, 4 * KV_LANES))
    p_h0 = jnp.zeros((Bp, D_RNN), jnp.float32)
    p_lbuf0 = jnp.zeros((Bp, CONV_W - 1, D_RNN), dt)
    p_S0 = jnp.zeros((Bp, DN_HEADS, DN_HD, DN_HD), jnp.float32)
    p_dbuf0 = jnp.zeros((Bp, CONV_W - 1, 3 * DN_WIDTH), dt)
    wa_bd = jax.vmap(_block_diag)(lru_wa).astype(bf16)
    wx_bd = jax.vmap(_block_diag)(lru_wx).astype(bf16)
    weights = (norm_gain, _pad_in_weight(w_in), lru_conv_w, lru_conv_b, wa_bd, lru_ba, wx_bd, lru_bx,
               lru_lambda, nsa_q_norm, nsa_k_norm, dn_conv_w, dn_A_log, dn_dt_bias, dn_o_norm,
               w_lru_out.astype(bf16), _heads_rg(w_nsa_out.swapaxes(1, 2)).swapaxes(1, 2).astype(bf16),
               w_dn_out.astype(bf16), w_out.astype(bf16))
    xp, xs = x_prompt, x_sample
    st_p, st_s = [], []
    for l in range(DEPTH):
        lw = [w[l] for w in weights]
        xp, sp = _layer(xp, None, p_h0, p_lbuf0, p_S0, p_dbuf0, *lw)
        xs, ss = _layer(xs, (page_table, cache, l, state_nsa_win[l]), state_lru_h[l], state_lru_conv[l],
                        state_dn_S[l], state_dn_conv[l], *lw)
        st_p.append(sp)
        st_s.append(ss)
    kv_p, win_p, lh_p, lc_p, S_p, dc_p = [jnp.stack(a) for a in zip(*st_p)]
    kv_s, win_s, lh_s, lc_s, S_s, dc_s = [jnp.stack(a) for a in zip(*st_s)]
    return (xp, xs, kv_p, kv_s, win_p, win_s, lh_p, lh_s, lc_p, lc_s, S_p, S_s, dc_p, dc_s)
```

```python
import functools
import math

import numpy as np
import jax
import jax.numpy as jnp
from jax import lax
from jax.experimental import pallas as pl
from jax.experimental.pallas import tpu as pltpu

D_MODEL = 1024
DEPTH = 4
PAGE_SIZE = 128
CONV_W = 4
RMS_EPS = 1e-6
L2_EPS = 1e-6
N_BRANCH = 3
D_RNN = D_MODEL // 2
LRU_BLOCKS = 8
LRU_BS = D_RNN // LRU_BLOCKS
LRU_C = 8.0
NSA_HEADS = 8
NSA_HD = 64
NSA_KV = 2
NSA_REP = NSA_HEADS // NSA_KV
NSA_WIDTH = NSA_HEADS * NSA_HD
CMP_BLOCK = 32
SLC_BLOCK = 64
N_SEL = 16
WINDOW = 512
Q_BLOCK = 128
ROT_DIM = NSA_HD // 4
ROPE_THETA = 500000.0
DN_HEADS = 4
DN_HD = 128
DN_WIDTH = DN_HEADS * DN_HD
DN_CHUNK = 64

LANE = 128

_SEG_NAMES = ("lru_x", "lru_z", "nsa_q", "nsa_kv", "nsa_z", "nsa_g", "dn_qkv", "dn_z", "dn_a", "dn_b", "merge_g")
_SEG_SIZES = (D_RNN, D_RNN, NSA_WIDTH, 2 * N_BRANCH * NSA_KV * NSA_HD, NSA_WIDTH, N_BRANCH * NSA_HEADS,
              3 * DN_WIDTH, DN_WIDTH, DN_HEADS, DN_HEADS, N_BRANCH * D_MODEL)
_SEG_SRC = dict(zip(_SEG_NAMES, np.concatenate([[0], np.cumsum(_SEG_SIZES)[:-1]]).tolist()))
_SEG_LEN = dict(zip(_SEG_NAMES, _SEG_SIZES))
_DST_SLOTS = (
    (("lru_x",), D_RNN), (("lru_z",), D_RNN), (("nsa_q",), NSA_WIDTH), (("nsa_z",), NSA_WIDTH),
    (("dn_z",), DN_WIDTH), (("nsa_g",), LANE), (("dn_a", "dn_b"), LANE),
    (("nsa_kv",), 2 * N_BRANCH * NSA_KV * NSA_HD), (("dn_qkv",), 3 * DN_WIDTH), (("merge_g",), N_BRANCH * D_MODEL),
)


def _dst_layout():
    off, dst = 0, {}
    for names, width in _DST_SLOTS:
        o = off
        for n in names:
            dst[n] = o
            o += _SEG_LEN[n]
        off += width
    return dst, off


_SEG_DST, IN_COLS_PAD = _dst_layout()


def _heads_rg(a):
    lead = a.shape[:-1]
    return a.reshape(lead + (NSA_KV, NSA_REP, NSA_HD)).swapaxes(-3, -2).reshape(lead + (NSA_WIDTH,))


def _pad_in_weight(w_in):
    pieces = []
    for names, width in _DST_SLOTS:
        used = 0
        for n in names:
            piece = w_in[:, :, _SEG_SRC[n]:_SEG_SRC[n] + _SEG_LEN[n]]
            pieces.append(_heads_rg(piece) if n in ("nsa_q", "nsa_z") else piece)
            used += _SEG_LEN[n]
        if used < width:
            pieces.append(jnp.zeros(w_in.shape[:2] + (width - used,), w_in.dtype))
    return jnp.concatenate(pieces, axis=-1).astype(jnp.bfloat16)


IN_PROJ_ROWS = 2048
IN_PROJ_COLS = 512


def _in_proj_kernel(x_ref, g_ref, w_ref, o_ref, h_ref):
    @pl.when(pl.program_id(1) == 0)
    def _():
        x = x_ref[...]
        ms = jnp.mean(x * x, axis=-1, keepdims=True)
        h_ref[...] = (x * lax.rsqrt(ms + RMS_EPS) * g_ref[...]).astype(jnp.bfloat16)

    o_ref[...] = jnp.dot(h_ref[...], w_ref[...], preferred_element_type=jnp.float32)


def _in_proj(x2d, gain, w_pad):
    m = x2d.shape[0]
    tm = min(IN_PROJ_ROWS, m)
    tn = IN_PROJ_COLS
    vmem = 2 * (tm * D_MODEL * 4 + D_MODEL * tn * 2 + tm * tn * 4) + tm * D_MODEL * 2 + (4 << 20)
    return pl.pallas_call(
        _in_proj_kernel,
        out_shape=jax.ShapeDtypeStruct((m, IN_COLS_PAD), jnp.float32),
        grid=(m // tm, IN_COLS_PAD // tn),
        in_specs=[
            pl.BlockSpec((tm, D_MODEL), lambda i, j: (i, 0)),
            pl.BlockSpec((1, D_MODEL), lambda i, j: (0, 0)),
            pl.BlockSpec((D_MODEL, tn), lambda i, j: (0, j)),
        ],
        out_specs=pl.BlockSpec((tm, tn), lambda i, j: (i, j)),
        scratch_shapes=[pltpu.VMEM((tm, D_MODEL), jnp.bfloat16)],
        compiler_params=pltpu.CompilerParams(dimension_semantics=("parallel", "arbitrary"),
                                             vmem_limit_bytes=vmem),
        name="in_proj",
    )(x2d, gain.reshape(1, D_MODEL), w_pad)


def _silu(z):
    return z * jax.nn.sigmoid(z)


def _out_stage_kernel(x_ref, a_lru, z_lru, a_nsa, z_nsa, a_dn, z_dn, mg0, mg1, mg2,
                      w_lru, w_nsa, w_dn, w_out, o_ref):
    def branch(a, z, w):
        y = (a[...] * _silu(z[...])).astype(jnp.bfloat16)
        return jnp.dot(y, w[...], preferred_element_type=jnp.float32)

    merged = jax.nn.sigmoid(mg0[...]) * branch(a_lru, z_lru, w_lru)
    merged = merged + jax.nn.sigmoid(mg1[...]) * branch(a_nsa, z_nsa, w_nsa)
    merged = merged + jax.nn.sigmoid(mg2[...]) * branch(a_dn, z_dn, w_dn)
    y = jnp.dot(merged.astype(jnp.bfloat16), w_out[...], preferred_element_type=jnp.float32)
    o_ref[...] = x_ref[...] + y


def _out_stage(x2d, proj, a_lru, a_nsa, a_dn, w_lru, w_nsa, w_dn, w_out):
    m = x2d.shape[0]
    tm = min(256, m)
    half = D_RNN
    row = lambda i: (i, 0)
    col = lambda c: (lambda i: (i, c))
    full = lambda i: (0, 0)
    mg_blk = _SEG_DST["merge_g"] // D_MODEL
    return pl.pallas_call(
        _out_stage_kernel,
        out_shape=jax.ShapeDtypeStruct((m, D_MODEL), jnp.float32),
        grid=(m // tm,),
        in_specs=[
            pl.BlockSpec((tm, D_MODEL), row),
            pl.BlockSpec((tm, half), row),
            pl.BlockSpec((tm, half), col(_SEG_DST["lru_z"] // half)),
            pl.BlockSpec((tm, half), row),
            pl.BlockSpec((tm, half), col(_SEG_DST["nsa_z"] // half)),
            pl.BlockSpec((tm, half), row),
            pl.BlockSpec((tm, half), col(_SEG_DST["dn_z"] // half)),
            pl.BlockSpec((tm, D_MODEL), col(mg_blk)),
            pl.BlockSpec((tm, D_MODEL), col(mg_blk + 1)),
            pl.BlockSpec((tm, D_MODEL), col(mg_blk + 2)),
            pl.BlockSpec((half, D_MODEL), full),
            pl.BlockSpec((half, D_MODEL), full),
            pl.BlockSpec((half, D_MODEL), full),
            pl.BlockSpec((D_MODEL, D_MODEL), full),
        ],
        out_specs=pl.BlockSpec((tm, D_MODEL), row),
        compiler_params=pltpu.CompilerParams(dimension_semantics=("parallel",)),
        name="out_stage",
    )(x2d, a_lru, proj, a_nsa, proj, a_dn, proj, proj, proj, proj, w_lru, w_nsa, w_dn, w_out)


PREP_TILE = 512
HALF_ROT = ROT_DIM // 2


def _rope_tables(pos):
    inv = ROPE_THETA ** (-jnp.arange(HALF_ROT, dtype=jnp.float32) * 2.0 / ROT_DIM)
    ang = pos.astype(jnp.float32)[:, None] * inv[None, :]
    c, s = jnp.cos(ang), jnp.sin(ang)
    n = pos.shape[0]
    rest = NSA_HD - ROT_DIM
    cos_h = jnp.concatenate([c, c, jnp.ones((n, rest), jnp.float32)], axis=1)
    sin_h = jnp.concatenate([-s, s, jnp.zeros((n, rest), jnp.float32)], axis=1)
    return jnp.tile(cos_h, (1, NSA_KV)), jnp.tile(sin_h, (1, NSA_KV))


def _nsa_prep_kernel(q_ref, kv0_ref, kv1_ref, kv2_ref, cos_ref, sin_ref, qn_ref, kn_ref, ones_ref,
                     q_out, rows_out, win_out, kvb_out):
    f32, bf16 = jnp.float32, jnp.bfloat16
    cos, sin = cos_ref[...], sin_ref[...]
    ones = ones_ref[...]
    n = cos.shape[0]
    first = (lax.broadcasted_iota(jnp.int32, (n, KV_LANES), 1) % NSA_HD) < HALF_ROT

    def norm_rope(x, gain):
        sq = x * x
        hi = sq.astype(bf16)
        lo = (sq - hi.astype(f32)).astype(bf16)
        ssq = (jnp.dot(hi, ones, preferred_element_type=f32) + jnp.dot(lo, ones, preferred_element_type=f32))
        y = x * lax.rsqrt(ssq * (1.0 / NSA_HD) + RMS_EPS) * gain
        partner = jnp.where(first, pltpu.roll(y, KV_LANES - HALF_ROT, axis=1), pltpu.roll(y, HALF_ROT, axis=1))
        return y * cos + partner * sin

    qn = qn_ref[...]
    for r in range(NSA_REP):
        lanes = slice(r * KV_LANES, (r + 1) * KV_LANES)
        q_out[:, lanes] = norm_rope(q_ref[:, lanes], qn) * (NSA_HD ** -0.5)
    kn = kn_ref[...]
    k_cmp = norm_rope(kv0_ref[:, :KV_LANES], kn[0:1])
    k_slc = norm_rope(kv1_ref[:, :KV_LANES], kn[1:2])
    k_win = norm_rope(kv2_ref[:, :KV_LANES], kn[2:3])
    v_slc = kv1_ref[:, KV_LANES:]
    v_win = kv2_ref[:, KV_LANES:]
    rows_out[:, 0:KV_LANES] = k_cmp
    rows_out[:, KV_LANES:2 * KV_LANES] = kv0_ref[:, KV_LANES:]
    rows_out[:, 2 * KV_LANES:3 * KV_LANES] = k_slc
    rows_out[:, 3 * KV_LANES:4 * KV_LANES] = v_slc
    win_out[:, 0:KV_LANES] = k_win
    win_out[:, KV_LANES:2 * KV_LANES] = v_win
    kvb_out[:, 0:KV_LANES] = k_slc.astype(bf16)
    kvb_out[:, KV_LANES:2 * KV_LANES] = v_slc.astype(bf16)
    kvb_out[:, 2 * KV_LANES:3 * KV_LANES] = k_win.astype(bf16)
    kvb_out[:, 3 * KV_LANES:4 * KV_LANES] = v_win.astype(bf16)


def _nsa_prep(proj3, pos, q_norm, k_norm):
    B, T, _ = proj3.shape
    tile = min(PREP_TILE, T)
    assert T % tile == 0
    pair = 2 * KV_LANES
    q_blk = _SEG_DST["nsa_q"] // NSA_WIDTH
    kv_blk = _SEG_DST["nsa_kv"] // pair
    assert _SEG_DST["nsa_q"] % NSA_WIDTH == 0 and _SEG_DST["nsa_kv"] % pair == 0
    cos_t, sin_t = _rope_tables(pos)
    lane_head = jnp.arange(KV_LANES) // NSA_HD
    ones = (lane_head[:, None] == lane_head[None, :]).astype(jnp.bfloat16)
    col = lambda c: (lambda b, t: (b, t, c))
    tab = pl.BlockSpec((tile, KV_LANES), lambda b, t: (t, 0))
    const = lambda b, t: (0, 0)
    out = lambda w: pl.BlockSpec((None, tile, w), lambda b, t: (b, t, 0))
    return pl.pallas_call(
        _nsa_prep_kernel,
        out_shape=(jax.ShapeDtypeStruct((B, T, NSA_WIDTH), jnp.float32),
                   jax.ShapeDtypeStruct((B, T, 4 * KV_LANES), jnp.float32),
                   jax.ShapeDtypeStruct((B, T, 2 * KV_LANES), jnp.float32),
                   jax.ShapeDtypeStruct((B, T, 4 * KV_LANES), jnp.bfloat16)),
        grid=(B, T // tile),
        in_specs=[
            pl.BlockSpec((None, tile, NSA_WIDTH), col(q_blk)),
            pl.BlockSpec((None, tile, pair), col(kv_blk)),
            pl.BlockSpec((None, tile, pair), col(kv_blk + 1)),
            pl.BlockSpec((None, tile, pair), col(kv_blk + 2)),
            tab, tab,
            pl.BlockSpec((1, KV_LANES), const),
            pl.BlockSpec((N_BRANCH, KV_LANES), const),
            pl.BlockSpec((KV_LANES, KV_LANES), const),
        ],
        out_specs=(out(NSA_WIDTH), out(4 * KV_LANES), out(2 * KV_LANES), out(4 * KV_LANES)),
        compiler_params=pltpu.CompilerParams(dimension_semantics=("parallel", "parallel")),
        name="nsa_prep",
    )(proj3, proj3, proj3, proj3, cos_t, sin_t, jnp.tile(q_norm, NSA_KV).reshape(1, KV_LANES),
      jnp.tile(k_norm, (1, NSA_KV)), ones)


NEG_BIG = -1e30
SLC_CHUNK = 512
N_ROWS = NSA_KV * NSA_REP * Q_BLOCK
KV_LANES = NSA_KV * NSA_HD
WIN_KEYS = WINDOW + Q_BLOCK
MAX_SLC_BLOCKS = LANE // 2


def _dot_t(a, b):
    return lax.dot_general(a, b, (((1,), (1,)), ((), ())), preferred_element_type=jnp.float32)


def _select_blocks(imp_t, cur, n_top):
    nb = imp_t.shape[0]
    jj = lax.broadcasted_iota(jnp.int32, imp_t.shape, 0)
    v = jnp.where((jj == 0) | (jj == cur), jnp.inf, jnp.where(jj > cur, -jnp.inf, imp_t))
    sub = lax.broadcasted_iota(jnp.int32, (8, imp_t.shape[1]), 0)
    ranks = []
    for a in range(nb // 8):
        va = v[8 * a:8 * a + 8]
        rank = jnp.zeros(va.shape, jnp.float32)
        for j in range(nb):
            row = v[j:j + 1]
            ge = jnp.where(row >= va, 1.0, 0.0)
            gt = jnp.where(row > va, 1.0, 0.0)
            if j < 8 * a:
                rank = rank + ge
            elif j >= 8 * a + 8:
                rank = rank + gt
            else:
                rank = rank + jnp.where(sub > (j - 8 * a), ge, gt)
        ranks.append(rank)
    rank = jnp.concatenate(ranks, axis=0)
    return jnp.where((rank < n_top) & (jj <= cur), 1.0, 0.0)


def _nsa_prompt_kernel(q_ref, gate_ref, kc_ref, vc_ref, ks_ref, vs_ref, kw_ref, vw_ref, o_ref,
                       kcb_ref, vcb_ref, m_ref, l_ref, acc_ref, *, seq_len, n_top):
    f32, bf16 = jnp.float32, jnp.bfloat16
    i = pl.program_id(1)
    nsb = seq_len // SLC_BLOCK
    half = MAX_SLC_BLOCKS

    @pl.when(i == 0)
    def _():
        if nsb < half:
            kcb_ref[...] = jnp.zeros(kcb_ref.shape, f32)
            vcb_ref[...] = jnp.zeros(vcb_ref.shape, f32)
        for src, dst in ((kc_ref, kcb_ref), (vc_ref, vcb_ref)):
            x = src[...].reshape(nsb, SLC_BLOCK, KV_LANES)
            dst[0:nsb, :] = jnp.sum(x[:, :CMP_BLOCK, :], axis=1) * (1.0 / CMP_BLOCK)
            dst[half:half + nsb, :] = jnp.sum(x[:, CMP_BLOCK:, :], axis=1) * (1.0 / CMP_BLOCK)

    lane = lax.broadcasted_iota(jnp.int32, (Q_BLOCK, LANE), 1)
    tq = lax.broadcasted_iota(jnp.int32, (Q_BLOCK, LANE), 0)
    pos = i * Q_BLOCK + tq
    low = lane < NSA_HD

    q = q_ref[...]
    parts = []
    for g in range(NSA_KV):
        for r in range(NSA_REP):
            qr = q[:, r * KV_LANES:(r + 1) * KV_LANES]
            parts.append(jnp.where(low if g == 0 else ~low, qr, 0.0))
    qpad = jnp.concatenate(parts, axis=0).astype(bf16)

    s = _dot_t(qpad, kcb_ref[...].astype(bf16))
    cblk = jnp.where(lane < half, 2 * lane, 2 * lane - (2 * half - 1))
    okc = (cblk * CMP_BLOCK + (CMP_BLOCK - 1)) <= pos
    s3 = s.reshape(NSA_KV * NSA_REP, Q_BLOCK, LANE) + jnp.where(okc, 0.0, NEG_BIG)[None]
    mx = jnp.max(s3, axis=-1, keepdims=True)
    e = jnp.where(okc[None], jnp.exp(s3 - mx), 0.0)
    den = jnp.sum(e, axis=-1, keepdims=True)
    p3 = e / jnp.where(den > 0.0, den, 1.0)
    o_cmp = jnp.dot(p3.reshape(N_ROWS, LANE).astype(bf16), vcb_ref[...].astype(bf16),
                    preferred_element_type=f32)

    pg = p3.reshape(NSA_KV, NSA_REP, Q_BLOCK, LANE).sum(axis=1)
    cur_t = (i * Q_BLOCK + lax.broadcasted_iota(jnp.int32, (half, Q_BLOCK), 1)) // SLC_BLOCK
    selq = []
    for g in range(NSA_KV):
        imp = pg[g] + pltpu.roll(pg[g], half, axis=1)
        sel_t = _select_blocks(imp.T[:half], cur_t, n_top)
        sel_full = jnp.concatenate([sel_t, jnp.zeros_like(sel_t)], axis=0)
        selq.append(sel_full.T[:, :half].astype(bf16))

    m_ref[...] = jnp.full(m_ref.shape, NEG_BIG, f32)
    l_ref[...] = jnp.zeros(l_ref.shape, f32)
    acc_ref[...] = jnp.zeros(acc_ref.shape, f32)
    heads = (NSA_KV, NSA_REP, Q_BLOCK)

    def slc_step(c, carry):
        start = pl.multiple_of(c * SLC_CHUNK, SLC_CHUNK)
        kch = ks_ref[pl.ds(start, SLC_CHUNK), :]
        vch = vs_ref[pl.ds(start, SLC_CHUNK), :]
        sc = _dot_t(qpad, kch)
        kidx = start + lax.broadcasted_iota(jnp.int32, (Q_BLOCK, SLC_CHUNK), 1)
        causal = kidx <= i * Q_BLOCK + lax.broadcasted_iota(jnp.int32, (Q_BLOCK, SLC_CHUNK), 0)
        kblk = (start + lax.broadcasted_iota(jnp.int32, (half, SLC_CHUNK), 1)) // SLC_BLOCK
        expand = jnp.where(kblk == lax.broadcasted_iota(jnp.int32, (half, SLC_CHUNK), 0), 1.0, 0.0).astype(bf16)
        bias = []
        for g in range(NSA_KV):
            picked = jnp.dot(selq[g], expand, preferred_element_type=f32) > 0.5
            bias.append(jnp.where(picked & causal, 0.0, NEG_BIG))
        bias = jnp.stack(bias, axis=0)[:, None]
        s4 = sc.reshape(heads + (SLC_CHUNK,)) + bias
        m_old = m_ref[...].reshape(heads + (1,))
        m_new = jnp.maximum(m_old, jnp.max(s4, axis=-1, keepdims=True))
        alpha = jnp.exp(m_old - m_new)
        ex = jnp.exp(s4 - m_new)
        l_new = alpha * l_ref[...].reshape(heads + (1,)) + jnp.sum(ex, axis=-1, keepdims=True)
        pv = jnp.dot(ex.reshape(N_ROWS, SLC_CHUNK).astype(bf16), vch, preferred_element_type=f32)
        acc_ref[...] = alpha.reshape(N_ROWS, 1) * acc_ref[...] + pv
        l_ref[...] = l_new.reshape(N_ROWS, 1)
        m_ref[...] = m_new.reshape(N_ROWS, 1)
        return carry

    n_chunks = (i * Q_BLOCK + Q_BLOCK + SLC_CHUNK - 1) // SLC_CHUNK
    lax.fori_loop(0, n_chunks, slc_step, 0)
    o_slc = acc_ref[...] / l_ref[...]

    wstart = pl.multiple_of(jnp.maximum(i - WINDOW // Q_BLOCK, 0) * Q_BLOCK, Q_BLOCK)
    kwin = kw_ref[pl.ds(wstart, WIN_KEYS), :]
    vwin = vw_ref[pl.ds(wstart, WIN_KEYS), :]
    sw = _dot_t(qpad, kwin)
    widx = wstart + lax.broadcasted_iota(jnp.int32, (Q_BLOCK, WIN_KEYS), 1)
    wpos = i * Q_BLOCK + lax.broadcasted_iota(jnp.int32, (Q_BLOCK, WIN_KEYS), 0)
    okw = (widx <= wpos) & (wpos - widx < WINDOW)
    sw3 = sw.reshape(NSA_KV * NSA_REP, Q_BLOCK, WIN_KEYS) + jnp.where(okw, 0.0, NEG_BIG)[None]
    ew = jnp.exp(sw3 - jnp.max(sw3, axis=-1, keepdims=True))
    lw = jnp.sum(ew, axis=-1, keepdims=True).reshape(N_ROWS, 1)
    o_win = jnp.dot(ew.reshape(N_ROWS, WIN_KEYS).astype(bf16), vwin, preferred_element_type=f32) / lw

    gate = jax.nn.sigmoid(gate_ref[...])
    n_hd = NSA_KV * NSA_REP
    for r in range(NSA_REP):
        per_g = []
        for g in range(NSA_KV):
            h = g * NSA_REP + r
            rows = slice(h * Q_BLOCK, (h + 1) * Q_BLOCK)
            per_g.append(gate[:, h:h + 1] * o_cmp[rows]
                         + gate[:, n_hd + h:n_hd + h + 1] * o_slc[rows]
                         + gate[:, 2 * n_hd + h:2 * n_hd + h + 1] * o_win[rows])
        o_ref[:, r * KV_LANES:(r + 1) * KV_LANES] = jnp.where(low, per_g[0], per_g[1])


def _gate_block(rows):
    c = _SEG_DST["nsa_g"] // LANE
    assert _SEG_DST["nsa_g"] % LANE == 0
    return pl.BlockSpec((None, rows, LANE), lambda b, i, *_: (b, i, c))


def _nsa_prompt(q, proj3, rows, kvb):
    B, T, _ = q.shape
    assert T % SLC_CHUNK == 0 and T >= WIN_KEYS and T // SLC_BLOCK <= MAX_SLC_BLOCKS
    n_top = min(N_SEL, T // SLC_BLOCK)
    blk = lambda b, i: (b, i, 0)
    kv_col = lambda c: pl.BlockSpec((None, T, KV_LANES), lambda b, i: (b, 0, c))
    return pl.pallas_call(
        functools.partial(_nsa_prompt_kernel, seq_len=T, n_top=n_top),
        out_shape=jax.ShapeDtypeStruct((B, T, NSA_WIDTH), jnp.float32),
        grid=(B, T // Q_BLOCK),
        in_specs=[
            pl.BlockSpec((None, Q_BLOCK, NSA_WIDTH), blk),
            _gate_block(Q_BLOCK),
            kv_col(0), kv_col(1), kv_col(0), kv_col(1), kv_col(2), kv_col(3),
        ],
        out_specs=pl.BlockSpec((None, Q_BLOCK, NSA_WIDTH), blk),
        scratch_shapes=[
            pltpu.VMEM((LANE, KV_LANES), jnp.float32),
            pltpu.VMEM((LANE, KV_LANES), jnp.float32),
            pltpu.VMEM((N_ROWS, 1), jnp.float32),
            pltpu.VMEM((N_ROWS, 1), jnp.float32),
            pltpu.VMEM((N_ROWS, KV_LANES), jnp.float32),
        ],
        compiler_params=pltpu.CompilerParams(
            dimension_semantics=("parallel", "arbitrary"), vmem_limit_bytes=48 * 1024 * 1024),
        name="nsa_prompt",
    )(q, proj3, rows, rows, kvb, kvb, kvb, kvb)


PAGES_PER_STEP = 8
NEW_PAD = 128


def _qpad_rows(q, nq):
    low = lax.broadcasted_iota(jnp.int32, (nq, LANE), 1) < NSA_HD
    parts = []
    for g in range(NSA_KV):
        for r in range(NSA_REP):
            qr = q[:, r * KV_LANES:(r + 1) * KV_LANES]
            parts.append(jnp.where(low if g == 0 else ~low, qr, 0.0))
    return jnp.concatenate(parts, axis=0).astype(jnp.bfloat16)


def _nsa_sample_kernel(pt_ref, *refs, n_q, past_len, n_top):
    f32, bf16 = jnp.float32, jnp.bfloat16
    pages = refs[:PAGES_PER_STEP]
    (q_ref, gate_ref, ksn_ref, vsn_ref, win_ref, kwn_ref, vwn_ref, o_ref,
     kcb_ref, vcb_ref, sel_ref, ocmp_ref, m_ref, l_ref, acc_ref) = refs[PAGES_PER_STEP:]
    ph = pl.program_id(1)
    p = pl.program_id(2)
    last = pl.num_programs(2) - 1
    n_rows = NSA_KV * NSA_REP * n_q
    n_blk = past_len // SLC_BLOCK
    cmp_per_step = PAGES_PER_STEP * PAGE_SIZE // CMP_BLOCK
    keys_per_step = PAGES_PER_STEP * PAGE_SIZE
    heads = (NSA_KV, NSA_REP, n_q)
    qpad = _qpad_rows(q_ref[...], n_q)

    @pl.when(ph == 0)
    def _():
        means = [pg[...].reshape(PAGE_SIZE // CMP_BLOCK, CMP_BLOCK, 2 * KV_LANES).sum(axis=1) * (1.0 / CMP_BLOCK)
                 for pg in pages]
        means = jnp.concatenate(means, axis=0)
        row0 = pl.multiple_of(p * cmp_per_step, cmp_per_step)
        kcb_ref[pl.ds(row0, cmp_per_step), :] = means[:, :KV_LANES]
        vcb_ref[pl.ds(row0, cmp_per_step), :] = means[:, KV_LANES:]

    @pl.when((ph == 0) & (p == last))
    def _():
        halves = []
        for par in range(2):
            kc = kcb_ref[pl.ds(par, n_blk, stride=2), :].astype(bf16)
            halves.append(_dot_t(qpad, kc))
        mx = jnp.maximum(jnp.max(halves[0], axis=-1, keepdims=True), jnp.max(halves[1], axis=-1, keepdims=True))
        e0, e1 = jnp.exp(halves[0] - mx), jnp.exp(halves[1] - mx)
        inv = 1.0 / (jnp.sum(e0, axis=-1, keepdims=True) + jnp.sum(e1, axis=-1, keepdims=True))
        p0, p1 = e0 * inv, e1 * inv
        oc = jnp.zeros((n_rows, KV_LANES), f32)
        for par, pp in ((0, p0), (1, p1)):
            vc = vcb_ref[pl.ds(par, n_blk, stride=2), :].astype(bf16)
            oc = oc + jnp.dot(pp.astype(bf16), vc, preferred_element_type=f32)
        ocmp_ref[...] = oc
        imp = (p0 + p1).reshape(heads + (n_blk,)).sum(axis=1).reshape(NSA_KV * n_q, n_blk)
        lane = lax.broadcasted_iota(jnp.int32, imp.shape, 1)
        v = jnp.where(lane == 0, jnp.inf, imp)
        rank = jnp.zeros(imp.shape, f32)
        for j in range(n_blk):
            col = v[:, j:j + 1]
            tie = jnp.where(lane > j, 1.0, 0.0)
            rank = rank + jnp.where(col > v, 1.0, jnp.where(col == v, tie, 0.0))
        sel_ref[...] = jnp.where(rank < n_top - 1, 1.0, 0.0)
        m_ref[...] = jnp.full(m_ref.shape, NEG_BIG, f32)
        l_ref[...] = jnp.zeros(l_ref.shape, f32)
        acc_ref[...] = jnp.zeros(acc_ref.shape, f32)

    def online(s4, vals):
        m_old = m_ref[...].reshape(heads + (1,))
        m_new = jnp.maximum(m_old, jnp.max(s4, axis=-1, keepdims=True))
        alpha = jnp.exp(m_old - m_new)
        ex = jnp.exp(s4 - m_new)
        l_new = alpha * l_ref[...].reshape(heads + (1,)) + jnp.sum(ex, axis=-1, keepdims=True)
        pv = jnp.dot(ex.reshape(n_rows, s4.shape[-1]).astype(bf16), vals, preferred_element_type=f32)
        acc_ref[...] = alpha.reshape(n_rows, 1) * acc_ref[...] + pv
        l_ref[...] = l_new.reshape(n_rows, 1)
        m_ref[...] = m_new.reshape(n_rows, 1)

    @pl.when(ph == 1)
    def _():
        xs = [pg[...] for pg in pages]
        sc = jnp.concatenate([_dot_t(qpad, x[:, :KV_LANES].astype(bf16)) for x in xs], axis=-1)
        vals = jnp.concatenate([x[:, KV_LANES:].astype(bf16) for x in xs], axis=0)
        kblk = (p * keys_per_step + lax.broadcasted_iota(jnp.int32, (n_blk, keys_per_step), 1)) // SLC_BLOCK
        expand = jnp.where(kblk == lax.broadcasted_iota(jnp.int32, (n_blk, keys_per_step), 0), 1.0, 0.0).astype(bf16)
        picked = jnp.dot(sel_ref[...].astype(bf16), expand, preferred_element_type=f32) > 0.5
        bias = jnp.where(picked, 0.0, NEG_BIG).reshape(NSA_KV, 1, n_q, keys_per_step)
        online(sc.reshape(heads + (keys_per_step,)) + bias, vals)

    @pl.when((ph == 1) & (p == last))
    def _():
        tq = lax.broadcasted_iota(jnp.int32, (n_q, NEW_PAD), 0)
        tk = lax.broadcasted_iota(jnp.int32, (n_q, NEW_PAD), 1)
        new_bias = jnp.where(tk <= tq, 0.0, NEG_BIG)[None, None]
        s_new = _dot_t(qpad, ksn_ref[...]).reshape(heads + (NEW_PAD,)) + new_bias
        online(s_new, vsn_ref[...])
        o_slc = acc_ref[...] / l_ref[...]

        win = win_ref[...]
        n_prev = win.shape[0]
        s_prev = _dot_t(qpad, win[:, :KV_LANES].astype(bf16))
        pk = lax.broadcasted_iota(jnp.int32, (n_q, n_prev), 1)
        pq = lax.broadcasted_iota(jnp.int32, (n_q, n_prev), 0)
        s_prev = s_prev.reshape(heads + (n_prev,)) + jnp.where(pk > pq + (n_prev - WINDOW), 0.0, NEG_BIG)[None, None]
        s_wnew = _dot_t(qpad, kwn_ref[...]).reshape(heads + (NEW_PAD,)) + new_bias
        mw = jnp.maximum(jnp.max(s_prev, axis=-1, keepdims=True), jnp.max(s_wnew, axis=-1, keepdims=True))
        e_prev, e_new = jnp.exp(s_prev - mw), jnp.exp(s_wnew - mw)
        lw = jnp.sum(e_prev, axis=-1, keepdims=True) + jnp.sum(e_new, axis=-1, keepdims=True)
        o_win = (jnp.dot(e_prev.reshape(n_rows, n_prev).astype(bf16), win[:, KV_LANES:].astype(bf16),
                         preferred_element_type=f32)
                 + jnp.dot(e_new.reshape(n_rows, NEW_PAD).astype(bf16), vwn_ref[...], preferred_element_type=f32))
        o_win = o_win / lw.reshape(n_rows, 1)

        gate = jax.nn.sigmoid(gate_ref[...])
        low = lax.broadcasted_iota(jnp.int32, (n_q, LANE), 1) < NSA_HD
        o_cmp = ocmp_ref[...]
        n_hd = NSA_KV * NSA_REP
        for r in range(NSA_REP):
            per_g = []
            for g in range(NSA_KV):
                h = g * NSA_REP + r
                rows = slice(h * n_q, (h + 1) * n_q)
                per_g.append(gate[:, h:h + 1] * o_cmp[rows]
                             + gate[:, n_hd + h:n_hd + h + 1] * o_slc[rows]
                             + gate[:, 2 * n_hd + h:2 * n_hd + h + 1] * o_win[rows])
            o_ref[:, r * KV_LANES:(r + 1) * KV_LANES] = jnp.where(low, per_g[0], per_g[1])


def _nsa_sample(page_table, cache, layer, q, proj3, kvb_new, win_prev):
    B, T, _ = q.shape
    n_pages = page_table.shape[1]
    past_len = n_pages * PAGE_SIZE
    n_prev = win_prev.shape[1]
    n_blk = past_len // SLC_BLOCK
    assert T % 8 == 0 and T < CMP_BLOCK and n_pages % PAGES_PER_STEP == 0 and n_blk % LANE == 0
    assert n_blk <= LANE and n_prev == WINDOW
    n_top = min(N_SEL, n_blk + 1)
    n_rows = NSA_KV * NSA_REP * T
    per_b = lambda b, ph, p, pt: (b, 0, 0)

    def page_spec(k):
        return pl.BlockSpec((None, None, PAGE_SIZE, 2 * KV_LANES),
                            lambda b, ph, p, pt: (layer, pt[b, p * PAGES_PER_STEP + k], 0, ph))

    new_col = lambda c: pl.BlockSpec((None, NEW_PAD, KV_LANES), lambda b, ph, p, pt: (b, 0, c))
    gate_col = _SEG_DST["nsa_g"] // LANE
    grid_spec = pltpu.PrefetchScalarGridSpec(
        num_scalar_prefetch=1,
        grid=(B, 2, n_pages // PAGES_PER_STEP),
        in_specs=[page_spec(k) for k in range(PAGES_PER_STEP)] + [
            pl.BlockSpec((None, T, NSA_WIDTH), per_b),
            pl.BlockSpec((None, T, LANE), lambda b, ph, p, pt: (b, 0, gate_col)),
            new_col(0), new_col(1),
            pl.BlockSpec((None, n_prev, 2 * KV_LANES), per_b),
            new_col(2), new_col(3),
        ],
        out_specs=pl.BlockSpec((None, T, NSA_WIDTH), per_b),
        scratch_shapes=[
            pltpu.VMEM((past_len // CMP_BLOCK, KV_LANES), jnp.float32),
            pltpu.VMEM((past_len // CMP_BLOCK, KV_LANES), jnp.float32),
            pltpu.VMEM((NSA_KV * T, n_blk), jnp.float32),
            pltpu.VMEM((n_rows, KV_LANES), jnp.float32),
            pltpu.VMEM((n_rows, 1), jnp.float32),
            pltpu.VMEM((n_rows, 1), jnp.float32),
            pltpu.VMEM((n_rows, KV_LANES), jnp.float32),
        ],
    )
    return pl.pallas_call(
        functools.partial(_nsa_sample_kernel, n_q=T, past_len=past_len, n_top=n_top),
        out_shape=jax.ShapeDtypeStruct((B, T, NSA_WIDTH), jnp.float32),
        grid_spec=grid_spec,
        compiler_params=pltpu.CompilerParams(dimension_semantics=("parallel", "arbitrary", "arbitrary")),
        name="nsa_sample",
    )(page_table, *([cache] * PAGES_PER_STEP), q, proj3, kvb_new, kvb_new, win_prev, kvb_new, kvb_new)


LRU_TILE = 512
SUBLANES = 8


EXPM1_SERIES_RANGE = 0.35
_EXPM1_COEFFS = tuple(1.0 / math.factorial(n) for n in range(9, 1, -1))


def _one_minus_exp(y):
    acc = jnp.full_like(y, _EXPM1_COEFFS[0])
    for c in _EXPM1_COEFFS[1:]:
        acc = acc * y + c
    series = -(y * (1.0 + y * acc))
    return jnp.where(y > -EXPM1_SERIES_RANGE, series, 1.0 - jnp.exp(y))


def _lru_kernel(x_ref, buf_ref, h0_ref, cw_ref, cb_ref, wa_ref, ba_ref, wx_ref, bx_ref, lsl_ref,
                hs_ref, hlast_ref, cbuf_ref, xe_ref, h_ref, *, tile):
    f32 = jnp.float32
    t = pl.program_id(1)
    tail = CONV_W - 1

    @pl.when(t == 0)
    def _():
        xe_ref[0:SUBLANES, :] = jnp.zeros((SUBLANES, D_RNN), f32)
        xe_ref[SUBLANES - tail:SUBLANES, :] = buf_ref[...]
        h_ref[...] = h0_ref[...]

    xe_ref[SUBLANES:SUBLANES + tile, :] = x_ref[...]
    cw = cw_ref[...]
    u = cb_ref[...]
    for j in range(CONV_W):
        u = u + cw[j:j + 1] * xe_ref[SUBLANES - tail + j:SUBLANES - tail + j + tile, :]

    @pl.when(t == pl.num_programs(1) - 1)
    def _():
        cbuf_ref[...] = xe_ref[SUBLANES + tile - tail:SUBLANES + tile, :]

    xe_ref[0:SUBLANES, :] = xe_ref[tile:tile + SUBLANES, :]

    ub = u.astype(jnp.bfloat16)
    r = jax.nn.sigmoid(jnp.dot(ub, wa_ref[...], preferred_element_type=f32) + ba_ref[...])
    i = jax.nn.sigmoid(jnp.dot(ub, wx_ref[...], preferred_element_type=f32) + bx_ref[...])
    log_a = LRU_C * r * lsl_ref[...]
    a = jnp.exp(log_a)
    b = jnp.sqrt(_one_minus_exp(2.0 * log_a)) * (i * u)

    row = lax.broadcasted_iota(jnp.int32, (SUBLANES, D_RNN), 0)
    h = h_ref[...]
    for k in range(tile // SUBLANES):
        ak = a[k * SUBLANES:(k + 1) * SUBLANES]
        bk = b[k * SUBLANES:(k + 1) * SUBLANES]
        for s in (1, 2, 4):
            a_prev = jnp.where(row >= s, pltpu.roll(ak, s, axis=0), 1.0)
            b_prev = jnp.where(row >= s, pltpu.roll(bk, s, axis=0), 0.0)
            bk = ak * b_prev + bk
            ak = ak * a_prev
        hk = ak * h + bk
        hs_ref[k * SUBLANES:(k + 1) * SUBLANES, :] = hk
        h = hk[SUBLANES - 1:SUBLANES]
    h_ref[...] = h
    hlast_ref[...] = h


def _block_diag(w):
    n, c, d = w.shape
    return jnp.einsum('ncd,nm->ncmd', w, jnp.eye(n, dtype=w.dtype)).reshape(n * c, n * d)


def _lru(proj, B, T, buf, h0, conv_w, conv_b, wa_bd, ba, wx_bd, bx, lam):
    tile = min(LRU_TILE, T)
    assert T % tile == 0 and tile % SUBLANES == 0 and _SEG_DST["lru_x"] == 0
    row = lambda a: a.reshape(1, D_RNN)
    per_b = lambda b, t: (b, 0, 0)
    const = lambda b, t: (0, 0)
    vec = pl.BlockSpec((1, D_RNN), const)
    mat = pl.BlockSpec((D_RNN, D_RNN), const)
    hs, hlast, cbuf = pl.pallas_call(
        functools.partial(_lru_kernel, tile=tile),
        out_shape=(jax.ShapeDtypeStruct((B, T, D_RNN), jnp.float32),
                   jax.ShapeDtypeStruct((B, 1, D_RNN), jnp.float32),
                   jax.ShapeDtypeStruct((B, CONV_W - 1, D_RNN), jnp.float32)),
        grid=(B, T // tile),
        in_specs=[
            pl.BlockSpec((None, tile, D_RNN), lambda b, t: (b, t, 0)),
            pl.BlockSpec((None, CONV_W - 1, D_RNN), per_b),
            pl.BlockSpec((None, 1, D_RNN), per_b),
            pl.BlockSpec((CONV_W, D_RNN), const),
            vec, mat, vec, mat, vec, vec,
        ],
        out_specs=(pl.BlockSpec((None, tile, D_RNN), lambda b, t: (b, t, 0)),
                   pl.BlockSpec((None, 1, D_RNN), per_b),
                   pl.BlockSpec((None, CONV_W - 1, D_RNN), per_b)),
        scratch_shapes=[pltpu.VMEM((tile + SUBLANES, D_RNN), jnp.float32),
                        pltpu.VMEM((1, D_RNN), jnp.float32)],
        compiler_params=pltpu.CompilerParams(dimension_semantics=("parallel", "arbitrary")),
        name="rg_lru",
    )(proj.reshape(B, T, IN_COLS_PAD), buf, h0.reshape(B, 1, D_RNN), conv_w, row(conv_b),
      wa_bd, row(ba), wx_bd, row(bx), row(jax.nn.log_sigmoid(lam)))
    return hs, hlast.reshape(B, D_RNN), cbuf


DN_CHUNKS_PER_STEP = 4


def _split2(a):
    hi = a.astype(jnp.bfloat16)
    return hi, (a - hi.astype(jnp.float32)).astype(jnp.bfloat16)


def _dot_hi(a, b):
    a1, a2 = _split2(a)
    b1, b2 = _split2(b)
    d = lambda x, y: jnp.dot(x, y, preferred_element_type=jnp.float32)
    return d(a1, b1) + (d(a1, b2) + d(a2, b1))


def _unit_lower_solve(n_mats, rhss):
    C, n = rhss[0].shape
    f32 = jnp.float32
    ci = lax.broadcasted_iota(jnp.int32, (C, C), 0)
    ei = lax.broadcasted_iota(jnp.int32, (C, C), 1)
    same = (ci // SUBLANES) == (ei // SUBLANES)
    eye = jnp.where(ci == ei, 1.0, 0.0)
    diags = [jnp.where(same, m, 0.0) for m in n_mats]
    offs = [jnp.where(same, 0.0, m) for m in n_mats]
    d2 = [_dot_hi(d, d) for d in diags]
    d4 = [_dot_hi(d, d) for d in d2]
    part = [_dot_hi(eye - d, eye + s) for d, s in zip(diags, d2)]
    dinv = [_dot_hi(p, eye + s) for p, s in zip(part, d4)]
    solved = [[] for _ in rhss]
    for i in range(C // SUBLANES):
        rows = slice(i * SUBLANES, (i + 1) * SUBLANES)
        ys = [r[rows] for r in rhss]
        if i:
            below = jnp.zeros((C - i * SUBLANES, n), f32)
            ys = [y - _dot_hi(off[rows], jnp.concatenate(done + [below], axis=0))
                  for y, off, done in zip(ys, offs, solved)]
        above = [jnp.zeros((i * SUBLANES, n), f32)] if i else []
        rest = [jnp.zeros((C - (i + 1) * SUBLANES, n), f32)] if (i + 1) * SUBLANES < C else []
        xs = [_dot_hi(inv[rows], jnp.concatenate(above + [y] + rest, axis=0)) for inv, y in zip(dinv, ys)]
        for done, x in zip(solved, xs):
            done.append(x)
    return [jnp.concatenate(done, axis=0) for done in solved]


def _dn_kernel(xq_ref, xk_ref, xv_ref, ab_ref, buf_ref, s0_ref, cw_ref, alog_ref, dtb_ref, onorm_ref,
               o_ref, s_out_ref, cbuf_ref, xe_ref, s_ref, *, n_in):
    f32, bf16 = jnp.float32, jnp.bfloat16
    C = DN_CHUNK
    n = pl.program_id(1)
    last = pl.num_programs(1) - 1
    tail = CONV_W - 1
    width = 3 * DN_WIDTH

    @pl.when(n == 0)
    def _():
        xe_ref[...] = jnp.zeros(xe_ref.shape, f32)
        xe_ref[SUBLANES - tail:SUBLANES, :] = buf_ref[...]
        s_ref[...] = s0_ref[...]

    xe_ref[SUBLANES:SUBLANES + n_in, 0:DN_WIDTH] = xq_ref[...]
    xe_ref[SUBLANES:SUBLANES + n_in, DN_WIDTH:2 * DN_WIDTH] = xk_ref[...]
    xe_ref[SUBLANES:SUBLANES + n_in, 2 * DN_WIDTH:width] = xv_ref[...]
    n_chunks = -(-n_in // C)
    R = n_chunks * C
    cw = cw_ref[...]
    y = jnp.zeros((R, width), f32)
    for j in range(CONV_W):
        y = y + cw[j:j + 1] * xe_ref[SUBLANES - tail + j:SUBLANES - tail + j + R, :]

    @pl.when(n == last)
    def _():
        cbuf_ref[...] = xe_ref[SUBLANES + n_in - tail:SUBLANES + n_in, :]

    xe_ref[0:SUBLANES, :] = xe_ref[n_in:n_in + SUBLANES, :]

    valid = lax.broadcasted_iota(jnp.int32, (R, 1), 0) < n_in
    y = jnp.where(valid, y * jax.nn.sigmoid(y), 0.0)
    ab = ab_ref[...]
    if n_in < R:
        ab = jnp.concatenate([ab, jnp.zeros((R - n_in, LANE), f32)], axis=0)
    g_all = jnp.where(valid, -jnp.exp(alog_ref[...]) * jax.nn.softplus(ab + dtb_ref[...]), 0.0)
    beta_all = jnp.where(valid, jax.nn.sigmoid(ab), 0.0)

    ci = lax.broadcasted_iota(jnp.int32, (C, C), 0)
    ei = lax.broadcasted_iota(jnp.int32, (C, C), 1)
    tri = jnp.where(ei <= ci, 1.0, 0.0).astype(bf16)
    csum = lambda x: jnp.dot(tri, x, preferred_element_type=f32)
    heads = range(DN_HEADS)
    mm = lambda a, b: jnp.dot(a, b, preferred_element_type=f32)
    col = lambda a, h: a[:, h:h + 1]

    q_dec, qks, k_dec, e_last, n_mats, rhss = [], [], [], [], [], []
    for c in range(n_chunks):
        rows = slice(c * C, (c + 1) * C)
        g_c = g_all[rows]
        g1 = g_c.astype(bf16)
        r1 = g_c - g1.astype(f32)
        g2 = r1.astype(bf16)
        g3 = (r1 - g2.astype(f32)).astype(bf16)
        G = csum(g1) + (csum(g2) + csum(g3))
        G_t = G.T
        g_last = G[C - 1:C, :]
        e_g = jnp.exp(G)
        e_rest = jnp.exp(g_last - G)
        e_last.append(jnp.exp(g_last))
        head_lanes = lambda base, h: y[rows, base + h * DN_HD:base + (h + 1) * DN_HD]
        qs = [head_lanes(0, h) for h in heads]
        ks = [head_lanes(DN_WIDTH, h) for h in heads]
        vs = [head_lanes(2 * DN_WIDTH, h) for h in heads]
        qs = [q * lax.rsqrt(jnp.sum(q * q, axis=-1, keepdims=True) + L2_EPS) * (DN_HD ** -0.5) for q in qs]
        ks = [k * lax.rsqrt(jnp.sum(k * k, axis=-1, keepdims=True) + L2_EPS) for k in ks]
        decays = [jnp.where(ei <= ci, jnp.exp(col(G, h) - G_t[h:h + 1, :]), 0.0) for h in heads]
        betas = [col(beta_all[rows], DN_HEADS + h) for h in heads]
        kbs = [k.astype(bf16) for k in ks]
        kks = [_dot_t(kb, kb) for kb in kbs]
        qk = [_dot_t(q.astype(bf16), kb) for q, kb in zip(qs, kbs)]
        n_mats += [jnp.where(ei < ci, b * kk * d, 0.0) for b, kk, d in zip(betas, kks, decays)]
        qks.append([(a * d).astype(bf16) for a, d in zip(qk, decays)])
        rhss += [jnp.concatenate([betas[h] * vs[h], betas[h] * ks[h] * col(e_g, h)], axis=1) for h in heads]
        q_dec.append([(qs[h] * col(e_g, h)).astype(bf16) for h in heads])
        k_dec.append([(ks[h] * col(e_rest, h)).astype(bf16) for h in heads])
    xs = _unit_lower_solve(n_mats, rhss)

    states = [s_ref[h] for h in heads]
    o_gain = onorm_ref[...]
    for c in range(n_chunks):
        xc = xs[c * DN_HEADS:(c + 1) * DN_HEADS]
        sbs = [S.astype(bf16) for S in states]
        v_news = [x[:, :DN_HD] - mm(x[:, DN_HD:].astype(bf16), sb) for x, sb in zip(xc, sbs)]
        vbs = [v.astype(bf16) for v in v_news]
        o_state = [mm(qd, sb) for qd, sb in zip(q_dec[c], sbs)]
        o_local = [mm(qk, vb) for qk, vb in zip(qks[c], vbs)]
        grow = [lax.dot_general(kd, vb, (((0,), (0,)), ((), ())), preferred_element_type=f32)
                for kd, vb in zip(k_dec[c], vbs)]
        states = [col(e_last[c], h) * states[h] + grow[h] for h in heads]
        n_out = min(C, n_in)
        for h in heads:
            o = o_state[h] + o_local[h]
            o = o * lax.rsqrt(jnp.mean(o * o, axis=-1, keepdims=True) + RMS_EPS) * o_gain
            o_ref[c * C:c * C + n_out, h * DN_HD:(h + 1) * DN_HD] = o[:n_out]
    for h in heads:
        s_ref[h] = states[h]

    @pl.when(n == last)
    def _():
        s_out_ref[...] = s_ref[...]


def _gated_delta_net(proj, B, T, buf, S0, conv_w, A_log, dt_bias, o_norm):
    n_in = min(DN_CHUNKS_PER_STEP * DN_CHUNK, T)
    assert T % n_in == 0 and n_in % SUBLANES == 0 and (n_in <= DN_CHUNK or n_in % DN_CHUNK == 0)
    rows_pad = -(-n_in // DN_CHUNK) * DN_CHUNK
    qkv0 = _SEG_DST["dn_qkv"] // DN_WIDTH
    assert _SEG_DST["dn_qkv"] % DN_WIDTH == 0 and _SEG_DST["dn_b"] == _SEG_DST["dn_a"] + DN_HEADS
    ab_blk = _SEG_DST["dn_a"] // LANE
    proj3 = proj.reshape(B, T, IN_COLS_PAD)
    col = lambda c: (lambda b, n: (b, n, c))
    per_b = lambda b, n: (b, 0, 0)
    const = lambda b, n: (0, 0)
    pad_row = lambda a, off: jnp.zeros((1, LANE), jnp.float32).at[0, off:off + DN_HEADS].set(a)
    o, s_out, cbuf = pl.pallas_call(
        functools.partial(_dn_kernel, n_in=n_in),
        out_shape=(jax.ShapeDtypeStruct((B, T, DN_WIDTH), jnp.float32),
                   jax.ShapeDtypeStruct((B, DN_HEADS, DN_HD, DN_HD), jnp.float32),
                   jax.ShapeDtypeStruct((B, CONV_W - 1, 3 * DN_WIDTH), jnp.float32)),
        grid=(B, T // n_in),
        in_specs=[
            pl.BlockSpec((None, n_in, DN_WIDTH), col(qkv0)),
            pl.BlockSpec((None, n_in, DN_WIDTH), col(qkv0 + 1)),
            pl.BlockSpec((None, n_in, DN_WIDTH), col(qkv0 + 2)),
            pl.BlockSpec((None, n_in, LANE), col(ab_blk)),
            pl.BlockSpec((None, CONV_W - 1, 3 * DN_WIDTH), per_b),
            pl.BlockSpec((None, DN_HEADS, DN_HD, DN_HD), lambda b, n: (b, 0, 0, 0)),
            pl.BlockSpec((CONV_W, 3 * DN_WIDTH), const),
            pl.BlockSpec((1, LANE), const),
            pl.BlockSpec((1, LANE), const),
            pl.BlockSpec((1, DN_HD), const),
        ],
        out_specs=(pl.BlockSpec((None, n_in, DN_WIDTH), lambda b, n: (b, n, 0)),
                   pl.BlockSpec((None, DN_HEADS, DN_HD, DN_HD), lambda b, n: (b, 0, 0, 0)),
                   pl.BlockSpec((None, CONV_W - 1, 3 * DN_WIDTH), per_b)),
        scratch_shapes=[pltpu.VMEM((rows_pad + 2 * SUBLANES, 3 * DN_WIDTH), jnp.float32),
                        pltpu.VMEM((DN_HEADS, DN_HD, DN_HD), jnp.float32)],
        compiler_params=pltpu.CompilerParams(dimension_semantics=("parallel", "arbitrary")),
        name="gated_delta",
    )(proj3, proj3, proj3, proj3, buf, S0, conv_w, pad_row(A_log, 0), pad_row(dt_bias, 0),
      o_norm.reshape(1, DN_HD))
    return o, s_out, cbuf


def _layer(x, paged, lru_h0, lru_buf, dn_S0, dn_buf,
           norm_g, w_in_pad, lru_conv_w, lru_conv_b, lru_wa, lru_ba, lru_wx, lru_bx, lru_lam,
           q_norm, k_norm, dn_conv_w, dn_A_log, dn_dt_bias, dn_o_norm,
           w_lru_out, w_nsa_out, w_dn_out, w_out):
    B, T, _ = x.shape
    P = 0 if paged is None else paged[0].shape[1] * PAGE_SIZE
    pos = P + jnp.arange(T, dtype=jnp.int32)
    x2d = x.reshape(B * T, D_MODEL)
    proj = _in_proj(x2d, norm_g, w_in_pad)
    proj3 = proj.reshape(B, T, IN_COLS_PAD)

    lru_seq, lru_h, lru_buf_new = _lru(proj, B, T, lru_buf, lru_h0, lru_conv_w, lru_conv_b,
                                       lru_wa, lru_ba, lru_wx, lru_bx, lru_lam)

    q, rows, win_rows, kvb = _nsa_prep(proj3, pos, q_norm, k_norm)
    if paged is None:
        o_nsa = _nsa_prompt(q, proj3, rows, kvb)
        win_state = win_rows[:, -min(WINDOW, T):]
    else:
        page_table, cache, layer, win_prev = paged
        win_prev = win_prev.reshape(B, win_prev.shape[1], 2 * KV_LANES)
        kvb_new = jnp.pad(kvb, ((0, 0), (0, NEW_PAD - T), (0, 0)))
        o_nsa = _nsa_sample(page_table, cache, layer, q, proj3, kvb_new, win_prev)
        win_state = jnp.concatenate([win_prev, win_rows], axis=1)[:, -WINDOW:]
    rows = rows.reshape(B, T, 4, NSA_KV, NSA_HD)
    win_state = win_state.reshape(B, win_state.shape[1], 2, NSA_KV, NSA_HD)

    o_dn, dn_S, dn_buf_new = _gated_delta_net(proj, B, T, dn_buf, dn_S0, dn_conv_w, dn_A_log, dn_dt_bias,
                                              dn_o_norm)

    y2d = _out_stage(x2d, proj, lru_seq.reshape(B * T, D_RNN), o_nsa.reshape(B * T, NSA_WIDTH),
                     o_dn.reshape(B * T, DN_WIDTH), w_lru_out, w_nsa_out, w_dn_out, w_out)
    return y2d.reshape(B, T, D_MODEL), (rows, win_state, lru_h, lru_buf_new, dn_S, dn_buf_new)


def kernel(x_prompt, x_sample, cache_nsa_kv, page_table, state_nsa_win, state_lru_h, state_lru_conv,
           state_dn_S, state_dn_conv, norm_gain, w_in, lru_conv_w, lru_conv_b, lru_wa, lru_ba,
           lru_wx, lru_bx, lru_lambda, nsa_q_norm, nsa_k_norm, dn_conv_w, dn_A_log, dn_dt_bias,
           dn_o_norm, w_lru_out, w_nsa_out, w_dn_out, w_out):
    Bp = x_prompt.shape[0]
    dt = x_prompt.dtype
    bf16 = jnp.bfloat16
    cache = cache_nsa_kv.reshape(cache_nsa_kv.shape[:2] + (PAGE_SIZE, 4 * KV_LANES))
    p_h0 = jnp.zeros((Bp, D_RNN), jnp.float32)
    p_lbuf0 = jnp.zeros((Bp, CONV_W - 1, D_RNN), dt)
    p_S0 = jnp.zeros((Bp, DN_HEADS, DN_HD, DN_HD), jnp.float32)
    p_dbuf0 = jnp.zeros((Bp, CONV_W - 1, 3 * DN_WIDTH), dt)
    wa_bd = jax.vmap(_block_diag)(lru_wa).astype(bf16)
    wx_bd = jax.vmap(_block_diag)(lru_wx).astype(bf16)
    weights = (norm_gain, _pad_in_weight(w_in), lru_conv_w, lru_conv_b, wa_bd, lru_ba, wx_bd, lru_bx,
               lru_lambda, nsa_q_norm, nsa_k_norm, dn_conv_w, dn_A_log, dn_dt_bias, dn_o_norm,
               w_lru_out.astype(bf16), _heads_rg(w_nsa_out.swapaxes(1, 2)).swapaxes(1, 2).astype(bf16),
               w_dn_out.astype(bf16), w_out.astype(bf16))
    xp, xs = x_prompt, x_sample
    st_p, st_s = [], []
    for l in range(DEPTH):
        lw = [w[l] for w in weights]
        xp, sp = _layer(xp, None, p_h0, p_lbuf0, p_S0, p_dbuf0, *lw)
        xs, ss = _layer(xs, (page_table, cache, l, state_nsa_win[l]), state_lru_h[l], state_lru_conv[l],
                        state_dn_S[l], state_dn_conv[l], *lw)
        st_p.append(sp)
        st_s.append(ss)
    kv_p, win_p, lh_p, lc_p, S_p, dc_p = [jnp.stack(a) for a in zip(*st_p)]
    kv_s, win_s, lh_s, lc_s, S_s, dc_s = [jnp.stack(a) for a in zip(*st_s)]
    return (xp, xs, kv_p, kv_s, win_p, win_s, lh_p, lh_s, lc_p, lc_s, S_p, S_s, dc_p, dc_s)
```

```python
import functools
import math

import numpy as np
import jax
import jax.numpy as jnp
from jax import lax
from jax.experimental import pallas as pl
from jax.experimental.pallas import tpu as pltpu

D_MODEL = 1024
DEPTH = 4
PAGE_SIZE = 128
CONV_W = 4
RMS_EPS = 1e-6
L2_EPS = 1e-6
N_BRANCH = 3
D_RNN = D_MODEL // 2
LRU_BLOCKS = 8
LRU_BS = D_RNN // LRU_BLOCKS
LRU_C = 8.0
NSA_HEADS = 8
NSA_HD = 64
NSA_KV = 2
NSA_REP = NSA_HEADS // NSA_KV
NSA_WIDTH = NSA_HEADS * NSA_HD
CMP_BLOCK = 32
SLC_BLOCK = 64
N_SEL = 16
WINDOW = 512
Q_BLOCK = 128
ROT_DIM = NSA_HD // 4
ROPE_THETA = 500000.0
DN_HEADS = 4
DN_HD = 128
DN_WIDTH = DN_HEADS * DN_HD
DN_CHUNK = 64

LANE = 128

_SEG_NAMES = ("lru_x", "lru_z", "nsa_q", "nsa_kv", "nsa_z", "nsa_g", "dn_qkv", "dn_z", "dn_a", "dn_b", "merge_g")
_SEG_SIZES = (D_RNN, D_RNN, NSA_WIDTH, 2 * N_BRANCH * NSA_KV * NSA_HD, NSA_WIDTH, N_BRANCH * NSA_HEADS,
              3 * DN_WIDTH, DN_WIDTH, DN_HEADS, DN_HEADS, N_BRANCH * D_MODEL)
_SEG_SRC = dict(zip(_SEG_NAMES, np.concatenate([[0], np.cumsum(_SEG_SIZES)[:-1]]).tolist()))
_SEG_LEN = dict(zip(_SEG_NAMES, _SEG_SIZES))
_DST_SLOTS = (
    (("lru_x",), D_RNN), (("lru_z",), D_RNN), (("nsa_q",), NSA_WIDTH), (("nsa_z",), NSA_WIDTH),
    (("dn_z",), DN_WIDTH), (("nsa_g",), LANE), (("dn_a", "dn_b"), LANE),
    (("nsa_kv",), 2 * N_BRANCH * NSA_KV * NSA_HD), (("dn_qkv",), 3 * DN_WIDTH), (("merge_g",), N_BRANCH * D_MODEL),
)


def _dst_layout():
    off, dst = 0, {}
    for names, width in _DST_SLOTS:
        o = off
        for n in names:
            dst[n] = o
            o += _SEG_LEN[n]
        off += width
    return dst, off


_SEG_DST, IN_COLS_PAD = _dst_layout()


def _heads_rg(a):
    lead = a.shape[:-1]
    return a.reshape(lead + (NSA_KV, NSA_REP, NSA_HD)).swapaxes(-3, -2).reshape(lead + (NSA_WIDTH,))


def _pad_in_weight(w_in):
    pieces = []
    for names, width in _DST_SLOTS:
        used = 0
        for n in names:
            piece = w_in[:, :, _SEG_SRC[n]:_SEG_SRC[n] + _SEG_LEN[n]]
            pieces.append(_heads_rg(piece) if n in ("nsa_q", "nsa_z") else piece)
            used += _SEG_LEN[n]
        if used < width:
            pieces.append(jnp.zeros(w_in.shape[:2] + (width - used,), w_in.dtype))
    return jnp.concatenate(pieces, axis=-1).astype(jnp.bfloat16)


IN_PROJ_ROWS = 2048
IN_PROJ_COLS = 512


def _in_proj_kernel(x_ref, g_ref, w_ref, o_ref, h_ref):
    @pl.when(pl.program_id(1) == 0)
    def _():
        x = x_ref[...]
        ms = jnp.mean(x * x, axis=-1, keepdims=True)
        h_ref[...] = (x * lax.rsqrt(ms + RMS_EPS) * g_ref[...]).astype(jnp.bfloat16)

    o_ref[...] = jnp.dot(h_ref[...], w_ref[...], preferred_element_type=jnp.float32)


def _in_proj(x2d, gain, w_pad):
    m = x2d.shape[0]
    tm = min(IN_PROJ_ROWS, m)
    tn = IN_PROJ_COLS
    vmem = 2 * (tm * D_MODEL * 4 + D_MODEL * tn * 2 + tm * tn * 4) + tm * D_MODEL * 2 + (4 << 20)
    return pl.pallas_call(
        _in_proj_kernel,
        out_shape=jax.ShapeDtypeStruct((m, IN_COLS_PAD), jnp.float32),
        grid=(m // tm, IN_COLS_PAD // tn),
        in_specs=[
            pl.BlockSpec((tm, D_MODEL), lambda i, j: (i, 0)),
            pl.BlockSpec((1, D_MODEL), lambda i, j: (0, 0)),
            pl.BlockSpec((D_MODEL, tn), lambda i, j: (0, j)),
        ],
        out_specs=pl.BlockSpec((tm, tn), lambda i, j: (i, j)),
        scratch_shapes=[pltpu.VMEM((tm, D_MODEL), jnp.bfloat16)],
        compiler_params=pltpu.CompilerParams(dimension_semantics=("parallel", "arbitrary"),
                                             vmem_limit_bytes=vmem),
        name="in_proj",
    )(x2d, gain.reshape(1, D_MODEL), w_pad)


def _silu(z):
    return z * jax.nn.sigmoid(z)


def _out_stage_kernel(x_ref, a_lru, z_lru, a_nsa, z_nsa, a_dn, z_dn, mg0, mg1, mg2,
                      w_lru, w_nsa, w_dn, w_out, o_ref):
    def branch(a, z, w):
        y = (a[...] * _silu(z[...])).astype(jnp.bfloat16)
        return jnp.dot(y, w[...], preferred_element_type=jnp.float32)

    merged = jax.nn.sigmoid(mg0[...]) * branch(a_lru, z_lru, w_lru)
    merged = merged + jax.nn.sigmoid(mg1[...]) * branch(a_nsa, z_nsa, w_nsa)
    merged = merged + jax.nn.sigmoid(mg2[...]) * branch(a_dn, z_dn, w_dn)
    y = jnp.dot(merged.astype(jnp.bfloat16), w_out[...], preferred_element_type=jnp.float32)
    o_ref[...] = x_ref[...] + y


def _out_stage(x2d, proj, a_lru, a_nsa, a_dn, w_lru, w_nsa, w_dn, w_out):
    m = x2d.shape[0]
    tm = min(256, m)
    half = D_RNN
    row = lambda i: (i, 0)
    col = lambda c: (lambda i: (i, c))
    full = lambda i: (0, 0)
    mg_blk = _SEG_DST["merge_g"] // D_MODEL
    return pl.pallas_call(
        _out_stage_kernel,
        out_shape=jax.ShapeDtypeStruct((m, D_MODEL), jnp.float32),
        grid=(m // tm,),
        in_specs=[
            pl.BlockSpec((tm, D_MODEL), row),
            pl.BlockSpec((tm, half), row),
            pl.BlockSpec((tm, half), col(_SEG_DST["lru_z"] // half)),
            pl.BlockSpec((tm, half), row),
            pl.BlockSpec((tm, half), col(_SEG_DST["nsa_z"] // half)),
            pl.BlockSpec((tm, half), row),
            pl.BlockSpec((tm, half), col(_SEG_DST["dn_z"] // half)),
            pl.BlockSpec((tm, D_MODEL), col(mg_blk)),
            pl.BlockSpec((tm, D_MODEL), col(mg_blk + 1)),
            pl.BlockSpec((tm, D_MODEL), col(mg_blk + 2)),
            pl.BlockSpec((half, D_MODEL), full),
            pl.BlockSpec((half, D_MODEL), full),
            pl.BlockSpec((half, D_MODEL), full),
            pl.BlockSpec((D_MODEL, D_MODEL), full),
        ],
        out_specs=pl.BlockSpec((tm, D_MODEL), row),
        compiler_params=pltpu.CompilerParams(dimension_semantics=("parallel",)),
        name="out_stage",
    )(x2d, a_lru, proj, a_nsa, proj, a_dn, proj, proj, proj, proj, w_lru, w_nsa, w_dn, w_out)


PREP_TILE = 512
HALF_ROT = ROT_DIM // 2


def _rope_tables(pos):
    inv = ROPE_THETA ** (-jnp.arange(HALF_ROT, dtype=jnp.float32) * 2.0 / ROT_DIM)
    ang = pos.astype(jnp.float32)[:, None] * inv[None, :]
    c, s = jnp.cos(ang), jnp.sin(ang)
    n = pos.shape[0]
    rest = NSA_HD - ROT_DIM
    cos_h = jnp.concatenate([c, c, jnp.ones((n, rest), jnp.float32)], axis=1)
    sin_h = jnp.concatenate([-s, s, jnp.zeros((n, rest), jnp.float32)], axis=1)
    return jnp.tile(cos_h, (1, NSA_KV)), jnp.tile(sin_h, (1, NSA_KV))


def _nsa_prep_kernel(q_ref, kv0_ref, kv1_ref, kv2_ref, cos_ref, sin_ref, qn_ref, kn_ref, ones_ref,
                     q_out, rows_out, win_out, kvb_out):
    f32, bf16 = jnp.float32, jnp.bfloat16
    cos, sin = cos_ref[...], sin_ref[...]
    ones = ones_ref[...]
    n = cos.shape[0]
    first = (lax.broadcasted_iota(jnp.int32, (n, KV_LANES), 1) % NSA_HD) < HALF_ROT

    def norm_rope(x, gain):
        sq = x * x
        hi = sq.astype(bf16)
        lo = (sq - hi.astype(f32)).astype(bf16)
        ssq = (jnp.dot(hi, ones, preferred_element_type=f32) + jnp.dot(lo, ones, preferred_element_type=f32))
        y = x * lax.rsqrt(ssq * (1.0 / NSA_HD) + RMS_EPS) * gain
        partner = jnp.where(first, pltpu.roll(y, KV_LANES - HALF_ROT, axis=1), pltpu.roll(y, HALF_ROT, axis=1))
        return y * cos + partner * sin

    qn = qn_ref[...]
    for r in range(NSA_REP):
        lanes = slice(r * KV_LANES, (r + 1) * KV_LANES)
        q_out[:, lanes] = norm_rope(q_ref[:, lanes], qn) * (NSA_HD ** -0.5)
    kn = kn_ref[...]
    k_cmp = norm_rope(kv0_ref[:, :KV_LANES], kn[0:1])
    k_slc = norm_rope(kv1_ref[:, :KV_LANES], kn[1:2])
    k_win = norm_rope(kv2_ref[:, :KV_LANES], kn[2:3])
    v_slc = kv1_ref[:, KV_LANES:]
    v_win = kv2_ref[:, KV_LANES:]
    rows_out[:, 0:KV_LANES] = k_cmp
    rows_out[:, KV_LANES:2 * KV_LANES] = kv0_ref[:, KV_LANES:]
    rows_out[:, 2 * KV_LANES:3 * KV_LANES] = k_slc
    rows_out[:, 3 * KV_LANES:4 * KV_LANES] = v_slc
    win_out[:, 0:KV_LANES] = k_win
    win_out[:, KV_LANES:2 * KV_LANES] = v_win
    kvb_out[:, 0:KV_LANES] = k_slc.astype(bf16)
    kvb_out[:, KV_LANES:2 * KV_LANES] = v_slc.astype(bf16)
    kvb_out[:, 2 * KV_LANES:3 * KV_LANES] = k_win.astype(bf16)
    kvb_out[:, 3 * KV_LANES:4 * KV_LANES] = v_win.astype(bf16)


def _nsa_prep(proj3, pos, q_norm, k_norm):
    B, T, _ = proj3.shape
    tile = min(PREP_TILE, T)
    assert T % tile == 0
    pair = 2 * KV_LANES
    q_blk = _SEG_DST["nsa_q"] // NSA_WIDTH
    kv_blk = _SEG_DST["nsa_kv"] // pair
    assert _SEG_DST["nsa_q"] % NSA_WIDTH == 0 and _SEG_DST["nsa_kv"] % pair == 0
    cos_t, sin_t = _rope_tables(pos)
    lane_head = jnp.arange(KV_LANES) // NSA_HD
    ones = (lane_head[:, None] == lane_head[None, :]).astype(jnp.bfloat16)
    col = lambda c: (lambda b, t: (b, t, c))
    tab = pl.BlockSpec((tile, KV_LANES), lambda b, t: (t, 0))
    const = lambda b, t: (0, 0)
    out = lambda w: pl.BlockSpec((None, tile, w), lambda b, t: (b, t, 0))
    return pl.pallas_call(
        _nsa_prep_kernel,
        out_shape=(jax.ShapeDtypeStruct((B, T, NSA_WIDTH), jnp.float32),
                   jax.ShapeDtypeStruct((B, T, 4 * KV_LANES), jnp.float32),
                   jax.ShapeDtypeStruct((B, T, 2 * KV_LANES), jnp.float32),
                   jax.ShapeDtypeStruct((B, T, 4 * KV_LANES), jnp.bfloat16)),
        grid=(B, T // tile),
        in_specs=[
            pl.BlockSpec((None, tile, NSA_WIDTH), col(q_blk)),
            pl.BlockSpec((None, tile, pair), col(kv_blk)),
            pl.BlockSpec((None, tile, pair), col(kv_blk + 1)),
            pl.BlockSpec((None, tile, pair), col(kv_blk + 2)),
            tab, tab,
            pl.BlockSpec((1, KV_LANES), const),
            pl.BlockSpec((N_BRANCH, KV_LANES), const),
            pl.BlockSpec((KV_LANES, KV_LANES), const),
        ],
        out_specs=(out(NSA_WIDTH), out(4 * KV_LANES), out(2 * KV_LANES), out(4 * KV_LANES)),
        compiler_params=pltpu.CompilerParams(dimension_semantics=("parallel", "parallel")),
        name="nsa_prep",
    )(proj3, proj3, proj3, proj3, cos_t, sin_t, jnp.tile(q_norm, NSA_KV).reshape(1, KV_LANES),
      jnp.tile(k_norm, (1, NSA_KV)), ones)


NEG_BIG = -1e30
SLC_CHUNK = 512
N_ROWS = NSA_KV * NSA_REP * Q_BLOCK
KV_LANES = NSA_KV * NSA_HD
WIN_KEYS = WINDOW + Q_BLOCK
MAX_SLC_BLOCKS = LANE // 2


def _dot_t(a, b):
    return lax.dot_general(a, b, (((1,), (1,)), ((), ())), preferred_element_type=jnp.float32)


def _select_blocks(imp_t, cur, n_top):
    nb = imp_t.shape[0]
    jj = lax.broadcasted_iota(jnp.int32, imp_t.shape, 0)
    v = jnp.where((jj == 0) | (jj == cur), jnp.inf, jnp.where(jj > cur, -jnp.inf, imp_t))
    sub = lax.broadcasted_iota(jnp.int32, (8, imp_t.shape[1]), 0)
    ranks = []
    for a in range(nb // 8):
        va = v[8 * a:8 * a + 8]
        rank = jnp.zeros(va.shape, jnp.float32)
        for j in range(nb):
            row = v[j:j + 1]
            ge = jnp.where(row >= va, 1.0, 0.0)
            gt = jnp.where(row > va, 1.0, 0.0)
            if j < 8 * a:
                rank = rank + ge
            elif j >= 8 * a + 8:
                rank = rank + gt
            else:
                rank = rank + jnp.where(sub > (j - 8 * a), ge, gt)
        ranks.append(rank)
    rank = jnp.concatenate(ranks, axis=0)
    return jnp.where((rank < n_top) & (jj <= cur), 1.0, 0.0)


def _nsa_prompt_kernel(q_ref, gate_ref, kc_ref, vc_ref, ks_ref, vs_ref, kw_ref, vw_ref, o_ref,
                       kcb_ref, vcb_ref, m_ref, l_ref, acc_ref, *, seq_len, n_top):
    f32, bf16 = jnp.float32, jnp.bfloat16
    i = pl.program_id(1)
    nsb = seq_len // SLC_BLOCK
    half = MAX_SLC_BLOCKS

    @pl.when(i == 0)
    def _():
        if nsb < half:
            kcb_ref[...] = jnp.zeros(kcb_ref.shape, f32)
            vcb_ref[...] = jnp.zeros(vcb_ref.shape, f32)
        for src, dst in ((kc_ref, kcb_ref), (vc_ref, vcb_ref)):
            x = src[...].reshape(nsb, SLC_BLOCK, KV_LANES)
            dst[0:nsb, :] = jnp.sum(x[:, :CMP_BLOCK, :], axis=1) * (1.0 / CMP_BLOCK)
            dst[half:half + nsb, :] = jnp.sum(x[:, CMP_BLOCK:, :], axis=1) * (1.0 / CMP_BLOCK)

    lane = lax.broadcasted_iota(jnp.int32, (Q_BLOCK, LANE), 1)
    tq = lax.broadcasted_iota(jnp.int32, (Q_BLOCK, LANE), 0)
    pos = i * Q_BLOCK + tq
    low = lane < NSA_HD

    q = q_ref[...]
    parts = []
    for g in range(NSA_KV):
        for r in range(NSA_REP):
            qr = q[:, r * KV_LANES:(r + 1) * KV_LANES]
            parts.append(jnp.where(low if g == 0 else ~low, qr, 0.0))
    qpad = jnp.concatenate(parts, axis=0).astype(bf16)

    s = _dot_t(qpad, kcb_ref[...].astype(bf16))
    cblk = jnp.where(lane < half, 2 * lane, 2 * lane - (2 * half - 1))
    okc = (cblk * CMP_BLOCK + (CMP_BLOCK - 1)) <= pos
    s3 = s.reshape(NSA_KV * NSA_REP, Q_BLOCK, LANE) + jnp.where(okc, 0.0, NEG_BIG)[None]
    mx = jnp.max(s3, axis=-1, keepdims=True)
    e = jnp.where(okc[None], jnp.exp(s3 - mx), 0.0)
    den = jnp.sum(e, axis=-1, keepdims=True)
    p3 = e / jnp.where(den > 0.0, den, 1.0)
    o_cmp = jnp.dot(p3.reshape(N_ROWS, LANE).astype(bf16), vcb_ref[...].astype(bf16),
                    preferred_element_type=f32)

    pg = p3.reshape(NSA_KV, NSA_REP, Q_BLOCK, LANE).sum(axis=1)
    cur_t = (i * Q_BLOCK + lax.broadcasted_iota(jnp.int32, (half, Q_BLOCK), 1)) // SLC_BLOCK
    selq = []
    for g in range(NSA_KV):
        imp = pg[g] + pltpu.roll(pg[g], half, axis=1)
        sel_t = _select_blocks(imp.T[:half], cur_t, n_top)
        sel_full = jnp.concatenate([sel_t, jnp.zeros_like(sel_t)], axis=0)
        selq.append(sel_full.T[:, :half].astype(bf16))

    m_ref[...] = jnp.full(m_ref.shape, NEG_BIG, f32)
    l_ref[...] = jnp.zeros(l_ref.shape, f32)
    acc_ref[...] = jnp.zeros(acc_ref.shape, f32)
    heads = (NSA_KV, NSA_REP, Q_BLOCK)

    def slc_step(c, carry):
        start = pl.multiple_of(c * SLC_CHUNK, SLC_CHUNK)
        kch = ks_ref[pl.ds(start, SLC_CHUNK), :]
        vch = vs_ref[pl.ds(start, SLC_CHUNK), :]
        sc = _dot_t(qpad, kch)
        kidx = start + lax.broadcasted_iota(jnp.int32, (Q_BLOCK, SLC_CHUNK), 1)
        causal = kidx <= i * Q_BLOCK + lax.broadcasted_iota(jnp.int32, (Q_BLOCK, SLC_CHUNK), 0)
        kblk = (start + lax.broadcasted_iota(jnp.int32, (half, SLC_CHUNK), 1)) // SLC_BLOCK
        expand = jnp.where(kblk == lax.broadcasted_iota(jnp.int32, (half, SLC_CHUNK), 0), 1.0, 0.0).astype(bf16)
        bias = []
        for g in range(NSA_KV):
            picked = jnp.dot(selq[g], expand, preferred_element_type=f32) > 0.5
            bias.append(jnp.where(picked & causal, 0.0, NEG_BIG))
        bias = jnp.stack(bias, axis=0)[:, None]
        s4 = sc.reshape(heads + (SLC_CHUNK,)) + bias
        m_old = m_ref[...].reshape(heads + (1,))
        m_new = jnp.maximum(m_old, jnp.max(s4, axis=-1, keepdims=True))
        alpha = jnp.exp(m_old - m_new)
        ex = jnp.exp(s4 - m_new)
        l_new = alpha * l_ref[...].reshape(heads + (1,)) + jnp.sum(ex, axis=-1, keepdims=True)
        pv = jnp.dot(ex.reshape(N_ROWS, SLC_CHUNK).astype(bf16), vch, preferred_element_type=f32)
        acc_ref[...] = alpha.reshape(N_ROWS, 1) * acc_ref[...] + pv
        l_ref[...] = l_new.reshape(N_ROWS, 1)
        m_ref[...] = m_new.reshape(N_ROWS, 1)
        return carry

    n_chunks = (i * Q_BLOCK + Q_BLOCK + SLC_CHUNK - 1) // SLC_CHUNK
    lax.fori_loop(0, n_chunks, slc_step, 0)
    o_slc = acc_ref[...] / l_ref[...]

    wstart = pl.multiple_of(jnp.maximum(i - WINDOW // Q_BLOCK, 0) * Q_BLOCK, Q_BLOCK)
    kwin = kw_ref[pl.ds(wstart, WIN_KEYS), :]
    vwin = vw_ref[pl.ds(wstart, WIN_KEYS), :]
    sw = _dot_t(qpad, kwin)
    widx = wstart + lax.broadcasted_iota(jnp.int32, (Q_BLOCK, WIN_KEYS), 1)
    wpos = i * Q_BLOCK + lax.broadcasted_iota(jnp.int32, (Q_BLOCK, WIN_KEYS), 0)
    okw = (widx <= wpos) & (wpos - widx < WINDOW)
    sw3 = sw.reshape(NSA_KV * NSA_REP, Q_BLOCK, WIN_KEYS) + jnp.where(okw, 0.0, NEG_BIG)[None]
    ew = jnp.exp(sw3 - jnp.max(sw3, axis=-1, keepdims=True))
    lw = jnp.sum(ew, axis=-1, keepdims=True).reshape(N_ROWS, 1)
    o_win = jnp.dot(ew.reshape(N_ROWS, WIN_KEYS).astype(bf16), vwin, preferred_element_type=f32) / lw

    gate = jax.nn.sigmoid(gate_ref[...])
    n_hd = NSA_KV * NSA_REP
    for r in range(NSA_REP):
        per_g = []
        for g in range(NSA_KV):
            h = g * NSA_REP + r
            rows = slice(h * Q_BLOCK, (h + 1) * Q_BLOCK)
            per_g.append(gate[:, h:h + 1] * o_cmp[rows]
                         + gate[:, n_hd + h:n_hd + h + 1] * o_slc[rows]
                         + gate[:, 2 * n_hd + h:2 * n_hd + h + 1] * o_win[rows])
        o_ref[:, r * KV_LANES:(r + 1) * KV_LANES] = jnp.where(low, per_g[0], per_g[1])


def _gate_block(rows):
    c = _SEG_DST["nsa_g"] // LANE
    assert _SEG_DST["nsa_g"] % LANE == 0
    return pl.BlockSpec((None, rows, LANE), lambda b, i, *_: (b, i, c))


def _nsa_prompt(q, proj3, rows, kvb):
    B, T, _ = q.shape
    assert T % SLC_CHUNK == 0 and T >= WIN_KEYS and T // SLC_BLOCK <= MAX_SLC_BLOCKS
    n_top = min(N_SEL, T // SLC_BLOCK)
    blk = lambda b, i: (b, i, 0)
    kv_col = lambda c: pl.BlockSpec((None, T, KV_LANES), lambda b, i: (b, 0, c))
    return pl.pallas_call(
        functools.partial(_nsa_prompt_kernel, seq_len=T, n_top=n_top),
        out_shape=jax.ShapeDtypeStruct((B, T, NSA_WIDTH), jnp.float32),
        grid=(B, T // Q_BLOCK),
        in_specs=[
            pl.BlockSpec((None, Q_BLOCK, NSA_WIDTH), blk),
            _gate_block(Q_BLOCK),
            kv_col(0), kv_col(1), kv_col(0), kv_col(1), kv_col(2), kv_col(3),
        ],
        out_specs=pl.BlockSpec((None, Q_BLOCK, NSA_WIDTH), blk),
        scratch_shapes=[
            pltpu.VMEM((LANE, KV_LANES), jnp.float32),
            pltpu.VMEM((LANE, KV_LANES), jnp.float32),
            pltpu.VMEM((N_ROWS, 1), jnp.float32),
            pltpu.VMEM((N_ROWS, 1), jnp.float32),
            pltpu.VMEM((N_ROWS, KV_LANES), jnp.float32),
        ],
        compiler_params=pltpu.CompilerParams(
            dimension_semantics=("parallel", "arbitrary"), vmem_limit_bytes=48 * 1024 * 1024),
        name="nsa_prompt",
    )(q, proj3, rows, rows, kvb, kvb, kvb, kvb)


PAGES_PER_STEP = 8
NEW_PAD = 128


def _qpad_rows(q, nq):
    low = lax.broadcasted_iota(jnp.int32, (nq, LANE), 1) < NSA_HD
    parts = []
    for g in range(NSA_KV):
        for r in range(NSA_REP):
            qr = q[:, r * KV_LANES:(r + 1) * KV_LANES]
            parts.append(jnp.where(low if g == 0 else ~low, qr, 0.0))
    return jnp.concatenate(parts, axis=0).astype(jnp.bfloat16)


def _nsa_sample_kernel(pt_ref, *refs, n_q, past_len, n_top):
    f32, bf16 = jnp.float32, jnp.bfloat16
    pages = refs[:PAGES_PER_STEP]
    (q_ref, gate_ref, ksn_ref, vsn_ref, win_ref, kwn_ref, vwn_ref, o_ref,
     kcb_ref, vcb_ref, sel_ref, ocmp_ref, m_ref, l_ref, acc_ref, stash_ref) = refs[PAGES_PER_STEP:]
    ph = pl.program_id(1)
    p = pl.program_id(2)
    last = pl.num_programs(2) - 1
    n_rows = NSA_KV * NSA_REP * n_q
    n_blk = past_len // SLC_BLOCK
    cmp_per_step = PAGES_PER_STEP * PAGE_SIZE // CMP_BLOCK
    keys_per_step = PAGES_PER_STEP * PAGE_SIZE
    heads = (NSA_KV, NSA_REP, n_q)
    qpad = _qpad_rows(q_ref[...], n_q)

    @pl.when(ph == 0)
    def _():
        pair = 2 * KV_LANES
        means = [pg[:, :pair].reshape(PAGE_SIZE // CMP_BLOCK, CMP_BLOCK, pair).sum(axis=1) * (1.0 / CMP_BLOCK)
                 for pg in pages]
        means = jnp.concatenate(means, axis=0)
        row0 = pl.multiple_of(p * cmp_per_step, cmp_per_step)
        kcb_ref[pl.ds(row0, cmp_per_step), :] = means[:, :KV_LANES]
        vcb_ref[pl.ds(row0, cmp_per_step), :] = means[:, KV_LANES:]
        key0 = pl.multiple_of(p * keys_per_step, keys_per_step)
        for k, pg in enumerate(pages):
            stash_ref[pl.ds(key0 + k * PAGE_SIZE, PAGE_SIZE), :] = pg[:, pair:].astype(bf16)

    @pl.when((ph == 0) & (p == last))
    def _():
        halves = []
        for par in range(2):
            kc = kcb_ref[pl.ds(par, n_blk, stride=2), :].astype(bf16)
            halves.append(_dot_t(qpad, kc))
        mx = jnp.maximum(jnp.max(halves[0], axis=-1, keepdims=True), jnp.max(halves[1], axis=-1, keepdims=True))
        e0, e1 = jnp.exp(halves[0] - mx), jnp.exp(halves[1] - mx)
        inv = 1.0 / (jnp.sum(e0, axis=-1, keepdims=True) + jnp.sum(e1, axis=-1, keepdims=True))
        p0, p1 = e0 * inv, e1 * inv
        oc = jnp.zeros((n_rows, KV_LANES), f32)
        for par, pp in ((0, p0), (1, p1)):
            vc = vcb_ref[pl.ds(par, n_blk, stride=2), :].astype(bf16)
            oc = oc + jnp.dot(pp.astype(bf16), vc, preferred_element_type=f32)
        ocmp_ref[...] = oc
        imp = (p0 + p1).reshape(heads + (n_blk,)).sum(axis=1).reshape(NSA_KV * n_q, n_blk)
        lane = lax.broadcasted_iota(jnp.int32, imp.shape, 1)
        v = jnp.where(lane == 0, jnp.inf, imp)
        rank = jnp.zeros(imp.shape, f32)
        for j in range(n_blk):
            col = v[:, j:j + 1]
            tie = jnp.where(lane > j, 1.0, 0.0)
            rank = rank + jnp.where(col > v, 1.0, jnp.where(col == v, tie, 0.0))
        sel_ref[...] = jnp.where(rank < n_top - 1, 1.0, 0.0)
        m_ref[...] = jnp.full(m_ref.shape, NEG_BIG, f32)
        l_ref[...] = jnp.zeros(l_ref.shape, f32)
        acc_ref[...] = jnp.zeros(acc_ref.shape, f32)

    def online(s4, vals):
        m_old = m_ref[...].reshape(heads + (1,))
        m_new = jnp.maximum(m_old, jnp.max(s4, axis=-1, keepdims=True))
        alpha = jnp.exp(m_old - m_new)
        ex = jnp.exp(s4 - m_new)
        l_new = alpha * l_ref[...].reshape(heads + (1,)) + jnp.sum(ex, axis=-1, keepdims=True)
        pv = jnp.dot(ex.reshape(n_rows, s4.shape[-1]).astype(bf16), vals, preferred_element_type=f32)
        acc_ref[...] = alpha.reshape(n_rows, 1) * acc_ref[...] + pv
        l_ref[...] = l_new.reshape(n_rows, 1)
        m_ref[...] = m_new.reshape(n_rows, 1)

    @pl.when(ph == 1)
    def _():
        key0 = pl.multiple_of(p * keys_per_step, keys_per_step)
        kv = stash_ref[pl.ds(key0, keys_per_step), :]
        sc = _dot_t(qpad, kv[:, :KV_LANES])
        vals = kv[:, KV_LANES:]
        kblk =(p * keys_per_step + lax.broadcasted_iota(jnp.int32, (n_blk, keys_per_step), 1)) // SLC_BLOCK
        expand = jnp.where(kblk == lax.broadcasted_iota(jnp.int32, (n_blk, keys_per_step), 0), 1.0, 0.0).astype(bf16)
        picked = jnp.dot(sel_ref[...].astype(bf16), expand, preferred_element_type=f32) > 0.5
        bias = jnp.where(picked, 0.0, NEG_BIG).reshape(NSA_KV, 1, n_q, keys_per_step)
        online(sc.reshape(heads + (keys_per_step,)) + bias, vals)

    @pl.when((ph == 1) & (p == last))
    def _():
        tq = lax.broadcasted_iota(jnp.int32, (n_q, NEW_PAD), 0)
        tk = lax.broadcasted_iota(jnp.int32, (n_q, NEW_PAD), 1)
        new_bias = jnp.where(tk <= tq, 0.0, NEG_BIG)[None, None]
        s_new = _dot_t(qpad, ksn_ref[...]).reshape(heads + (NEW_PAD,)) + new_bias
        online(s_new, vsn_ref[...])
        o_slc = acc_ref[...] / l_ref[...]

        win = win_ref[...]
        n_prev = win.shape[0]
        s_prev = _dot_t(qpad, win[:, :KV_LANES].astype(bf16))
        pk = lax.broadcasted_iota(jnp.int32, (n_q, n_prev), 1)
        pq = lax.broadcasted_iota(jnp.int32, (n_q, n_prev), 0)
        s_prev = s_prev.reshape(heads + (n_prev,)) + jnp.where(pk > pq + (n_prev - WINDOW), 0.0, NEG_BIG)[None, None]
        s_wnew = _dot_t(qpad, kwn_ref[...]).reshape(heads + (NEW_PAD,)) + new_bias
        mw = jnp.maximum(jnp.max(s_prev, axis=-1, keepdims=True), jnp.max(s_wnew, axis=-1, keepdims=True))
        e_prev, e_new = jnp.exp(s_prev - mw), jnp.exp(s_wnew - mw)
        lw = jnp.sum(e_prev, axis=-1, keepdims=True) + jnp.sum(e_new, axis=-1, keepdims=True)
        o_win = (jnp.dot(e_prev.reshape(n_rows, n_prev).astype(bf16), win[:, KV_LANES:].astype(bf16),
                         preferred_element_type=f32)
                 + jnp.dot(e_new.reshape(n_rows, NEW_PAD).astype(bf16), vwn_ref[...], preferred_element_type=f32))
        o_win = o_win / lw.reshape(n_rows, 1)

        gate = jax.nn.sigmoid(gate_ref[...])
        low = lax.broadcasted_iota(jnp.int32, (n_q, LANE), 1) < NSA_HD
        o_cmp = ocmp_ref[...]
        n_hd = NSA_KV * NSA_REP
        for r in range(NSA_REP):
            per_g = []
            for g in range(NSA_KV):
                h = g * NSA_REP + r
                rows = slice(h * n_q, (h + 1) * n_q)
                per_g.append(gate[:, h:h + 1] * o_cmp[rows]
                             + gate[:, n_hd + h:n_hd + h + 1] * o_slc[rows]
                             + gate[:, 2 * n_hd + h:2 * n_hd + h + 1] * o_win[rows])
            o_ref[:, r * KV_LANES:(r + 1) * KV_LANES] = jnp.where(low, per_g[0], per_g[1])


def _nsa_sample(page_table, cache, layer, q, proj3, kvb_new, win_prev):
    B, T, _ = q.shape
    n_pages = page_table.shape[1]
    past_len = n_pages * PAGE_SIZE
    n_prev = win_prev.shape[1]
    n_blk = past_len // SLC_BLOCK
    assert T % 8 == 0 and T < CMP_BLOCK and n_pages % PAGES_PER_STEP == 0 and n_blk % LANE == 0
    assert n_blk <= LANE and n_prev == WINDOW
    n_top = min(N_SEL, n_blk + 1)
    n_rows = NSA_KV * NSA_REP * T
    per_b = lambda b, ph, p, pt: (b, 0, 0)

    n_steps = n_pages // PAGES_PER_STEP

    def page_spec(k):
        def index(b, ph, p, pt):
            step = p * (1 - ph) + (n_steps - 1) * ph
            return (layer, pt[b, step * PAGES_PER_STEP + k], 0, 0)
        return pl.BlockSpec((None, None, PAGE_SIZE, 4 * KV_LANES), index)

    new_col = lambda c: pl.BlockSpec((None, NEW_PAD, KV_LANES), lambda b, ph, p, pt: (b, 0, c))
    gate_col = _SEG_DST["nsa_g"] // LANE
    grid_spec = pltpu.PrefetchScalarGridSpec(
        num_scalar_prefetch=1,
        grid=(B, 2, n_pages // PAGES_PER_STEP),
        in_specs=[page_spec(k) for k in range(PAGES_PER_STEP)] + [
            pl.BlockSpec((None, T, NSA_WIDTH), per_b),
            pl.BlockSpec((None, T, LANE), lambda b, ph, p, pt: (b, 0, gate_col)),
            new_col(0), new_col(1),
            pl.BlockSpec((None, n_prev, 2 * KV_LANES), per_b),
            new_col(2), new_col(3),
        ],
        out_specs=pl.BlockSpec((None, T, NSA_WIDTH), per_b),
        scratch_shapes=[
            pltpu.VMEM((past_len // CMP_BLOCK, KV_LANES), jnp.float32),
            pltpu.VMEM((past_len // CMP_BLOCK, KV_LANES), jnp.float32),
            pltpu.VMEM((NSA_KV * T, n_blk), jnp.float32),
            pltpu.VMEM((n_rows, KV_LANES), jnp.float32),
            pltpu.VMEM((n_rows, 1), jnp.float32),
            pltpu.VMEM((n_rows, 1), jnp.float32),
            pltpu.VMEM((n_rows, KV_LANES), jnp.float32),
            pltpu.VMEM((past_len, 2 * KV_LANES), jnp.bfloat16),
        ],
    )
    return pl.pallas_call(
        functools.partial(_nsa_sample_kernel, n_q=T, past_len=past_len, n_top=n_top),
        out_shape=jax.ShapeDtypeStruct((B, T, NSA_WIDTH), jnp.float32),
        grid_spec=grid_spec,
        compiler_params=pltpu.CompilerParams(dimension_semantics=("parallel", "arbitrary", "arbitrary")),
        name="nsa_sample",
    )(page_table, *([cache] * PAGES_PER_STEP), q, proj3, kvb_new, kvb_new, win_prev, kvb_new, kvb_new)


LRU_TILE = 512
SUBLANES = 8


EXPM1_SERIES_RANGE = 0.35
_EXPM1_COEFFS = tuple(1.0 / math.factorial(n) for n in range(9, 1, -1))


def _one_minus_exp(y):
    acc = jnp.full_like(y, _EXPM1_COEFFS[0])
    for c in _EXPM1_COEFFS[1:]:
        acc = acc * y + c
    series = -(y * (1.0 + y * acc))
    return jnp.where(y > -EXPM1_SERIES_RANGE, series, 1.0 - jnp.exp(y))


def _lru_kernel(x_ref, buf_ref, h0_ref, cw_ref, cb_ref, wa_ref, ba_ref, wx_ref, bx_ref, lsl_ref,
                hs_ref, hlast_ref, cbuf_ref, xe_ref, h_ref, *, tile):
    f32 = jnp.float32
    t = pl.program_id(1)
    tail = CONV_W - 1

    @pl.when(t == 0)
    def _():
        xe_ref[0:SUBLANES, :] = jnp.zeros((SUBLANES, D_RNN), f32)
        xe_ref[SUBLANES - tail:SUBLANES, :] = buf_ref[...]
        h_ref[...] = h0_ref[...]

    xe_ref[SUBLANES:SUBLANES + tile, :] = x_ref[...]
    cw = cw_ref[...]
    u = cb_ref[...]
    for j in range(CONV_W):
        u = u + cw[j:j + 1] * xe_ref[SUBLANES - tail + j:SUBLANES - tail + j + tile, :]

    @pl.when(t == pl.num_programs(1) - 1)
    def _():
        cbuf_ref[...] = xe_ref[SUBLANES + tile - tail:SUBLANES + tile, :]

    xe_ref[0:SUBLANES, :] = xe_ref[tile:tile + SUBLANES, :]

    ub = u.astype(jnp.bfloat16)
    r = jax.nn.sigmoid(jnp.dot(ub, wa_ref[...], preferred_element_type=f32) + ba_ref[...])
    i = jax.nn.sigmoid(jnp.dot(ub, wx_ref[...], preferred_element_type=f32) + bx_ref[...])
    log_a = LRU_C * r * lsl_ref[...]
    a = jnp.exp(log_a)
    b = jnp.sqrt(_one_minus_exp(2.0 * log_a)) * (i * u)

    row = lax.broadcasted_iota(jnp.int32, (SUBLANES, D_RNN), 0)
    h = h_ref[...]
    for k in range(tile // SUBLANES):
        ak = a[k * SUBLANES:(k + 1) * SUBLANES]
        bk = b[k * SUBLANES:(k + 1) * SUBLANES]
        for s in (1, 2, 4):
            a_prev = jnp.where(row >= s, pltpu.roll(ak, s, axis=0), 1.0)
            b_prev = jnp.where(row >= s, pltpu.roll(bk, s, axis=0), 0.0)
            bk = ak * b_prev + bk
            ak = ak * a_prev
        hk = ak * h + bk
        hs_ref[k * SUBLANES:(k + 1) * SUBLANES, :] = hk
        h = hk[SUBLANES - 1:SUBLANES]
    h_ref[...] = h
    hlast_ref[...] = h


def _block_diag(w):
    n, c, d = w.shape
    return jnp.einsum('ncd,nm->ncmd', w, jnp.eye(n, dtype=w.dtype)).reshape(n * c, n * d)


def _lru(proj, B, T, buf, h0, conv_w, conv_b, wa_bd, ba, wx_bd, bx, lam):
    tile = min(LRU_TILE, T)
    assert T % tile == 0 and tile % SUBLANES == 0 and _SEG_DST["lru_x"] == 0
    row = lambda a: a.reshape(1, D_RNN)
    per_b = lambda b, t: (b, 0, 0)
    const = lambda b, t: (0, 0)
    vec = pl.BlockSpec((1, D_RNN), const)
    mat = pl.BlockSpec((D_RNN, D_RNN), const)
    hs, hlast, cbuf = pl.pallas_call(
        functools.partial(_lru_kernel, tile=tile),
        out_shape=(jax.ShapeDtypeStruct((B, T, D_RNN), jnp.float32),
                   jax.ShapeDtypeStruct((B, 1, D_RNN), jnp.float32),
                   jax.ShapeDtypeStruct((B, CONV_W - 1, D_RNN), jnp.float32)),
        grid=(B, T // tile),
        in_specs=[
            pl.BlockSpec((None, tile, D_RNN), lambda b, t: (b, t, 0)),
            pl.BlockSpec((None, CONV_W - 1, D_RNN), per_b),
            pl.BlockSpec((None, 1, D_RNN), per_b),
            pl.BlockSpec((CONV_W, D_RNN), const),
            vec, mat, vec, mat, vec, vec,
        ],
        out_specs=(pl.BlockSpec((None, tile, D_RNN), lambda b, t: (b, t, 0)),
                   pl.BlockSpec((None, 1, D_RNN), per_b),
                   pl.BlockSpec((None, CONV_W - 1, D_RNN), per_b)),
        scratch_shapes=[pltpu.VMEM((tile + SUBLANES, D_RNN), jnp.float32),
                        pltpu.VMEM((1, D_RNN), jnp.float32)],
        compiler_params=pltpu.CompilerParams(dimension_semantics=("parallel", "arbitrary")),
        name="rg_lru",
    )(proj.reshape(B, T, IN_COLS_PAD), buf, h0.reshape(B, 1, D_RNN), conv_w, row(conv_b),
      wa_bd, row(ba), wx_bd, row(bx), row(jax.nn.log_sigmoid(lam)))
    return hs, hlast.reshape(B, D_RNN), cbuf


DN_CHUNKS_PER_STEP = 4


def _split2(a):
    hi = a.astype(jnp.bfloat16)
    return hi, (a - hi.astype(jnp.float32)).astype(jnp.bfloat16)


def _dot_hi(a, b):
    a1, a2 = _split2(a)
    b1, b2 = _split2(b)
    d = lambda x, y: jnp.dot(x, y, preferred_element_type=jnp.float32)
    return d(a1, b1) + (d(a1, b2) + d(a2, b1))


def _unit_lower_solve(n_mats, rhss):
    C, n = rhss[0].shape
    f32 = jnp.float32
    ci = lax.broadcasted_iota(jnp.int32, (C, C), 0)
    ei = lax.broadcasted_iota(jnp.int32, (C, C), 1)
    same = (ci // SUBLANES) == (ei // SUBLANES)
    eye = jnp.where(ci == ei, 1.0, 0.0)
    diags = [jnp.where(same, m, 0.0) for m in n_mats]
    offs = [jnp.where(same, 0.0, m) for m in n_mats]
    d2 = [_dot_hi(d, d) for d in diags]
    d4 = [_dot_hi(d, d) for d in d2]
    part = [_dot_hi(eye - d, eye + s) for d, s in zip(diags, d2)]
    dinv = [_dot_hi(p, eye + s) for p, s in zip(part, d4)]
    solved = [[] for _ in rhss]
    for i in range(C // SUBLANES):
        rows = slice(i * SUBLANES, (i + 1) * SUBLANES)
        ys = [r[rows] for r in rhss]
        if i:
            below = jnp.zeros((C - i * SUBLANES, n), f32)
            ys = [y - _dot_hi(off[rows], jnp.concatenate(done + [below], axis=0))
                  for y, off, done in zip(ys, offs, solved)]
        above = [jnp.zeros((i * SUBLANES, n), f32)] if i else []
        rest = [jnp.zeros((C - (i + 1) * SUBLANES, n), f32)] if (i + 1) * SUBLANES < C else []
        xs = [_dot_hi(inv[rows], jnp.concatenate(above + [y] + rest, axis=0)) for inv, y in zip(dinv, ys)]
        for done, x in zip(solved, xs):
            done.append(x)
    return [jnp.concatenate(done, axis=0) for done in solved]


def _dn_kernel(xq_ref, xk_ref, xv_ref, ab_ref, buf_ref, s0_ref, cw_ref, alog_ref, dtb_ref, onorm_ref,
               o_ref, s_out_ref, cbuf_ref, xe_ref, s_ref, *, n_in):
    f32, bf16 = jnp.float32, jnp.bfloat16
    C = DN_CHUNK
    n = pl.program_id(1)
    last = pl.num_programs(1) - 1
    tail = CONV_W - 1
    width = 3 * DN_WIDTH

    @pl.when(n == 0)
    def _():
        xe_ref[...] = jnp.zeros(xe_ref.shape, f32)
        xe_ref[SUBLANES - tail:SUBLANES, :] = buf_ref[...]
        s_ref[...] = s0_ref[...]

    xe_ref[SUBLANES:SUBLANES + n_in, 0:DN_WIDTH] = xq_ref[...]
    xe_ref[SUBLANES:SUBLANES + n_in, DN_WIDTH:2 * DN_WIDTH] = xk_ref[...]
    xe_ref[SUBLANES:SUBLANES + n_in, 2 * DN_WIDTH:width] = xv_ref[...]
    n_chunks = -(-n_in // C)
    R = n_chunks * C
    cw = cw_ref[...]
    y = jnp.zeros((R, width), f32)
    for j in range(CONV_W):
        y = y + cw[j:j + 1] * xe_ref[SUBLANES - tail + j:SUBLANES - tail + j + R, :]

    @pl.when(n == last)
    def _():
        cbuf_ref[...] = xe_ref[SUBLANES + n_in - tail:SUBLANES + n_in, :]

    xe_ref[0:SUBLANES, :] = xe_ref[n_in:n_in + SUBLANES, :]

    valid = lax.broadcasted_iota(jnp.int32, (R, 1), 0) < n_in
    y = jnp.where(valid, y * jax.nn.sigmoid(y), 0.0)
    ab = ab_ref[...]
    if n_in < R:
        ab = jnp.concatenate([ab, jnp.zeros((R - n_in, LANE), f32)], axis=0)
    g_all = jnp.where(valid, -jnp.exp(alog_ref[...]) * jax.nn.softplus(ab + dtb_ref[...]), 0.0)
    beta_all = jnp.where(valid, jax.nn.sigmoid(ab), 0.0)

    ci = lax.broadcasted_iota(jnp.int32, (C, C), 0)
    ei = lax.broadcasted_iota(jnp.int32, (C, C), 1)
    tri = jnp.where(ei <= ci, 1.0, 0.0).astype(bf16)
    csum = lambda x: jnp.dot(tri, x, preferred_element_type=f32)
    heads = range(DN_HEADS)
    mm = lambda a, b: jnp.dot(a, b, preferred_element_type=f32)
    col = lambda a, h: a[:, h:h + 1]

    q_dec, qks, k_dec, e_last, n_mats, rhss = [], [], [], [], [], []
    for c in range(n_chunks):
        rows = slice(c * C, (c + 1) * C)
        g_c = g_all[rows]
        g1 = g_c.astype(bf16)
        r1 = g_c - g1.astype(f32)
        g2 = r1.astype(bf16)
        g3 = (r1 - g2.astype(f32)).astype(bf16)
        G = csum(g1) + (csum(g2) + csum(g3))
        G_t = G.T
        g_last = G[C - 1:C, :]
        e_g = jnp.exp(G)
        e_rest = jnp.exp(g_last - G)
        e_last.append(jnp.exp(g_last))
        head_lanes = lambda base, h: y[rows, base + h * DN_HD:base + (h + 1) * DN_HD]
        qs = [head_lanes(0, h) for h in heads]
        ks = [head_lanes(DN_WIDTH, h) for h in heads]
        vs = [head_lanes(2 * DN_WIDTH, h) for h in heads]
        qs = [q * lax.rsqrt(jnp.sum(q * q, axis=-1, keepdims=True) + L2_EPS) * (DN_HD ** -0.5) for q in qs]
        ks = [k * lax.rsqrt(jnp.sum(k * k, axis=-1, keepdims=True) + L2_EPS) for k in ks]
        decays = [jnp.where(ei <= ci, jnp.exp(col(G, h) - G_t[h:h + 1, :]), 0.0) for h in heads]
        betas = [col(beta_all[rows], DN_HEADS + h) for h in heads]
        kbs = [k.astype(bf16) for k in ks]
        kks = [_dot_t(kb, kb) for kb in kbs]
        qk = [_dot_t(q.astype(bf16), kb) for q, kb in zip(qs, kbs)]
        n_mats += [jnp.where(ei < ci, b * kk * d, 0.0) for b, kk, d in zip(betas, kks, decays)]
        qks.append([(a * d).astype(bf16) for a, d in zip(qk, decays)])
        rhss += [jnp.concatenate([betas[h] * vs[h], betas[h] * ks[h] * col(e_g, h)], axis=1) for h in heads]
        q_dec.append([(qs[h] * col(e_g, h)).astype(bf16) for h in heads])
        k_dec.append([(ks[h] * col(e_rest, h)).astype(bf16) for h in heads])
    xs = _unit_lower_solve(n_mats, rhss)

    states = [s_ref[h] for h in heads]
    o_gain = onorm_ref[...]
    for c in range(n_chunks):
        xc = xs[c * DN_HEADS:(c + 1) * DN_HEADS]
        sbs = [S.astype(bf16) for S in states]
        v_news = [x[:, :DN_HD] - mm(x[:, DN_HD:].astype(bf16), sb) for x, sb in zip(xc, sbs)]
        vbs = [v.astype(bf16) for v in v_news]
        o_state = [mm(qd, sb) for qd, sb in zip(q_dec[c], sbs)]
        o_local = [mm(qk, vb) for qk, vb in zip(qks[c], vbs)]
        grow = [lax.dot_general(kd, vb, (((0,), (0,)), ((), ())), preferred_element_type=f32)
                for kd, vb in zip(k_dec[c], vbs)]
        states = [col(e_last[c], h) * states[h] + grow[h] for h in heads]
        n_out = min(C, n_in)
        for h in heads:
            o = o_state[h] + o_local[h]
            o = o * lax.rsqrt(jnp.mean(o * o, axis=-1, keepdims=True) + RMS_EPS) * o_gain
            o_ref[c * C:c * C + n_out, h * DN_HD:(h + 1) * DN_HD] = o[:n_out]
    for h in heads:
        s_ref[h] = states[h]

    @pl.when(n == last)
    def _():
        s_out_ref[...] = s_ref[...]


def _gated_delta_net(proj, B, T, buf, S0, conv_w, A_log, dt_bias, o_norm):
    n_in = min(DN_CHUNKS_PER_STEP * DN_CHUNK, T)
    assert T % n_in == 0 and n_in % SUBLANES == 0 and (n_in <= DN_CHUNK or n_in % DN_CHUNK == 0)
    rows_pad = -(-n_in // DN_CHUNK) * DN_CHUNK
    qkv0 = _SEG_DST["dn_qkv"] // DN_WIDTH
    assert _SEG_DST["dn_qkv"] % DN_WIDTH == 0 and _SEG_DST["dn_b"] == _SEG_DST["dn_a"] + DN_HEADS
    ab_blk = _SEG_DST["dn_a"] // LANE
    proj3 = proj.reshape(B, T, IN_COLS_PAD)
    col = lambda c: (lambda b, n: (b, n, c))
    per_b = lambda b, n: (b, 0, 0)
    const = lambda b, n: (0, 0)
    pad_row = lambda a, off: jnp.zeros((1, LANE), jnp.float32).at[0, off:off + DN_HEADS].set(a)
    o, s_out, cbuf = pl.pallas_call(
        functools.partial(_dn_kernel, n_in=n_in),
        out_shape=(jax.ShapeDtypeStruct((B, T, DN_WIDTH), jnp.float32),
                   jax.ShapeDtypeStruct((B, DN_HEADS, DN_HD, DN_HD), jnp.float32),
                   jax.ShapeDtypeStruct((B, CONV_W - 1, 3 * DN_WIDTH), jnp.float32)),
        grid=(B, T // n_in),
        in_specs=[
            pl.BlockSpec((None, n_in, DN_WIDTH), col(qkv0)),
            pl.BlockSpec((None, n_in, DN_WIDTH), col(qkv0 + 1)),
            pl.BlockSpec((None, n_in, DN_WIDTH), col(qkv0 + 2)),
            pl.BlockSpec((None, n_in, LANE), col(ab_blk)),
            pl.BlockSpec((None, CONV_W - 1, 3 * DN_WIDTH), per_b),
            pl.BlockSpec((None, DN_HEADS, DN_HD, DN_HD), lambda b, n: (b, 0, 0, 0)),
            pl.BlockSpec((CONV_W, 3 * DN_WIDTH), const),
            pl.BlockSpec((1, LANE), const),
            pl.BlockSpec((1, LANE), const),
            pl.BlockSpec((1, DN_HD), const),
        ],
        out_specs=(pl.BlockSpec((None, n_in, DN_WIDTH), lambda b, n: (b, n, 0)),
                   pl.BlockSpec((None, DN_HEADS, DN_HD, DN_HD), lambda b, n: (b, 0, 0, 0)),
                   pl.BlockSpec((None, CONV_W - 1, 3 * DN_WIDTH), per_b)),
        scratch_shapes=[pltpu.VMEM((rows_pad + 2 * SUBLANES, 3 * DN_WIDTH), jnp.float32),
                        pltpu.VMEM((DN_HEADS, DN_HD, DN_HD), jnp.float32)],
        compiler_params=pltpu.CompilerParams(dimension_semantics=("parallel", "arbitrary")),
        name="gated_delta",
    )(proj3, proj3, proj3, proj3, buf, S0, conv_w, pad_row(A_log, 0), pad_row(dt_bias, 0),
      o_norm.reshape(1, DN_HD))
    return o, s_out, cbuf


def _layer(x, paged, lru_h0, lru_buf, dn_S0, dn_buf,
           norm_g, w_in_pad, lru_conv_w, lru_conv_b, lru_wa, lru_ba, lru_wx, lru_bx, lru_lam,
           q_norm, k_norm, dn_conv_w, dn_A_log, dn_dt_bias, dn_o_norm,
           w_lru_out, w_nsa_out, w_dn_out, w_out):
    B, T, _ = x.shape
    P = 0 if paged is None else paged[0].shape[1] * PAGE_SIZE
    pos = P + jnp.arange(T, dtype=jnp.int32)
    x2d = x.reshape(B * T, D_MODEL)
    proj = _in_proj(x2d, norm_g, w_in_pad)
    proj3 = proj.reshape(B, T, IN_COLS_PAD)

    lru_seq, lru_h, lru_buf_new = _lru(proj, B, T, lru_buf, lru_h0, lru_conv_w, lru_conv_b,
                                       lru_wa, lru_ba, lru_wx, lru_bx, lru_lam)

    q, rows, win_rows, kvb = _nsa_prep(proj3, pos, q_norm, k_norm)
    if paged is None:
        o_nsa = _nsa_prompt(q, proj3, rows, kvb)
        win_state = win_rows[:, -min(WINDOW, T):]
    else:
        page_table, cache, layer, win_prev = paged
        win_prev = win_prev.reshape(B, win_prev.shape[1], 2 * KV_LANES)
        kvb_new = jnp.pad(kvb, ((0, 0), (0, NEW_PAD - T), (0, 0)))
        o_nsa = _nsa_sample(page_table, cache, layer, q, proj3, kvb_new, win_prev)
        win_state = jnp.concatenate([win_prev, win_rows], axis=1)[:, -WINDOW:]
    rows = rows.reshape(B, T, 4, NSA_KV, NSA_HD)
    win_state = win_state.reshape(B, win_state.shape[1], 2, NSA_KV, NSA_HD)

    o_dn, dn_S, dn_buf_new = _gated_delta_net(proj, B, T, dn_buf, dn_S0, dn_conv_w, dn_A_log, dn_dt_bias,
                                              dn_o_norm)

    y2d = _out_stage(x2d, proj, lru_seq.reshape(B * T, D_RNN), o_nsa.reshape(B * T, NSA_WIDTH),
                     o_dn.reshape(B * T, DN_WIDTH), w_lru_out, w_nsa_out, w_dn_out, w_out)
    return y2d.reshape(B, T, D_MODEL), (rows, win_state, lru_h, lru_buf_new, dn_S, dn_buf_new)


def kernel(x_prompt, x_sample, cache_nsa_kv, page_table, state_nsa_win, state_lru_h, state_lru_conv,
           state_dn_S, state_dn_conv, norm_gain, w_in, lru_conv_w, lru_conv_b, lru_wa, lru_ba,
           lru_wx, lru_bx, lru_lambda, nsa_q_norm, nsa_k_norm, dn_conv_w, dn_A_log, dn_dt_bias,
           dn_o_norm, w_lru_out, w_nsa_out, w_dn_out, w_out):
    Bp = x_prompt.shape[0]
    dt = x_prompt.dtype
    bf16 = jnp.bfloat16
    cache = cache_nsa_kv.reshape(cache_nsa_kv.shape[:2] + (PAGE_SIZE, 4 * KV_LANES))
    p_h0 = jnp.zeros((Bp, D_RNN), jnp.float32)
    p_lbuf0 = jnp.zeros((Bp, CONV_W - 1, D_RNN), dt)
    p_S0 = jnp.zeros((Bp, DN_HEADS, DN_HD, DN_HD), jnp.float32)
    p_dbuf0 = jnp.zeros((Bp, CONV_W - 1, 3 * DN_WIDTH), dt)
    wa_bd = jax.vmap(_block_diag)(lru_wa).astype(bf16)
    wx_bd = jax.vmap(_block_diag)(lru_wx).astype(bf16)
    weights = (norm_gain, _pad_in_weight(w_in), lru_conv_w, lru_conv_b, wa_bd, lru_ba, wx_bd, lru_bx,
               lru_lambda, nsa_q_norm, nsa_k_norm, dn_conv_w, dn_A_log, dn_dt_bias, dn_o_norm,
               w_lru_out.astype(bf16), _heads_rg(w_nsa_out.swapaxes(1, 2)).swapaxes(1, 2).astype(bf16),
               w_dn_out.astype(bf16), w_out.astype(bf16))
    xp, xs = x_prompt, x_sample
    st_p, st_s = [], []
    for l in range(DEPTH):
        lw = [w[l] for w in weights]
        xp, sp = _layer(xp, None, p_h0, p_lbuf0, p_S0, p_dbuf0, *lw)
        xs, ss = _layer(xs, (page_table, cache, l, state_nsa_win[l]), state_lru_h[l], state_lru_conv[l],
                        state_dn_S[l], state_dn_conv[l], *lw)
        st_p.append(sp)
        st_s.append(ss)
    kv_p, win_p, lh_p, lc_p, S_p, dc_p = [jnp.stack(a) for a in zip(*st_p)]
    kv_s, win_s, lh_s, lc_s, S_s, dc_s = [jnp.stack(a) for a in zip(*st_s)]
    return (xp, xs, kv_p, kv_s, win_p, win_s, lh_p, lh_s, lc_p, lc_s, S_p, S_s, dc_p, dc_s)
```

```python
import functools
import math

import numpy as np
import jax
import jax.numpy as jnp
from jax import lax
from jax.experimental import pallas as pl
from jax.experimental.pallas import tpu as pltpu

D_MODEL = 1024
DEPTH = 4
PAGE_SIZE = 128
CONV_W = 4
RMS_EPS = 1e-6
L2_EPS = 1e-6
N_BRANCH = 3
D_RNN = D_MODEL // 2
LRU_BLOCKS = 8
LRU_BS = D_RNN // LRU_BLOCKS
LRU_C = 8.0
NSA_HEADS = 8
NSA_HD = 64
NSA_KV = 2
NSA_REP = NSA_HEADS // NSA_KV
NSA_WIDTH = NSA_HEADS * NSA_HD
CMP_BLOCK = 32
SLC_BLOCK = 64
N_SEL = 16
WINDOW = 512
Q_BLOCK = 128
ROT_DIM = NSA_HD // 4
ROPE_THETA = 500000.0
DN_HEADS = 4
DN_HD = 128
DN_WIDTH = DN_HEADS * DN_HD
DN_CHUNK = 64

LANE = 128

_SEG_NAMES = ("lru_x", "lru_z", "nsa_q", "nsa_kv", "nsa_z", "nsa_g", "dn_qkv", "dn_z", "dn_a", "dn_b", "merge_g")
_SEG_SIZES = (D_RNN, D_RNN, NSA_WIDTH, 2 * N_BRANCH * NSA_KV * NSA_HD, NSA_WIDTH, N_BRANCH * NSA_HEADS,
              3 * DN_WIDTH, DN_WIDTH, DN_HEADS, DN_HEADS, N_BRANCH * D_MODEL)
_SEG_SRC = dict(zip(_SEG_NAMES, np.concatenate([[0], np.cumsum(_SEG_SIZES)[:-1]]).tolist()))
_SEG_LEN = dict(zip(_SEG_NAMES, _SEG_SIZES))
_DST_SLOTS = (
    (("lru_x",), D_RNN), (("lru_z",), D_RNN), (("nsa_q",), NSA_WIDTH), (("nsa_z",), NSA_WIDTH),
    (("dn_z",), DN_WIDTH), (("nsa_g",), LANE), (("dn_a", "dn_b"), LANE),
    (("nsa_kv",), 2 * N_BRANCH * NSA_KV * NSA_HD), (("dn_qkv",), 3 * DN_WIDTH), (("merge_g",), N_BRANCH * D_MODEL),
)


def _dst_layout():
    off, dst = 0, {}
    for names, width in _DST_SLOTS:
        o = off
        for n in names:
            dst[n] = o
            o += _SEG_LEN[n]
        off += width
    return dst, off


_SEG_DST, IN_COLS_PAD = _dst_layout()


def _heads_rg(a):
    lead = a.shape[:-1]
    return a.reshape(lead + (NSA_KV, NSA_REP, NSA_HD)).swapaxes(-3, -2).reshape(lead + (NSA_WIDTH,))


def _pad_in_weight(w_in):
    pieces = []
    for names, width in _DST_SLOTS:
        used = 0
        for n in names:
            piece = w_in[:, :, _SEG_SRC[n]:_SEG_SRC[n] + _SEG_LEN[n]]
            pieces.append(_heads_rg(piece) if n in ("nsa_q", "nsa_z") else piece)
            used += _SEG_LEN[n]
        if used < width:
            pieces.append(jnp.zeros(w_in.shape[:2] + (width - used,), w_in.dtype))
    return jnp.concatenate(pieces, axis=-1).astype(jnp.bfloat16)


IN_PROJ_ROWS = 2048
IN_PROJ_COLS = 512


def _in_proj_kernel(x_ref, g_ref, w_ref, o_ref, h_ref):
    @pl.when(pl.program_id(1) == 0)
    def _():
        x = x_ref[...]
        ms = jnp.mean(x * x, axis=-1, keepdims=True)
        h_ref[...] = (x * lax.rsqrt(ms + RMS_EPS) * g_ref[...]).astype(jnp.bfloat16)

    o_ref[...] = jnp.dot(h_ref[...], w_ref[...], preferred_element_type=jnp.float32)


def _in_proj(x2d, gain, w_pad):
    m = x2d.shape[0]
    tm = min(IN_PROJ_ROWS, m)
    tn = IN_PROJ_COLS
    vmem = 2 * (tm * D_MODEL * 4 + D_MODEL * tn * 2 + tm * tn * 4) + tm * D_MODEL * 2 + (4 << 20)
    return pl.pallas_call(
        _in_proj_kernel,
        out_shape=jax.ShapeDtypeStruct((m, IN_COLS_PAD), jnp.float32),
        grid=(m // tm, IN_COLS_PAD // tn),
        in_specs=[
            pl.BlockSpec((tm, D_MODEL), lambda i, j: (i, 0)),
            pl.BlockSpec((1, D_MODEL), lambda i, j: (0, 0)),
            pl.BlockSpec((D_MODEL, tn), lambda i, j: (0, j)),
        ],
        out_specs=pl.BlockSpec((tm, tn), lambda i, j: (i, j)),
        scratch_shapes=[pltpu.VMEM((tm, D_MODEL), jnp.bfloat16)],
        compiler_params=pltpu.CompilerParams(dimension_semantics=("parallel", "arbitrary"),
                                             vmem_limit_bytes=vmem),
        name="in_proj",
    )(x2d, gain.reshape(1, D_MODEL), w_pad)


def _silu(z):
    return z * jax.nn.sigmoid(z)


def _out_stage_kernel(x_ref, a_lru, z_lru, a_nsa, z_nsa, a_dn, z_dn, mg0, mg1, mg2,
                      w_lru, w_nsa, w_dn, w_out, o_ref):
    def branch(a, z, w):
        y = (a[...] * _silu(z[...])).astype(jnp.bfloat16)
        return jnp.dot(y, w[...], preferred_element_type=jnp.float32)

    merged = jax.nn.sigmoid(mg0[...]) * branch(a_lru, z_lru, w_lru)
    merged = merged + jax.nn.sigmoid(mg1[...]) * branch(a_nsa, z_nsa, w_nsa)
    merged = merged + jax.nn.sigmoid(mg2[...]) * branch(a_dn, z_dn, w_dn)
    y = jnp.dot(merged.astype(jnp.bfloat16), w_out[...], preferred_element_type=jnp.float32)
    o_ref[...] = x_ref[...] + y


def _out_stage(x2d, proj, a_lru, a_nsa, a_dn, w_lru, w_nsa, w_dn, w_out):
    m = x2d.shape[0]
    tm = min(256, m)
    half = D_RNN
    row = lambda i: (i, 0)
    col = lambda c: (lambda i: (i, c))
    full = lambda i: (0, 0)
    mg_blk = _SEG_DST["merge_g"] // D_MODEL
    return pl.pallas_call(
        _out_stage_kernel,
        out_shape=jax.ShapeDtypeStruct((m, D_MODEL), jnp.float32),
        grid=(m // tm,),
        in_specs=[
            pl.BlockSpec((tm, D_MODEL), row),
            pl.BlockSpec((tm, half), row),
            pl.BlockSpec((tm, half), col(_SEG_DST["lru_z"] // half)),
            pl.BlockSpec((tm, half), row),
            pl.BlockSpec((tm, half), col(_SEG_DST["nsa_z"] // half)),
            pl.BlockSpec((tm, half), row),
            pl.BlockSpec((tm, half), col(_SEG_DST["dn_z"] // half)),
            pl.BlockSpec((tm, D_MODEL), col(mg_blk)),
            pl.BlockSpec((tm, D_MODEL), col(mg_blk + 1)),
            pl.BlockSpec((tm, D_MODEL), col(mg_blk + 2)),
            pl.BlockSpec((half, D_MODEL), full),
            pl.BlockSpec((half, D_MODEL), full),
            pl.BlockSpec((half, D_MODEL), full),
            pl.BlockSpec((D_MODEL, D_MODEL), full),
        ],
        out_specs=pl.BlockSpec((tm, D_MODEL), row),
        compiler_params=pltpu.CompilerParams(dimension_semantics=("parallel",)),
        name="out_stage",
    )(x2d, a_lru, proj, a_nsa, proj, a_dn, proj, proj, proj, proj, w_lru, w_nsa, w_dn, w_out)


PREP_TILE = 512
HALF_ROT = ROT_DIM // 2


def _rope_tables(pos):
    inv = ROPE_THETA ** (-jnp.arange(HALF_ROT, dtype=jnp.float32) * 2.0 / ROT_DIM)
    ang = pos.astype(jnp.float32)[:, None] * inv[None, :]
    c, s = jnp.cos(ang), jnp.sin(ang)
    n = pos.shape[0]
    rest = NSA_HD - ROT_DIM
    cos_h = jnp.concatenate([c, c, jnp.ones((n, rest), jnp.float32)], axis=1)
    sin_h = jnp.concatenate([-s, s, jnp.zeros((n, rest), jnp.float32)], axis=1)
    return jnp.tile(cos_h, (1, NSA_KV)), jnp.tile(sin_h, (1, NSA_KV))


def _nsa_prep_kernel(q_ref, kv0_ref, kv1_ref, kv2_ref, cos_ref, sin_ref, qn_ref, kn_ref, ones_ref,
                     q_out, rows_out, win_out, kvb_out):
    f32, bf16 = jnp.float32, jnp.bfloat16
    cos, sin = cos_ref[...], sin_ref[...]
    ones = ones_ref[...]
    n = cos.shape[0]
    first = (lax.broadcasted_iota(jnp.int32, (n, KV_LANES), 1) % NSA_HD) < HALF_ROT

    def norm_rope(x, gain):
        sq = x * x
        hi = sq.astype(bf16)
        lo = (sq - hi.astype(f32)).astype(bf16)
        ssq = (jnp.dot(hi, ones, preferred_element_type=f32) + jnp.dot(lo, ones, preferred_element_type=f32))
        y = x * lax.rsqrt(ssq * (1.0 / NSA_HD) + RMS_EPS) * gain
        partner = jnp.where(first, pltpu.roll(y, KV_LANES - HALF_ROT, axis=1), pltpu.roll(y, HALF_ROT, axis=1))
        return y * cos + partner * sin

    qn = qn_ref[...]
    for r in range(NSA_REP):
        lanes = slice(r * KV_LANES, (r + 1) * KV_LANES)
        q_out[:, lanes] = norm_rope(q_ref[:, lanes], qn) * (NSA_HD ** -0.5)
    kn = kn_ref[...]
    k_cmp = norm_rope(kv0_ref[:, :KV_LANES], kn[0:1])
    k_slc = norm_rope(kv1_ref[:, :KV_LANES], kn[1:2])
    k_win = norm_rope(kv2_ref[:, :KV_LANES], kn[2:3])
    v_slc = kv1_ref[:, KV_LANES:]
    v_win = kv2_ref[:, KV_LANES:]
    rows_out[:, 0:KV_LANES] = k_cmp
    rows_out[:, KV_LANES:2 * KV_LANES] = kv0_ref[:, KV_LANES:]
    rows_out[:, 2 * KV_LANES:3 * KV_LANES] = k_slc
    rows_out[:, 3 * KV_LANES:4 * KV_LANES] = v_slc
    win_out[:, 0:KV_LANES] = k_win
    win_out[:, KV_LANES:2 * KV_LANES] = v_win
    kvb_out[:, 0:KV_LANES] = k_slc.astype(bf16)
    kvb_out[:, KV_LANES:2 * KV_LANES] = v_slc.astype(bf16)
    kvb_out[:, 2 * KV_LANES:3 * KV_LANES] = k_win.astype(bf16)
    kvb_out[:, 3 * KV_LANES:4 * KV_LANES] = v_win.astype(bf16)


def _nsa_prep(proj3, pos, q_norm, k_norm):
    B, T, _ = proj3.shape
    tile = min(PREP_TILE, T)
    assert T % tile == 0
    pair = 2 * KV_LANES
    q_blk = _SEG_DST["nsa_q"] // NSA_WIDTH
    kv_blk = _SEG_DST["nsa_kv"] // pair
    assert _SEG_DST["nsa_q"] % NSA_WIDTH == 0 and _SEG_DST["nsa_kv"] % pair == 0
    cos_t, sin_t = _rope_tables(pos)
    lane_head = jnp.arange(KV_LANES) // NSA_HD
    ones = (lane_head[:, None] == lane_head[None, :]).astype(jnp.bfloat16)
    col = lambda c: (lambda b, t: (b, t, c))
    tab = pl.BlockSpec((tile, KV_LANES), lambda b, t: (t, 0))
    const = lambda b, t: (0, 0)
    out = lambda w: pl.BlockSpec((None, tile, w), lambda b, t: (b, t, 0))
    return pl.pallas_call(
        _nsa_prep_kernel,
        out_shape=(jax.ShapeDtypeStruct((B, T, NSA_WIDTH), jnp.float32),
                   jax.ShapeDtypeStruct((B, T, 4 * KV_LANES), jnp.float32),
                   jax.ShapeDtypeStruct((B, T, 2 * KV_LANES), jnp.float32),
                   jax.ShapeDtypeStruct((B, T, 4 * KV_LANES), jnp.bfloat16)),
        grid=(B, T // tile),
        in_specs=[
            pl.BlockSpec((None, tile, NSA_WIDTH), col(q_blk)),
            pl.BlockSpec((None, tile, pair), col(kv_blk)),
            pl.BlockSpec((None, tile, pair), col(kv_blk + 1)),
            pl.BlockSpec((None, tile, pair), col(kv_blk + 2)),
            tab, tab,
            pl.BlockSpec((1, KV_LANES), const),
            pl.BlockSpec((N_BRANCH, KV_LANES), const),
            pl.BlockSpec((KV_LANES, KV_LANES), const),
        ],
        out_specs=(out(NSA_WIDTH), out(4 * KV_LANES), out(2 * KV_LANES), out(4 * KV_LANES)),
        compiler_params=pltpu.CompilerParams(dimension_semantics=("parallel", "parallel")),
        name="nsa_prep",
    )(proj3, proj3, proj3, proj3, cos_t, sin_t, jnp.tile(q_norm, NSA_KV).reshape(1, KV_LANES),
      jnp.tile(k_norm, (1, NSA_KV)), ones)


NEG_BIG = -1e30
SLC_CHUNK = 512
N_ROWS = NSA_KV * NSA_REP * Q_BLOCK
KV_LANES = NSA_KV * NSA_HD
WIN_KEYS = WINDOW + Q_BLOCK
MAX_SLC_BLOCKS = LANE // 2


def _dot_t(a, b):
    return lax.dot_general(a, b, (((1,), (1,)), ((), ())), preferred_element_type=jnp.float32)


def _select_blocks(imp_t, cur, n_top):
    nb = imp_t.shape[0]
    jj = lax.broadcasted_iota(jnp.int32, imp_t.shape, 0)
    v = jnp.where((jj == 0) | (jj == cur), jnp.inf, jnp.where(jj > cur, -jnp.inf, imp_t))
    sub = lax.broadcasted_iota(jnp.int32, (8, imp_t.shape[1]), 0)
    ranks = []
    for a in range(nb // 8):
        va = v[8 * a:8 * a + 8]
        rank = jnp.zeros(va.shape, jnp.float32)
        for j in range(nb):
            row = v[j:j + 1]
            ge = jnp.where(row >= va, 1.0, 0.0)
            gt = jnp.where(row > va, 1.0, 0.0)
            if j < 8 * a:
                rank = rank + ge
            elif j >= 8 * a + 8:
                rank = rank + gt
            else:
                rank = rank + jnp.where(sub > (j - 8 * a), ge, gt)
        ranks.append(rank)
    rank = jnp.concatenate(ranks, axis=0)
    return jnp.where((rank < n_top) & (jj <= cur), 1.0, 0.0)


def _nsa_prompt_kernel(q_ref, gate_ref, kc_ref, vc_ref, ks_ref, vs_ref, kw_ref, vw_ref, o_ref,
                       kcb_ref, vcb_ref, m_ref, l_ref, acc_ref, *, seq_len, n_top):
    f32, bf16 = jnp.float32, jnp.bfloat16
    i = pl.program_id(1)
    nsb = seq_len // SLC_BLOCK
    half = MAX_SLC_BLOCKS

    @pl.when(i == 0)
    def _():
        if nsb < half:
            kcb_ref[...] = jnp.zeros(kcb_ref.shape, f32)
            vcb_ref[...] = jnp.zeros(vcb_ref.shape, f32)
        for src, dst in ((kc_ref, kcb_ref), (vc_ref, vcb_ref)):
            x = src[...].reshape(nsb, SLC_BLOCK, KV_LANES)
            dst[0:nsb, :] = jnp.sum(x[:, :CMP_BLOCK, :], axis=1) * (1.0 / CMP_BLOCK)
            dst[half:half + nsb, :] = jnp.sum(x[:, CMP_BLOCK:, :], axis=1) * (1.0 / CMP_BLOCK)

    lane = lax.broadcasted_iota(jnp.int32, (Q_BLOCK, LANE), 1)
    tq = lax.broadcasted_iota(jnp.int32, (Q_BLOCK, LANE), 0)
    pos = i * Q_BLOCK + tq
    low = lane < NSA_HD

    q = q_ref[...]
    parts = []
    for g in range(NSA_KV):
        for r in range(NSA_REP):
            qr = q[:, r * KV_LANES:(r + 1) * KV_LANES]
            parts.append(jnp.where(low if g == 0 else ~low, qr, 0.0))
    qpad = jnp.concatenate(parts, axis=0).astype(bf16)

    s = _dot_t(qpad, kcb_ref[...].astype(bf16))
    cblk = jnp.where(lane < half, 2 * lane, 2 * lane - (2 * half - 1))
    okc = (cblk * CMP_BLOCK + (CMP_BLOCK - 1)) <= pos
    s3 = s.reshape(NSA_KV * NSA_REP, Q_BLOCK, LANE) + jnp.where(okc, 0.0, NEG_BIG)[None]
    mx = jnp.max(s3, axis=-1, keepdims=True)
    e = jnp.where(okc[None], jnp.exp(s3 - mx), 0.0)
    den = jnp.sum(e, axis=-1, keepdims=True)
    p3 = e / jnp.where(den > 0.0, den, 1.0)
    o_cmp = jnp.dot(p3.reshape(N_ROWS, LANE).astype(bf16), vcb_ref[...].astype(bf16),
                    preferred_element_type=f32)

    pg = p3.reshape(NSA_KV, NSA_REP, Q_BLOCK, LANE).sum(axis=1)
    cur_t = (i * Q_BLOCK + lax.broadcasted_iota(jnp.int32, (half, Q_BLOCK), 1)) // SLC_BLOCK
    selq = []
    for g in range(NSA_KV):
        imp = pg[g] + pltpu.roll(pg[g], half, axis=1)
        sel_t = _select_blocks(imp.T[:half], cur_t, n_top)
        sel_full = jnp.concatenate([sel_t, jnp.zeros_like(sel_t)], axis=0)
        selq.append(sel_full.T[:, :half].astype(bf16))

    m_ref[...] = jnp.full(m_ref.shape, NEG_BIG, f32)
    l_ref[...] = jnp.zeros(l_ref.shape, f32)
    acc_ref[...] = jnp.zeros(acc_ref.shape, f32)
    heads = (NSA_KV, NSA_REP, Q_BLOCK)

    def slc_step(c, carry):
        start = pl.multiple_of(c * SLC_CHUNK, SLC_CHUNK)
        kch = ks_ref[pl.ds(start, SLC_CHUNK), :]
        vch = vs_ref[pl.ds(start, SLC_CHUNK), :]
        sc = _dot_t(qpad, kch)
        kidx = start + lax.broadcasted_iota(jnp.int32, (Q_BLOCK, SLC_CHUNK), 1)
        causal = kidx <= i * Q_BLOCK + lax.broadcasted_iota(jnp.int32, (Q_BLOCK, SLC_CHUNK), 0)
        kblk = (start + lax.broadcasted_iota(jnp.int32, (half, SLC_CHUNK), 1)) // SLC_BLOCK
        expand = jnp.where(kblk == lax.broadcasted_iota(jnp.int32, (half, SLC_CHUNK), 0), 1.0, 0.0).astype(bf16)
        bias = []
        for g in range(NSA_KV):
            picked = jnp.dot(selq[g], expand, preferred_element_type=f32) > 0.5
            bias.append(jnp.where(picked & causal, 0.0, NEG_BIG))
        bias = jnp.stack(bias, axis=0)[:, None]
        s4 = sc.reshape(heads + (SLC_CHUNK,)) + bias
        m_old = m_ref[...].reshape(heads + (1,))
        m_new = jnp.maximum(m_old, jnp.max(s4, axis=-1, keepdims=True))
        alpha = jnp.exp(m_old - m_new)
        ex = jnp.exp(s4 - m_new)
        l_new = alpha * l_ref[...].reshape(heads + (1,)) + jnp.sum(ex, axis=-1, keepdims=True)
        pv = jnp.dot(ex.reshape(N_ROWS, SLC_CHUNK).astype(bf16), vch, preferred_element_type=f32)
        acc_ref[...] = alpha.reshape(N_ROWS, 1) * acc_ref[...] + pv
        l_ref[...] = l_new.reshape(N_ROWS, 1)
        m_ref[...] = m_new.reshape(N_ROWS, 1)
        return carry

    n_chunks = (i * Q_BLOCK + Q_BLOCK + SLC_CHUNK - 1) // SLC_CHUNK
    lax.fori_loop(0, n_chunks, slc_step, 0)
    o_slc = acc_ref[...] / l_ref[...]

    wstart = pl.multiple_of(jnp.maximum(i - WINDOW // Q_BLOCK, 0) * Q_BLOCK, Q_BLOCK)
    kwin = kw_ref[pl.ds(wstart, WIN_KEYS), :]
    vwin = vw_ref[pl.ds(wstart, WIN_KEYS), :]
    sw = _dot_t(qpad, kwin)
    widx = wstart + lax.broadcasted_iota(jnp.int32, (Q_BLOCK, WIN_KEYS), 1)
    wpos = i * Q_BLOCK + lax.broadcasted_iota(jnp.int32, (Q_BLOCK, WIN_KEYS), 0)
    okw = (widx <= wpos) & (wpos - widx < WINDOW)
    sw3 = sw.reshape(NSA_KV * NSA_REP, Q_BLOCK, WIN_KEYS) + jnp.where(okw, 0.0, NEG_BIG)[None]
    ew = jnp.exp(sw3 - jnp.max(sw3, axis=-1, keepdims=True))
    lw = jnp.sum(ew, axis=-1, keepdims=True).reshape(N_ROWS, 1)
    o_win = jnp.dot(ew.reshape(N_ROWS, WIN_KEYS).astype(bf16), vwin, preferred_element_type=f32) / lw

    gate = jax.nn.sigmoid(gate_ref[...])
    n_hd = NSA_KV * NSA_REP
    for r in range(NSA_REP):
        per_g = []
        for g in range(NSA_KV):
            h = g * NSA_REP + r
            rows = slice(h * Q_BLOCK, (h + 1) * Q_BLOCK)
            per_g.append(gate[:, h:h + 1] * o_cmp[rows]
                         + gate[:, n_hd + h:n_hd + h + 1] * o_slc[rows]
                         + gate[:, 2 * n_hd + h:2 * n_hd + h + 1] * o_win[rows])
        o_ref[:, r * KV_LANES:(r + 1) * KV_LANES] = jnp.where(low, per_g[0], per_g[1])


def _gate_block(rows):
    c = _SEG_DST["nsa_g"] // LANE
    assert _SEG_DST["nsa_g"] % LANE == 0
    return pl.BlockSpec((None, rows, LANE), lambda b, i, *_: (b, i, c))


def _nsa_prompt(q, proj3, rows, kvb):
    B, T, _ = q.shape
    assert T % SLC_CHUNK == 0 and T >= WIN_KEYS and T // SLC_BLOCK <= MAX_SLC_BLOCKS
    n_top = min(N_SEL, T // SLC_BLOCK)
    blk = lambda b, i: (b, i, 0)
    kv_col = lambda c: pl.BlockSpec((None, T, KV_LANES), lambda b, i: (b, 0, c))
    return pl.pallas_call(
        functools.partial(_nsa_prompt_kernel, seq_len=T, n_top=n_top),
        out_shape=jax.ShapeDtypeStruct((B, T, NSA_WIDTH), jnp.float32),
        grid=(B, T // Q_BLOCK),
        in_specs=[
            pl.BlockSpec((None, Q_BLOCK, NSA_WIDTH), blk),
            _gate_block(Q_BLOCK),
            kv_col(0), kv_col(1), kv_col(0), kv_col(1), kv_col(2), kv_col(3),
        ],
        out_specs=pl.BlockSpec((None, Q_BLOCK, NSA_WIDTH), blk),
        scratch_shapes=[
            pltpu.VMEM((LANE, KV_LANES), jnp.float32),
            pltpu.VMEM((LANE, KV_LANES), jnp.float32),
            pltpu.VMEM((N_ROWS, 1), jnp.float32),
            pltpu.VMEM((N_ROWS, 1), jnp.float32),
            pltpu.VMEM((N_ROWS, KV_LANES), jnp.float32),
        ],
        compiler_params=pltpu.CompilerParams(
            dimension_semantics=("parallel", "arbitrary"), vmem_limit_bytes=48 * 1024 * 1024),
        name="nsa_prompt",
    )(q, proj3, rows, rows, kvb, kvb, kvb, kvb)


PAGES_PER_STEP = 16
STASH_CHUNK = 1024
NEW_PAD = 128


def _qpad_rows(q, nq):
    low = lax.broadcasted_iota(jnp.int32, (nq, LANE), 1) < NSA_HD
    parts = []
    for g in range(NSA_KV):
        for r in range(NSA_REP):
            qr = q[:, r * KV_LANES:(r + 1) * KV_LANES]
            parts.append(jnp.where(low if g == 0 else ~low, qr, 0.0))
    return jnp.concatenate(parts, axis=0).astype(jnp.bfloat16)


def _nsa_sample_kernel(pt_ref, *refs, n_q, past_len, n_top):
    f32, bf16 = jnp.float32, jnp.bfloat16
    pages = refs[:PAGES_PER_STEP]
    (q_ref, gate_ref, ksn_ref, vsn_ref, win_ref, kwn_ref, vwn_ref, o_ref,
     kcb_ref, vcb_ref, sel_ref, ocmp_ref, m_ref, l_ref, acc_ref, stash_ref) = refs[PAGES_PER_STEP:]
    p = pl.program_id(1)
    n_steps = pl.num_programs(1) - 1
    last = n_steps - 1
    n_rows = NSA_KV * NSA_REP * n_q
    n_blk = past_len // SLC_BLOCK
    cmp_per_step = PAGES_PER_STEP * PAGE_SIZE // CMP_BLOCK
    keys_per_step = PAGES_PER_STEP * PAGE_SIZE
    heads = (NSA_KV, NSA_REP, n_q)
    qpad = _qpad_rows(q_ref[...], n_q)

    @pl.when(p < n_steps)
    def _():
        pair = 2 * KV_LANES
        means = [pg[:, :pair].reshape(PAGE_SIZE // CMP_BLOCK, CMP_BLOCK, pair).sum(axis=1) * (1.0 / CMP_BLOCK)
                 for pg in pages]
        means = jnp.concatenate(means, axis=0)
        row0 = pl.multiple_of(p * cmp_per_step, cmp_per_step)
        kcb_ref[pl.ds(row0, cmp_per_step), :] = means[:, :KV_LANES]
        vcb_ref[pl.ds(row0, cmp_per_step), :] = means[:, KV_LANES:]
        key0 = pl.multiple_of(p * keys_per_step, keys_per_step)
        for k, pg in enumerate(pages):
            stash_ref[pl.ds(key0 + k * PAGE_SIZE, PAGE_SIZE), :] = pg[:, pair:].astype(bf16)

    @pl.when(p == last)
    def _():
        halves = []
        for par in range(2):
            kc = kcb_ref[pl.ds(par, n_blk, stride=2), :].astype(bf16)
            halves.append(_dot_t(qpad, kc))
        mx = jnp.maximum(jnp.max(halves[0], axis=-1, keepdims=True), jnp.max(halves[1], axis=-1, keepdims=True))
        e0, e1 = jnp.exp(halves[0] - mx), jnp.exp(halves[1] - mx)
        inv = 1.0 / (jnp.sum(e0, axis=-1, keepdims=True) + jnp.sum(e1, axis=-1, keepdims=True))
        p0, p1 = e0 * inv, e1 * inv
        oc = jnp.zeros((n_rows, KV_LANES), f32)
        for par, pp in ((0, p0), (1, p1)):
            vc = vcb_ref[pl.ds(par, n_blk, stride=2), :].astype(bf16)
            oc = oc + jnp.dot(pp.astype(bf16), vc, preferred_element_type=f32)
        ocmp_ref[...] = oc
        imp = (p0 + p1).reshape(heads + (n_blk,)).sum(axis=1).reshape(NSA_KV * n_q, n_blk)
        lane = lax.broadcasted_iota(jnp.int32, imp.shape, 1)
        v = jnp.where(lane == 0, jnp.inf, imp)
        rank = jnp.zeros(imp.shape, f32)
        for j in range(n_blk):
            col = v[:, j:j + 1]
            tie = jnp.where(lane > j, 1.0, 0.0)
            rank = rank + jnp.where(col > v, 1.0, jnp.where(col == v, tie, 0.0))
        sel_ref[...] = jnp.where(rank < n_top - 1, 1.0, 0.0)
        m_ref[...] = jnp.full(m_ref.shape, NEG_BIG, f32)
        l_ref[...] = jnp.zeros(l_ref.shape, f32)
        acc_ref[...] = jnp.zeros(acc_ref.shape, f32)

    def online(s4, vals):
        m_old = m_ref[...].reshape(heads + (1,))
        m_new = jnp.maximum(m_old, jnp.max(s4, axis=-1, keepdims=True))
        alpha = jnp.exp(m_old - m_new)
        ex = jnp.exp(s4 - m_new)
        l_new = alpha * l_ref[...].reshape(heads + (1,)) + jnp.sum(ex, axis=-1, keepdims=True)
        pv = jnp.dot(ex.reshape(n_rows, s4.shape[-1]).astype(bf16), vals, preferred_element_type=f32)
        acc_ref[...] = alpha.reshape(n_rows, 1) * acc_ref[...] + pv
        l_ref[...] = l_new.reshape(n_rows, 1)
        m_ref[...] = m_new.reshape(n_rows, 1)

    @pl.when(p == n_steps)
    def _():
        sel = sel_ref[...].astype(bf16)

        def chunk(c, carry):
            key0 = pl.multiple_of(c * STASH_CHUNK, STASH_CHUNK)
            kv = stash_ref[pl.ds(key0, STASH_CHUNK), :]
            sc = _dot_t(qpad, kv[:, :KV_LANES])
            kblk = (key0 + lax.broadcasted_iota(jnp.int32, (n_blk, STASH_CHUNK), 1)) // SLC_BLOCK
            expand = jnp.where(kblk == lax.broadcasted_iota(jnp.int32, (n_blk, STASH_CHUNK), 0), 1.0, 0.0).astype(bf16)
            picked = jnp.dot(sel, expand, preferred_element_type=f32) > 0.5
            bias = jnp.where(picked, 0.0, NEG_BIG).reshape(NSA_KV, 1, n_q, STASH_CHUNK)
            online(sc.reshape(heads + (STASH_CHUNK,)) + bias, kv[:, KV_LANES:])
            return carry

        lax.fori_loop(0, past_len // STASH_CHUNK, chunk, 0)

        tq = lax.broadcasted_iota(jnp.int32, (n_q, NEW_PAD), 0)
        tk = lax.broadcasted_iota(jnp.int32, (n_q, NEW_PAD), 1)
        new_bias = jnp.where(tk <= tq, 0.0, NEG_BIG)[None, None]
        s_new = _dot_t(qpad, ksn_ref[...]).reshape(heads + (NEW_PAD,)) + new_bias
        online(s_new, vsn_ref[...])
        o_slc = acc_ref[...] / l_ref[...]

        win = win_ref[...]
        n_prev = win.shape[0]
        s_prev = _dot_t(qpad, win[:, :KV_LANES].astype(bf16))
        pk = lax.broadcasted_iota(jnp.int32, (n_q, n_prev), 1)
        pq = lax.broadcasted_iota(jnp.int32, (n_q, n_prev), 0)
        s_prev = s_prev.reshape(heads + (n_prev,)) + jnp.where(pk > pq + (n_prev - WINDOW), 0.0, NEG_BIG)[None, None]
        s_wnew = _dot_t(qpad, kwn_ref[...]).reshape(heads + (NEW_PAD,)) + new_bias
        mw = jnp.maximum(jnp.max(s_prev, axis=-1, keepdims=True), jnp.max(s_wnew, axis=-1, keepdims=True))
        e_prev, e_new = jnp.exp(s_prev - mw), jnp.exp(s_wnew - mw)
        lw = jnp.sum(e_prev, axis=-1, keepdims=True) + jnp.sum(e_new, axis=-1, keepdims=True)
        o_win = (jnp.dot(e_prev.reshape(n_rows, n_prev).astype(bf16), win[:, KV_LANES:].astype(bf16),
                         preferred_element_type=f32)
                 + jnp.dot(e_new.reshape(n_rows, NEW_PAD).astype(bf16), vwn_ref[...], preferred_element_type=f32))
        o_win = o_win / lw.reshape(n_rows, 1)

        gate = jax.nn.sigmoid(gate_ref[...])
        low = lax.broadcasted_iota(jnp.int32, (n_q, LANE), 1) < NSA_HD
        o_cmp = ocmp_ref[...]
        n_hd = NSA_KV * NSA_REP
        for r in range(NSA_REP):
            per_g = []
            for g in range(NSA_KV):
                h = g * NSA_REP + r
                rows = slice(h * n_q, (h + 1) * n_q)
                per_g.append(gate[:, h:h + 1] * o_cmp[rows]
                             + gate[:, n_hd + h:n_hd + h + 1] * o_slc[rows]
                             + gate[:, 2 * n_hd + h:2 * n_hd + h + 1] * o_win[rows])
            o_ref[:, r * KV_LANES:(r + 1) * KV_LANES] = jnp.where(low, per_g[0], per_g[1])


def _nsa_sample(page_table, cache, layer, q, proj3, kvb_new, win_prev):
    B, T, _ = q.shape
    n_pages = page_table.shape[1]
    past_len = n_pages * PAGE_SIZE
    n_prev = win_prev.shape[1]
    n_blk = past_len // SLC_BLOCK
    assert T % 8 == 0 and T < CMP_BLOCK and n_pages % PAGES_PER_STEP == 0 and n_blk % LANE == 0
    assert n_blk <= LANE and n_prev == WINDOW
    n_top = min(N_SEL, n_blk + 1)
    n_rows = NSA_KV * NSA_REP * T
    assert past_len % STASH_CHUNK == 0
    per_b = lambda b, p, pt: (b, 0, 0)
    n_steps = n_pages // PAGES_PER_STEP

    def page_spec(k):
        def index(b, p, pt):
            return (layer, pt[b, jnp.minimum(p, n_steps - 1) * PAGES_PER_STEP + k], 0, 0)
        return pl.BlockSpec((None, None, PAGE_SIZE, 4 * KV_LANES), index)

    new_col = lambda c: pl.BlockSpec((None, NEW_PAD, KV_LANES), lambda b, p, pt: (b, 0, c))
    gate_col = _SEG_DST["nsa_g"] // LANE
    grid_spec = pltpu.PrefetchScalarGridSpec(
        num_scalar_prefetch=1,
        grid=(B, n_steps + 1),
        in_specs=[page_spec(k) for k in range(PAGES_PER_STEP)] + [
            pl.BlockSpec((None, T, NSA_WIDTH), per_b),
            pl.BlockSpec((None, T, LANE), lambda b, p, pt: (b, 0, gate_col)),
            new_col(0), new_col(1),
            pl.BlockSpec((None, n_prev, 2 * KV_LANES), per_b),
            new_col(2), new_col(3),
        ],
        out_specs=pl.BlockSpec((None, T, NSA_WIDTH), per_b),
        scratch_shapes=[
            pltpu.VMEM((past_len // CMP_BLOCK, KV_LANES), jnp.float32),
            pltpu.VMEM((past_len // CMP_BLOCK, KV_LANES), jnp.float32),
            pltpu.VMEM((NSA_KV * T, n_blk), jnp.float32),
            pltpu.VMEM((n_rows, KV_LANES), jnp.float32),
            pltpu.VMEM((n_rows, 1), jnp.float32),
            pltpu.VMEM((n_rows, 1), jnp.float32),
            pltpu.VMEM((n_rows, KV_LANES), jnp.float32),
            pltpu.VMEM((past_len, 2 * KV_LANES), jnp.bfloat16),
        ],
    )
    return pl.pallas_call(
        functools.partial(_nsa_sample_kernel, n_q=T, past_len=past_len, n_top=n_top),
        out_shape=jax.ShapeDtypeStruct((B, T, NSA_WIDTH), jnp.float32),
        grid_spec=grid_spec,
        compiler_params=pltpu.CompilerParams(dimension_semantics=("parallel", "arbitrary")),
        name="nsa_sample",
    )(page_table, *([cache] * PAGES_PER_STEP), q, proj3, kvb_new, kvb_new, win_prev, kvb_new, kvb_new)


LRU_TILE = 512
SUBLANES = 8


EXPM1_SERIES_RANGE = 0.35
_EXPM1_COEFFS = tuple(1.0 / math.factorial(n) for n in range(9, 1, -1))


def _one_minus_exp(y):
    acc = jnp.full_like(y, _EXPM1_COEFFS[0])
    for c in _EXPM1_COEFFS[1:]:
        acc = acc * y + c
    series = -(y * (1.0 + y * acc))
    return jnp.where(y > -EXPM1_SERIES_RANGE, series, 1.0 - jnp.exp(y))


def _lru_kernel(x_ref, buf_ref, h0_ref, cw_ref, cb_ref, wa_ref, ba_ref, wx_ref, bx_ref, lsl_ref,
                hs_ref, hlast_ref, cbuf_ref, xe_ref, h_ref, *, tile):
    f32 = jnp.float32
    t = pl.program_id(1)
    tail = CONV_W - 1

    @pl.when(t == 0)
    def _():
        xe_ref[0:SUBLANES, :] = jnp.zeros((SUBLANES, D_RNN), f32)
        xe_ref[SUBLANES - tail:SUBLANES, :] = buf_ref[...]
        h_ref[...] = h0_ref[...]

    xe_ref[SUBLANES:SUBLANES + tile, :] = x_ref[...]
    cw = cw_ref[...]
    u = cb_ref[...]
    for j in range(CONV_W):
        u = u + cw[j:j + 1] * xe_ref[SUBLANES - tail + j:SUBLANES - tail + j + tile, :]

    @pl.when(t == pl.num_programs(1) - 1)
    def _():
        cbuf_ref[...] = xe_ref[SUBLANES + tile - tail:SUBLANES + tile, :]

    xe_ref[0:SUBLANES, :] = xe_ref[tile:tile + SUBLANES, :]

    ub = u.astype(jnp.bfloat16)
    r = jax.nn.sigmoid(jnp.dot(ub, wa_ref[...], preferred_element_type=f32) + ba_ref[...])
    i = jax.nn.sigmoid(jnp.dot(ub, wx_ref[...], preferred_element_type=f32) + bx_ref[...])
    log_a = LRU_C * r * lsl_ref[...]
    a = jnp.exp(log_a)
    b = jnp.sqrt(_one_minus_exp(2.0 * log_a)) * (i * u)

    row = lax.broadcasted_iota(jnp.int32, (SUBLANES, D_RNN), 0)
    h = h_ref[...]
    for k in range(tile // SUBLANES):
        ak = a[k * SUBLANES:(k + 1) * SUBLANES]
        bk = b[k * SUBLANES:(k + 1) * SUBLANES]
        for s in (1, 2, 4):
            a_prev = jnp.where(row >= s, pltpu.roll(ak, s, axis=0), 1.0)
            b_prev = jnp.where(row >= s, pltpu.roll(bk, s, axis=0), 0.0)
            bk = ak * b_prev + bk
            ak = ak * a_prev
        hk = ak * h + bk
        hs_ref[k * SUBLANES:(k + 1) * SUBLANES, :] = hk
        h = hk[SUBLANES - 1:SUBLANES]
    h_ref[...] = h
    hlast_ref[...] = h


def _block_diag(w):
    n, c, d = w.shape
    return jnp.einsum('ncd,nm->ncmd', w, jnp.eye(n, dtype=w.dtype)).reshape(n * c, n * d)


def _lru(proj, B, T, buf, h0, conv_w, conv_b, wa_bd, ba, wx_bd, bx, lam):
    tile = min(LRU_TILE, T)
    assert T % tile == 0 and tile % SUBLANES == 0 and _SEG_DST["lru_x"] == 0
    row = lambda a: a.reshape(1, D_RNN)
    per_b = lambda b, t: (b, 0, 0)
    const = lambda b, t: (0, 0)
    vec = pl.BlockSpec((1, D_RNN), const)
    mat = pl.BlockSpec((D_RNN, D_RNN), const)
    hs, hlast, cbuf = pl.pallas_call(
        functools.partial(_lru_kernel, tile=tile),
        out_shape=(jax.ShapeDtypeStruct((B, T, D_RNN), jnp.float32),
                   jax.ShapeDtypeStruct((B, 1, D_RNN), jnp.float32),
                   jax.ShapeDtypeStruct((B, CONV_W - 1, D_RNN), jnp.float32)),
        grid=(B, T // tile),
        in_specs=[
            pl.BlockSpec((None, tile, D_RNN), lambda b, t: (b, t, 0)),
            pl.BlockSpec((None, CONV_W - 1, D_RNN), per_b),
            pl.BlockSpec((None, 1, D_RNN), per_b),
            pl.BlockSpec((CONV_W, D_RNN), const),
            vec, mat, vec, mat, vec, vec,
        ],
        out_specs=(pl.BlockSpec((None, tile, D_RNN), lambda b, t: (b, t, 0)),
                   pl.BlockSpec((None, 1, D_RNN), per_b),
                   pl.BlockSpec((None, CONV_W - 1, D_RNN), per_b)),
        scratch_shapes=[pltpu.VMEM((tile + SUBLANES, D_RNN), jnp.float32),
                        pltpu.VMEM((1, D_RNN), jnp.float32)],
        compiler_params=pltpu.CompilerParams(dimension_semantics=("parallel", "arbitrary")),
        name="rg_lru",
    )(proj.reshape(B, T, IN_COLS_PAD), buf, h0.reshape(B, 1, D_RNN), conv_w, row(conv_b),
      wa_bd, row(ba), wx_bd, row(bx), row(jax.nn.log_sigmoid(lam)))
    return hs, hlast.reshape(B, D_RNN), cbuf


DN_CHUNKS_PER_STEP = 4


def _split2(a):
    hi = a.astype(jnp.bfloat16)
    return hi, (a - hi.astype(jnp.float32)).astype(jnp.bfloat16)


def _dot_hi(a, b):
    a1, a2 = _split2(a)
    b1, b2 = _split2(b)
    d = lambda x, y: jnp.dot(x, y, preferred_element_type=jnp.float32)
    return d(a1, b1) + (d(a1, b2) + d(a2, b1))


def _unit_lower_solve(n_mats, rhss):
    C, n = rhss[0].shape
    f32 = jnp.float32
    ci = lax.broadcasted_iota(jnp.int32, (C, C), 0)
    ei = lax.broadcasted_iota(jnp.int32, (C, C), 1)
    same = (ci // SUBLANES) == (ei // SUBLANES)
    eye = jnp.where(ci == ei, 1.0, 0.0)
    diags = [jnp.where(same, m, 0.0) for m in n_mats]
    offs = [jnp.where(same, 0.0, m) for m in n_mats]
    d2 = [_dot_hi(d, d) for d in diags]
    d4 = [_dot_hi(d, d) for d in d2]
    part = [_dot_hi(eye - d, eye + s) for d, s in zip(diags, d2)]
    dinv = [_dot_hi(p, eye + s) for p, s in zip(part, d4)]
    solved = [[] for _ in rhss]
    for i in range(C // SUBLANES):
        rows = slice(i * SUBLANES, (i + 1) * SUBLANES)
        ys = [r[rows] for r in rhss]
        if i:
            below = jnp.zeros((C - i * SUBLANES, n), f32)
            ys = [y - _dot_hi(off[rows], jnp.concatenate(done + [below], axis=0))
                  for y, off, done in zip(ys, offs, solved)]
        above = [jnp.zeros((i * SUBLANES, n), f32)] if i else []
        rest = [jnp.zeros((C - (i + 1) * SUBLANES, n), f32)] if (i + 1) * SUBLANES < C else []
        xs = [_dot_hi(inv[rows], jnp.concatenate(above + [y] + rest, axis=0)) for inv, y in zip(dinv, ys)]
        for done, x in zip(solved, xs):
            done.append(x)
    return [jnp.concatenate(done, axis=0) for done in solved]


def _dn_kernel(xq_ref, xk_ref, xv_ref, ab_ref, buf_ref, s0_ref, cw_ref, alog_ref, dtb_ref, onorm_ref,
               o_ref, s_out_ref, cbuf_ref, xe_ref, s_ref, *, n_in):
    f32, bf16 = jnp.float32, jnp.bfloat16
    C = DN_CHUNK
    n = pl.program_id(1)
    last = pl.num_programs(1) - 1
    tail = CONV_W - 1
    width = 3 * DN_WIDTH

    @pl.when(n == 0)
    def _():
        xe_ref[...] = jnp.zeros(xe_ref.shape, f32)
        xe_ref[SUBLANES - tail:SUBLANES, :] = buf_ref[...]
        s_ref[...] = s0_ref[...]

    xe_ref[SUBLANES:SUBLANES + n_in, 0:DN_WIDTH] = xq_ref[...]
    xe_ref[SUBLANES:SUBLANES + n_in, DN_WIDTH:2 * DN_WIDTH] = xk_ref[...]
    xe_ref[SUBLANES:SUBLANES + n_in, 2 * DN_WIDTH:width] = xv_ref[...]
    n_chunks = -(-n_in // C)
    R = n_chunks * C
    cw = cw_ref[...]
    y = jnp.zeros((R, width), f32)
    for j in range(CONV_W):
        y = y + cw[j:j + 1] * xe_ref[SUBLANES - tail + j:SUBLANES - tail + j + R, :]

    @pl.when(n == last)
    def _():
        cbuf_ref[...] = xe_ref[SUBLANES + n_in - tail:SUBLANES + n_in, :]

    xe_ref[0:SUBLANES, :] = xe_ref[n_in:n_in + SUBLANES, :]

    valid = lax.broadcasted_iota(jnp.int32, (R, 1), 0) < n_in
    y = jnp.where(valid, y * jax.nn.sigmoid(y), 0.0)
    ab = ab_ref[...]
    if n_in < R:
        ab = jnp.concatenate([ab, jnp.zeros((R - n_in, LANE), f32)], axis=0)
    g_all = jnp.where(valid, -jnp.exp(alog_ref[...]) * jax.nn.softplus(ab + dtb_ref[...]), 0.0)
    beta_all = jnp.where(valid, jax.nn.sigmoid(ab), 0.0)

    ci = lax.broadcasted_iota(jnp.int32, (C, C), 0)
    ei = lax.broadcasted_iota(jnp.int32, (C, C), 1)
    tri = jnp.where(ei <= ci, 1.0, 0.0).astype(bf16)
    csum = lambda x: jnp.dot(tri, x, preferred_element_type=f32)
    heads = range(DN_HEADS)
    mm = lambda a, b: jnp.dot(a, b, preferred_element_type=f32)
    col = lambda a, h: a[:, h:h + 1]

    q_dec, qks, k_dec, e_last, n_mats, rhss = [], [], [], [], [], []
    for c in range(n_chunks):
        rows = slice(c * C, (c + 1) * C)
        g_c = g_all[rows]
        g1 = g_c.astype(bf16)
        r1 = g_c - g1.astype(f32)
        g2 = r1.astype(bf16)
        g3 = (r1 - g2.astype(f32)).astype(bf16)
        G = csum(g1) + (csum(g2) + csum(g3))
        G_t = G.T
        g_last = G[C - 1:C, :]
        e_g = jnp.exp(G)
        e_rest = jnp.exp(g_last - G)
        e_last.append(jnp.exp(g_last))
        head_lanes = lambda base, h: y[rows, base + h * DN_HD:base + (h + 1) * DN_HD]
        qs = [head_lanes(0, h) for h in heads]
        ks = [head_lanes(DN_WIDTH, h) for h in heads]
        vs = [head_lanes(2 * DN_WIDTH, h) for h in heads]
        qs = [q * lax.rsqrt(jnp.sum(q * q, axis=-1, keepdims=True) + L2_EPS) * (DN_HD ** -0.5) for q in qs]
        ks = [k * lax.rsqrt(jnp.sum(k * k, axis=-1, keepdims=True) + L2_EPS) for k in ks]
        decays = [jnp.where(ei <= ci, jnp.exp(col(G, h) - G_t[h:h + 1, :]), 0.0) for h in heads]
        betas = [col(beta_all[rows], DN_HEADS + h) for h in heads]
        kbs = [k.astype(bf16) for k in ks]
        kks = [_dot_t(kb, kb) for kb in kbs]
        qk = [_dot_t(q.astype(bf16), kb) for q, kb in zip(qs, kbs)]
        n_mats += [jnp.where(ei < ci, b * kk * d, 0.0) for b, kk, d in zip(betas, kks, decays)]
        qks.append([(a * d).astype(bf16) for a, d in zip(qk, decays)])
        rhss += [jnp.concatenate([betas[h] * vs[h], betas[h] * ks[h] * col(e_g, h)], axis=1) for h in heads]
        q_dec.append([(qs[h] * col(e_g, h)).astype(bf16) for h in heads])
        k_dec.append([(ks[h] * col(e_rest, h)).astype(bf16) for h in heads])
    xs = _unit_lower_solve(n_mats, rhss)

    states = [s_ref[h] for h in heads]
    o_gain = onorm_ref[...]
    for c in range(n_chunks):
        xc = xs[c * DN_HEADS:(c + 1) * DN_HEADS]
        sbs = [S.astype(bf16) for S in states]
        v_news = [x[:, :DN_HD] - mm(x[:, DN_HD:].astype(bf16), sb) for x, sb in zip(xc, sbs)]
        vbs = [v.astype(bf16) for v in v_news]
        o_state = [mm(qd, sb) for qd, sb in zip(q_dec[c], sbs)]
        o_local = [mm(qk, vb) for qk, vb in zip(qks[c], vbs)]
        grow = [lax.dot_general(kd, vb, (((0,), (0,)), ((), ())), preferred_element_type=f32)
                for kd, vb in zip(k_dec[c], vbs)]
        states = [col(e_last[c], h) * states[h] + grow[h] for h in heads]
        n_out = min(C, n_in)
        for h in heads:
            o = o_state[h] + o_local[h]
            o = o * lax.rsqrt(jnp.mean(o * o, axis=-1, keepdims=True) + RMS_EPS) * o_gain
            o_ref[c * C:c * C + n_out, h * DN_HD:(h + 1) * DN_HD] = o[:n_out]
    for h in heads:
        s_ref[h] = states[h]

    @pl.when(n == last)
    def _():
        s_out_ref[...] = s_ref[...]


def _gated_delta_net(proj, B, T, buf, S0, conv_w, A_log, dt_bias, o_norm):
    n_in = min(DN_CHUNKS_PER_STEP * DN_CHUNK, T)
    assert T % n_in == 0 and n_in % SUBLANES == 0 and (n_in <= DN_CHUNK or n_in % DN_CHUNK == 0)
    rows_pad = -(-n_in // DN_CHUNK) * DN_CHUNK
    qkv0 = _SEG_DST["dn_qkv"] // DN_WIDTH
    assert _SEG_DST["dn_qkv"] % DN_WIDTH == 0 and _SEG_DST["dn_b"] == _SEG_DST["dn_a"] + DN_HEADS
    ab_blk = _SEG_DST["dn_a"] // LANE
    proj3 = proj.reshape(B, T, IN_COLS_PAD)
    col = lambda c: (lambda b, n: (b, n, c))
    per_b = lambda b, n: (b, 0, 0)
    const = lambda b, n: (0, 0)
    pad_row = lambda a, off: jnp.zeros((1, LANE), jnp.float32).at[0, off:off + DN_HEADS].set(a)
    o, s_out, cbuf = pl.pallas_call(
        functools.partial(_dn_kernel, n_in=n_in),
        out_shape=(jax.ShapeDtypeStruct((B, T, DN_WIDTH), jnp.float32),
                   jax.ShapeDtypeStruct((B, DN_HEADS, DN_HD, DN_HD), jnp.float32),
                   jax.ShapeDtypeStruct((B, CONV_W - 1, 3 * DN_WIDTH), jnp.float32)),
        grid=(B, T // n_in),
        in_specs=[
            pl.BlockSpec((None, n_in, DN_WIDTH), col(qkv0)),
            pl.BlockSpec((None, n_in, DN_WIDTH), col(qkv0 + 1)),
            pl.BlockSpec((None, n_in, DN_WIDTH), col(qkv0 + 2)),
            pl.BlockSpec((None, n_in, LANE), col(ab_blk)),
            pl.BlockSpec((None, CONV_W - 1, 3 * DN_WIDTH), per_b),
            pl.BlockSpec((None, DN_HEADS, DN_HD, DN_HD), lambda b, n: (b, 0, 0, 0)),
            pl.BlockSpec((CONV_W, 3 * DN_WIDTH), const),
            pl.BlockSpec((1, LANE), const),
            pl.BlockSpec((1, LANE), const),
            pl.BlockSpec((1, DN_HD), const),
        ],
        out_specs=(pl.BlockSpec((None, n_in, DN_WIDTH), lambda b, n: (b, n, 0)),
                   pl.BlockSpec((None, DN_HEADS, DN_HD, DN_HD), lambda b, n: (b, 0, 0, 0)),
                   pl.BlockSpec((None, CONV_W - 1, 3 * DN_WIDTH), per_b)),
        scratch_shapes=[pltpu.VMEM((rows_pad + 2 * SUBLANES, 3 * DN_WIDTH), jnp.float32),
                        pltpu.VMEM((DN_HEADS, DN_HD, DN_HD), jnp.float32)],
        compiler_params=pltpu.CompilerParams(dimension_semantics=("parallel", "arbitrary")),
        name="gated_delta",
    )(proj3, proj3, proj3, proj3, buf, S0, conv_w, pad_row(A_log, 0), pad_row(dt_bias, 0),
      o_norm.reshape(1, DN_HD))
    return o, s_out, cbuf


def _layer(x, paged, lru_h0, lru_buf, dn_S0, dn_buf,
           norm_g, w_in_pad, lru_conv_w, lru_conv_b, lru_wa, lru_ba, lru_wx, lru_bx, lru_lam,
           q_norm, k_norm, dn_conv_w, dn_A_log, dn_dt_bias, dn_o_norm,
           w_lru_out, w_nsa_out, w_dn_out, w_out):
    B, T, _ = x.shape
    P = 0 if paged is None else paged[0].shape[1] * PAGE_SIZE
    pos = P + jnp.arange(T, dtype=jnp.int32)
    x2d = x.reshape(B * T, D_MODEL)
    proj = _in_proj(x2d, norm_g, w_in_pad)
    proj3 = proj.reshape(B, T, IN_COLS_PAD)

    lru_seq, lru_h, lru_buf_new = _lru(proj, B, T, lru_buf, lru_h0, lru_conv_w, lru_conv_b,
                                       lru_wa, lru_ba, lru_wx, lru_bx, lru_lam)

    q, rows, win_rows, kvb = _nsa_prep(proj3, pos, q_norm, k_norm)
    if paged is None:
        o_nsa = _nsa_prompt(q, proj3, rows, kvb)
        win_state = win_rows[:, -min(WINDOW, T):]
    else:
        page_table, cache, layer, win_prev = paged
        win_prev = win_prev.reshape(B, win_prev.shape[1], 2 * KV_LANES)
        kvb_new = jnp.pad(kvb, ((0, 0), (0, NEW_PAD - T), (0, 0)))
        o_nsa = _nsa_sample(page_table, cache, layer, q, proj3, kvb_new, win_prev)
        win_state = jnp.concatenate([win_prev, win_rows], axis=1)[:, -WINDOW:]
    rows = rows.reshape(B, T, 4, NSA_KV, NSA_HD)
    win_state = win_state.reshape(B, win_state.shape[1], 2, NSA_KV, NSA_HD)

    o_dn, dn_S, dn_buf_new = _gated_delta_net(proj, B, T, dn_buf, dn_S0, dn_conv_w, dn_A_log, dn_dt_bias,
                                              dn_o_norm)

    y2d = _out_stage(x2d, proj, lru_seq.reshape(B * T, D_RNN), o_nsa.reshape(B * T, NSA_WIDTH),
                     o_dn.reshape(B * T, DN_WIDTH), w_lru_out, w_nsa_out, w_dn_out, w_out)
    return y2d.reshape(B, T, D_MODEL), (rows, win_state, lru_h, lru_buf_new, dn_S, dn_buf_new)


def kernel(x_prompt, x_sample, cache_nsa_kv, page_table, state_nsa_win, state_lru_h, state_lru_conv,
           state_dn_S, state_dn_conv, norm_gain, w_in, lru_conv_w, lru_conv_b, lru_wa, lru_ba,
           lru_wx, lru_bx, lru_lambda, nsa_q_norm, nsa_k_norm, dn_conv_w, dn_A_log, dn_dt_bias,
           dn_o_norm, w_lru_out, w_nsa_out, w_dn_out, w_out):
    Bp = x_prompt.shape[0]
    dt = x_prompt.dtype
    bf16 = jnp.bfloat16
    cache = cache_nsa_kv.reshape(cache_nsa_kv.shape[:2] + (PAGE_SIZE, 4 * KV_LANES))
    p_h0 = jnp.zeros((Bp, D_RNN), jnp.float32)
    p_lbuf0 = jnp.zeros((Bp, CONV_W - 1, D_RNN), dt)
    p_S0 = jnp.zeros((Bp, DN_HEADS, DN_HD, DN_HD), jnp.float32)
    p_dbuf0 = jnp.zeros((Bp, CONV_W - 1, 3 * DN_WIDTH), dt)
    wa_bd = jax.vmap(_block_diag)(lru_wa).astype(bf16)
    wx_bd = jax.vmap(_block_diag)(lru_wx).astype(bf16)
    weights = (norm_gain, _pad_in_weight(w_in), lru_conv_w, lru_conv_b, wa_bd, lru_ba, wx_bd, lru_bx,
               lru_lambda, nsa_q_norm, nsa_k_norm, dn_conv_w, dn_A_log, dn_dt_bias, dn_o_norm,
               w_lru_out.astype(bf16), _heads_rg(w_nsa_out.swapaxes(1, 2)).swapaxes(1, 2).astype(bf16),
               w_dn_out.astype(bf16), w_out.astype(bf16))
    xp, xs = x_prompt, x_sample
    st_p, st_s = [], []
    for l in range(DEPTH):
        lw = [w[l] for w in weights]
        xp, sp = _layer(xp, None, p_h0, p_lbuf0, p_S0, p_dbuf0, *lw)
        xs, ss = _layer(xs, (page_table, cache, l, state_nsa_win[l]), state_lru_h[l], state_lru_conv[l],
                        state_dn_S[l], state_dn_conv[l], *lw)
        st_p.append(sp)
        st_s.append(ss)
    kv_p, win_p, lh_p, lc_p, S_p, dc_p = [jnp.stack(a) for a in zip(*st_p)]
    kv_s, win_s, lh_s, lc_s, S_s, dc_s = [jnp.stack(a) for a in zip(*st_s)]
    return (xp, xs, kv_p, kv_s, win_p, win_s, lh_p, lh_s, lc_p, lc_s, S_p, S_s, dc_p, dc_s)
```

```python
import functools
import math

import numpy as np
import jax
import jax.numpy as jnp
from jax import lax
from jax.experimental import pallas as pl
from jax.experimental.pallas import tpu as pltpu

D_MODEL = 1024
DEPTH = 4
PAGE_SIZE = 128
CONV_W = 4
RMS_EPS = 1e-6
L2_EPS = 1e-6
N_BRANCH = 3
D_RNN = D_MODEL // 2
LRU_BLOCKS = 8
LRU_BS = D_RNN // LRU_BLOCKS
LRU_C = 8.0
NSA_HEADS = 8
NSA_HD = 64
NSA_KV = 2
NSA_REP = NSA_HEADS // NSA_KV
NSA_WIDTH = NSA_HEADS * NSA_HD
CMP_BLOCK = 32
SLC_BLOCK = 64
N_SEL = 16
WINDOW = 512
Q_BLOCK = 128
ROT_DIM = NSA_HD // 4
ROPE_THETA = 500000.0
DN_HEADS = 4
DN_HD = 128
DN_WIDTH = DN_HEADS * DN_HD
DN_CHUNK = 64

LANE = 128

_SEG_NAMES = ("lru_x", "lru_z", "nsa_q", "nsa_kv", "nsa_z", "nsa_g", "dn_qkv", "dn_z", "dn_a", "dn_b", "merge_g")
_SEG_SIZES = (D_RNN, D_RNN, NSA_WIDTH, 2 * N_BRANCH * NSA_KV * NSA_HD, NSA_WIDTH, N_BRANCH * NSA_HEADS,
              3 * DN_WIDTH, DN_WIDTH, DN_HEADS, DN_HEADS, N_BRANCH * D_MODEL)
_SEG_SRC = dict(zip(_SEG_NAMES, np.concatenate([[0], np.cumsum(_SEG_SIZES)[:-1]]).tolist()))
_SEG_LEN = dict(zip(_SEG_NAMES, _SEG_SIZES))
_DST_SLOTS = (
    (("lru_x",), D_RNN), (("lru_z",), D_RNN), (("nsa_q",), NSA_WIDTH), (("nsa_z",), NSA_WIDTH),
    (("dn_z",), DN_WIDTH), (("nsa_g",), LANE), (("dn_a", "dn_b"), LANE),
    (("nsa_kv",), 2 * N_BRANCH * NSA_KV * NSA_HD), (("dn_qkv",), 3 * DN_WIDTH), (("merge_g",), N_BRANCH * D_MODEL),
)


def _dst_layout():
    off, dst = 0, {}
    for names, width in _DST_SLOTS:
        o = off
        for n in names:
            dst[n] = o
            o += _SEG_LEN[n]
        off += width
    return dst, off


_SEG_DST, IN_COLS_PAD = _dst_layout()
PROJ_COLS = _SEG_DST["merge_g"]


def _heads_rg(a):
    lead = a.shape[:-1]
    return a.reshape(lead + (NSA_KV, NSA_REP, NSA_HD)).swapaxes(-3, -2).reshape(lead + (NSA_WIDTH,))


def _pad_in_weight(w_in):
    pieces = []
    for names, width in _DST_SLOTS:
        used = 0
        for n in names:
            piece = w_in[:, :, _SEG_SRC[n]:_SEG_SRC[n] + _SEG_LEN[n]]
            pieces.append(_heads_rg(piece) if n in ("nsa_q", "nsa_z") else piece)
            used += _SEG_LEN[n]
        if used < width:
            pieces.append(jnp.zeros(w_in.shape[:2] + (width - used,), w_in.dtype))
    return jnp.concatenate(pieces, axis=-1).astype(jnp.bfloat16)


IN_PROJ_ROWS = 2048
IN_PROJ_COLS = 512


def _in_proj_kernel(x_ref, g_ref, w_ref, o_ref, h_ref):
    @pl.when(pl.program_id(1) == 0)
    def _():
        x = x_ref[...]
        ms = jnp.mean(x * x, axis=-1, keepdims=True)
        h_ref[...] = (x * lax.rsqrt(ms + RMS_EPS) * g_ref[...]).astype(jnp.bfloat16)

    o_ref[...] = jnp.dot(h_ref[...], w_ref[...], preferred_element_type=jnp.float32)


def _in_proj(x2d, gain, w_pad):
    m = x2d.shape[0]
    tm = min(IN_PROJ_ROWS, m)
    tn = IN_PROJ_COLS
    vmem = 2 * (tm * D_MODEL * 4 + D_MODEL * tn * 2 + tm * tn * 4) + tm * D_MODEL * 2 + (4 << 20)
    return pl.pallas_call(
        _in_proj_kernel,
        out_shape=jax.ShapeDtypeStruct((m, PROJ_COLS), jnp.float32),
        grid=(m // tm, PROJ_COLS // tn),
        in_specs=[
            pl.BlockSpec((tm, D_MODEL), lambda i, j: (i, 0)),
            pl.BlockSpec((1, D_MODEL), lambda i, j: (0, 0)),
            pl.BlockSpec((D_MODEL, tn), lambda i, j: (0, j)),
        ],
        out_specs=pl.BlockSpec((tm, tn), lambda i, j: (i, j)),
        scratch_shapes=[pltpu.VMEM((tm, D_MODEL), jnp.bfloat16)],
        compiler_params=pltpu.CompilerParams(dimension_semantics=("parallel", "arbitrary"),
                                             vmem_limit_bytes=vmem),
        name="in_proj",
    )(x2d, gain.reshape(1, D_MODEL), w_pad)


def _silu(z):
    return z * jax.nn.sigmoid(z)


def _out_stage_kernel(x_ref, g_ref, a_lru, z_lru, a_nsa, z_nsa, a_dn, z_dn, wg0, wg1, wg2,
                      w_lru, w_nsa, w_dn, w_out, o_ref):
    f32 = jnp.float32
    x = x_ref[...]
    h = (x * lax.rsqrt(jnp.mean(x * x, axis=-1, keepdims=True) + RMS_EPS) * g_ref[...]).astype(jnp.bfloat16)
    gate = lambda wg: jax.nn.sigmoid(jnp.dot(h, wg[...], preferred_element_type=f32))

    def branch(a, z, w):
        y = (a[...] * _silu(z[...])).astype(jnp.bfloat16)
        return jnp.dot(y, w[...], preferred_element_type=f32)

    merged = gate(wg0) * branch(a_lru, z_lru, w_lru)
    merged = merged + gate(wg1) * branch(a_nsa, z_nsa, w_nsa)
    merged = merged + gate(wg2) * branch(a_dn, z_dn, w_dn)
    y = jnp.dot(merged.astype(jnp.bfloat16), w_out[...], preferred_element_type=f32)
    o_ref[...] = x + y


def _out_stage(x2d, gain, w_pad, proj, a_lru, a_nsa, a_dn, w_lru, w_nsa, w_dn, w_out):
    m = x2d.shape[0]
    tm = min(256, m)
    half = D_RNN
    row = lambda i: (i, 0)
    col = lambda c: (lambda i: (i, c))
    full = lambda i: (0, 0)
    mg_blk = _SEG_DST["merge_g"] // D_MODEL
    assert _SEG_DST["merge_g"] % D_MODEL == 0
    wg = lambda k: pl.BlockSpec((D_MODEL, D_MODEL), lambda i: (0, mg_blk + k))
    vmem = 2 * (2 * tm * D_MODEL * 4 + 6 * tm * half * 4 + (3 * half + 4 * D_MODEL) * D_MODEL * 2) + (8 << 20)
    return pl.pallas_call(
        _out_stage_kernel,
        out_shape=jax.ShapeDtypeStruct((m, D_MODEL), jnp.float32),
        grid=(m // tm,),
        in_specs=[
            pl.BlockSpec((tm, D_MODEL), row),
            pl.BlockSpec((1, D_MODEL), full),
            pl.BlockSpec((tm, half), row),
            pl.BlockSpec((tm, half), col(_SEG_DST["lru_z"] // half)),
            pl.BlockSpec((tm, half), row),
            pl.BlockSpec((tm, half), col(_SEG_DST["nsa_z"] // half)),
            pl.BlockSpec((tm, half), row),
            pl.BlockSpec((tm, half), col(_SEG_DST["dn_z"] // half)),
            wg(0), wg(1), wg(2),
            pl.BlockSpec((half, D_MODEL), full),
            pl.BlockSpec((half, D_MODEL), full),
            pl.BlockSpec((half, D_MODEL), full),
            pl.BlockSpec((D_MODEL, D_MODEL), full),
        ],
        out_specs=pl.BlockSpec((tm, D_MODEL), row),
        compiler_params=pltpu.CompilerParams(dimension_semantics=("parallel",), vmem_limit_bytes=vmem),
        name="out_stage",
    )(x2d, gain.reshape(1, D_MODEL), a_lru, proj, a_nsa, proj, a_dn, proj, w_pad, w_pad, w_pad,
      w_lru, w_nsa, w_dn, w_out)


PREP_TILE = 512
HALF_ROT = ROT_DIM // 2


def _rope_tables(pos):
    inv = ROPE_THETA ** (-jnp.arange(HALF_ROT, dtype=jnp.float32) * 2.0 / ROT_DIM)
    ang = pos.astype(jnp.float32)[:, None] * inv[None, :]
    c, s = jnp.cos(ang), jnp.sin(ang)
    n = pos.shape[0]
    rest = NSA_HD - ROT_DIM
    cos_h = jnp.concatenate([c, c, jnp.ones((n, rest), jnp.float32)], axis=1)
    sin_h = jnp.concatenate([-s, s, jnp.zeros((n, rest), jnp.float32)], axis=1)
    return jnp.tile(cos_h, (1, NSA_KV)), jnp.tile(sin_h, (1, NSA_KV))


def _nsa_prep_kernel(q_ref, kv0_ref, kv1_ref, kv2_ref, cos_ref, sin_ref, qn_ref, kn_ref, ones_ref,
                     q_out, rows_out, win_out, kvb_out):
    f32, bf16 = jnp.float32, jnp.bfloat16
    cos, sin = cos_ref[...], sin_ref[...]
    ones = ones_ref[...]
    n = cos.shape[0]
    first = (lax.broadcasted_iota(jnp.int32, (n, KV_LANES), 1) % NSA_HD) < HALF_ROT

    def norm_rope(x, gain):
        sq = x * x
        hi = sq.astype(bf16)
        lo = (sq - hi.astype(f32)).astype(bf16)
        ssq = (jnp.dot(hi, ones, preferred_element_type=f32) + jnp.dot(lo, ones, preferred_element_type=f32))
        y = x * lax.rsqrt(ssq * (1.0 / NSA_HD) + RMS_EPS) * gain
        partner = jnp.where(first, pltpu.roll(y, KV_LANES - HALF_ROT, axis=1), pltpu.roll(y, HALF_ROT, axis=1))
        return y * cos + partner * sin

    qn = qn_ref[...]
    for r in range(NSA_REP):
        lanes = slice(r * KV_LANES, (r + 1) * KV_LANES)
        q_out[:, lanes] = norm_rope(q_ref[:, lanes], qn) * (NSA_HD ** -0.5)
    kn = kn_ref[...]
    k_cmp = norm_rope(kv0_ref[:, :KV_LANES], kn[0:1])
    k_slc = norm_rope(kv1_ref[:, :KV_LANES], kn[1:2])
    k_win = norm_rope(kv2_ref[:, :KV_LANES], kn[2:3])
    v_slc = kv1_ref[:, KV_LANES:]
    v_win = kv2_ref[:, KV_LANES:]
    rows_out[:, 0:KV_LANES] = k_cmp
    rows_out[:, KV_LANES:2 * KV_LANES] = kv0_ref[:, KV_LANES:]
    rows_out[:, 2 * KV_LANES:3 * KV_LANES] = k_slc
    rows_out[:, 3 * KV_LANES:4 * KV_LANES] = v_slc
    win_out[:, 0:KV_LANES] = k_win
    win_out[:, KV_LANES:2 * KV_LANES] = v_win
    kvb_out[:, 0:KV_LANES] = k_slc.astype(bf16)
    kvb_out[:, KV_LANES:2 * KV_LANES] = v_slc.astype(bf16)
    kvb_out[:, 2 * KV_LANES:3 * KV_LANES] = k_win.astype(bf16)
    kvb_out[:, 3 * KV_LANES:4 * KV_LANES] = v_win.astype(bf16)


def _nsa_prep(proj3, pos, q_norm, k_norm):
    B, T, _ = proj3.shape
    tile = min(PREP_TILE, T)
    assert T % tile == 0
    pair = 2 * KV_LANES
    q_blk = _SEG_DST["nsa_q"] // NSA_WIDTH
    kv_blk = _SEG_DST["nsa_kv"] // pair
    assert _SEG_DST["nsa_q"] % NSA_WIDTH == 0 and _SEG_DST["nsa_kv"] % pair == 0
    cos_t, sin_t = _rope_tables(pos)
    lane_head = jnp.arange(KV_LANES) // NSA_HD
    ones = (lane_head[:, None] == lane_head[None, :]).astype(jnp.bfloat16)
    col = lambda c: (lambda b, t: (b, t, c))
    tab = pl.BlockSpec((tile, KV_LANES), lambda b, t: (t, 0))
    const = lambda b, t: (0, 0)
    out = lambda w: pl.BlockSpec((None, tile, w), lambda b, t: (b, t, 0))
    return pl.pallas_call(
        _nsa_prep_kernel,
        out_shape=(jax.ShapeDtypeStruct((B, T, NSA_WIDTH), jnp.float32),
                   jax.ShapeDtypeStruct((B, T, 4 * KV_LANES), jnp.float32),
                   jax.ShapeDtypeStruct((B, T, 2 * KV_LANES), jnp.float32),
                   jax.ShapeDtypeStruct((B, T, 4 * KV_LANES), jnp.bfloat16)),
        grid=(B, T // tile),
        in_specs=[
            pl.BlockSpec((None, tile, NSA_WIDTH), col(q_blk)),
            pl.BlockSpec((None, tile, pair), col(kv_blk)),
            pl.BlockSpec((None, tile, pair), col(kv_blk + 1)),
            pl.BlockSpec((None, tile, pair), col(kv_blk + 2)),
            tab, tab,
            pl.BlockSpec((1, KV_LANES), const),
            pl.BlockSpec((N_BRANCH, KV_LANES), const),
            pl.BlockSpec((KV_LANES, KV_LANES), const),
        ],
        out_specs=(out(NSA_WIDTH), out(4 * KV_LANES), out(2 * KV_LANES), out(4 * KV_LANES)),
        compiler_params=pltpu.CompilerParams(dimension_semantics=("parallel", "parallel")),
        name="nsa_prep",
    )(proj3, proj3, proj3, proj3, cos_t, sin_t, jnp.tile(q_norm, NSA_KV).reshape(1, KV_LANES),
      jnp.tile(k_norm, (1, NSA_KV)), ones)


NEG_BIG = -1e30
SLC_CHUNK = 512
N_ROWS = NSA_KV * NSA_REP * Q_BLOCK
KV_LANES = NSA_KV * NSA_HD
WIN_KEYS = WINDOW + Q_BLOCK
MAX_SLC_BLOCKS = LANE // 2


def _dot_t(a, b):
    return lax.dot_general(a, b, (((1,), (1,)), ((), ())), preferred_element_type=jnp.float32)


def _select_blocks(imp_t, cur, n_top):
    nb = imp_t.shape[0]
    jj = lax.broadcasted_iota(jnp.int32, imp_t.shape, 0)
    v = jnp.where((jj == 0) | (jj == cur), jnp.inf, jnp.where(jj > cur, -jnp.inf, imp_t))
    sub = lax.broadcasted_iota(jnp.int32, (8, imp_t.shape[1]), 0)
    ranks = []
    for a in range(nb // 8):
        va = v[8 * a:8 * a + 8]
        rank = jnp.zeros(va.shape, jnp.float32)
        for j in range(nb):
            row = v[j:j + 1]
            ge = jnp.where(row >= va, 1.0, 0.0)
            gt = jnp.where(row > va, 1.0, 0.0)
            if j < 8 * a:
                rank = rank + ge
            elif j >= 8 * a + 8:
                rank = rank + gt
            else:
                rank = rank + jnp.where(sub > (j - 8 * a), ge, gt)
        ranks.append(rank)
    rank = jnp.concatenate(ranks, axis=0)
    return jnp.where((rank < n_top) & (jj <= cur), 1.0, 0.0)


def _nsa_prompt_kernel(q_ref, gate_ref, kc_ref, vc_ref, ks_ref, vs_ref, kw_ref, vw_ref, o_ref,
                       kcb_ref, vcb_ref, m_ref, l_ref, acc_ref, *, seq_len, n_top):
    f32, bf16 = jnp.float32, jnp.bfloat16
    i = pl.program_id(1)
    nsb = seq_len // SLC_BLOCK
    half = MAX_SLC_BLOCKS

    @pl.when(i == 0)
    def _():
        if nsb < half:
            kcb_ref[...] = jnp.zeros(kcb_ref.shape, f32)
            vcb_ref[...] = jnp.zeros(vcb_ref.shape, f32)
        for src, dst in ((kc_ref, kcb_ref), (vc_ref, vcb_ref)):
            x = src[...].reshape(nsb, SLC_BLOCK, KV_LANES)
            dst[0:nsb, :] = jnp.sum(x[:, :CMP_BLOCK, :], axis=1) * (1.0 / CMP_BLOCK)
            dst[half:half + nsb, :] = jnp.sum(x[:, CMP_BLOCK:, :], axis=1) * (1.0 / CMP_BLOCK)

    lane = lax.broadcasted_iota(jnp.int32, (Q_BLOCK, LANE), 1)
    tq = lax.broadcasted_iota(jnp.int32, (Q_BLOCK, LANE), 0)
    pos = i * Q_BLOCK + tq
    low = lane < NSA_HD

    q = q_ref[...]
    parts = []
    for g in range(NSA_KV):
        for r in range(NSA_REP):
            qr = q[:, r * KV_LANES:(r + 1) * KV_LANES]
            parts.append(jnp.where(low if g == 0 else ~low, qr, 0.0))
    qpad = jnp.concatenate(parts, axis=0).astype(bf16)

    s = _dot_t(qpad, kcb_ref[...].astype(bf16))
    cblk = jnp.where(lane < half, 2 * lane, 2 * lane - (2 * half - 1))
    okc = (cblk * CMP_BLOCK + (CMP_BLOCK - 1)) <= pos
    s3 = s.reshape(NSA_KV * NSA_REP, Q_BLOCK, LANE) + jnp.where(okc, 0.0, NEG_BIG)[None]
    mx = jnp.max(s3, axis=-1, keepdims=True)
    e = jnp.where(okc[None], jnp.exp(s3 - mx), 0.0)
    den = jnp.sum(e, axis=-1, keepdims=True)
    p3 = e / jnp.where(den > 0.0, den, 1.0)
    o_cmp = jnp.dot(p3.reshape(N_ROWS, LANE).astype(bf16), vcb_ref[...].astype(bf16),
                    preferred_element_type=f32)

    pg = p3.reshape(NSA_KV, NSA_REP, Q_BLOCK, LANE).sum(axis=1)
    cur_t = (i * Q_BLOCK + lax.broadcasted_iota(jnp.int32, (half, Q_BLOCK), 1)) // SLC_BLOCK
    selq = []
    for g in range(NSA_KV):
        imp = pg[g] + pltpu.roll(pg[g], half, axis=1)
        sel_t = _select_blocks(imp.T[:half], cur_t, n_top)
        sel_full = jnp.concatenate([sel_t, jnp.zeros_like(sel_t)], axis=0)
        selq.append(sel_full.T[:, :half].astype(bf16))

    m_ref[...] = jnp.full(m_ref.shape, NEG_BIG, f32)
    l_ref[...] = jnp.zeros(l_ref.shape, f32)
    acc_ref[...] = jnp.zeros(acc_ref.shape, f32)
    heads = (NSA_KV, NSA_REP, Q_BLOCK)

    def slc_step(c, carry):
        start = pl.multiple_of(c * SLC_CHUNK, SLC_CHUNK)
        kch = ks_ref[pl.ds(start, SLC_CHUNK), :]
        vch = vs_ref[pl.ds(start, SLC_CHUNK), :]
        sc = _dot_t(qpad, kch)
        kidx = start + lax.broadcasted_iota(jnp.int32, (Q_BLOCK, SLC_CHUNK), 1)
        causal = kidx <= i * Q_BLOCK + lax.broadcasted_iota(jnp.int32, (Q_BLOCK, SLC_CHUNK), 0)
        kblk = (start + lax.broadcasted_iota(jnp.int32, (half, SLC_CHUNK), 1)) // SLC_BLOCK
        expand = jnp.where(kblk == lax.broadcasted_iota(jnp.int32, (half, SLC_CHUNK), 0), 1.0, 0.0).astype(bf16)
        bias = []
        for g in range(NSA_KV):
            picked = jnp.dot(selq[g], expand, preferred_element_type=f32) > 0.5
            bias.append(jnp.where(picked & causal, 0.0, NEG_BIG))
        bias = jnp.stack(bias, axis=0)[:, None]
        s4 = sc.reshape(heads + (SLC_CHUNK,)) + bias
        m_old = m_ref[...].reshape(heads + (1,))
        m_new = jnp.maximum(m_old, jnp.max(s4, axis=-1, keepdims=True))
        alpha = jnp.exp(m_old - m_new)
        ex = jnp.exp(s4 - m_new)
        l_new = alpha * l_ref[...].reshape(heads + (1,)) + jnp.sum(ex, axis=-1, keepdims=True)
        pv = jnp.dot(ex.reshape(N_ROWS, SLC_CHUNK).astype(bf16), vch, preferred_element_type=f32)
        acc_ref[...] = alpha.reshape(N_ROWS, 1) * acc_ref[...] + pv
        l_ref[...] = l_new.reshape(N_ROWS, 1)
        m_ref[...] = m_new.reshape(N_ROWS, 1)
        return carry

    n_chunks = (i * Q_BLOCK + Q_BLOCK + SLC_CHUNK - 1) // SLC_CHUNK
    lax.fori_loop(0, n_chunks, slc_step, 0)
    o_slc = acc_ref[...] / l_ref[...]

    wstart = pl.multiple_of(jnp.maximum(i - WINDOW // Q_BLOCK, 0) * Q_BLOCK, Q_BLOCK)
    kwin = kw_ref[pl.ds(wstart, WIN_KEYS), :]
    vwin = vw_ref[pl.ds(wstart, WIN_KEYS), :]
    sw = _dot_t(qpad, kwin)
    widx = wstart + lax.broadcasted_iota(jnp.int32, (Q_BLOCK, WIN_KEYS), 1)
    wpos = i * Q_BLOCK + lax.broadcasted_iota(jnp.int32, (Q_BLOCK, WIN_KEYS), 0)
    okw = (widx <= wpos) & (wpos - widx < WINDOW)
    sw3 = sw.reshape(NSA_KV * NSA_REP, Q_BLOCK, WIN_KEYS) + jnp.where(okw, 0.0, NEG_BIG)[None]
    ew = jnp.exp(sw3 - jnp.max(sw3, axis=-1, keepdims=True))
    lw = jnp.sum(ew, axis=-1, keepdims=True).reshape(N_ROWS, 1)
    o_win = jnp.dot(ew.reshape(N_ROWS, WIN_KEYS).astype(bf16), vwin, preferred_element_type=f32) / lw

    gate = jax.nn.sigmoid(gate_ref[...])
    n_hd = NSA_KV * NSA_REP
    for r in range(NSA_REP):
        per_g = []
        for g in range(NSA_KV):
            h = g * NSA_REP + r
            rows = slice(h * Q_BLOCK, (h + 1) * Q_BLOCK)
            per_g.append(gate[:, h:h + 1] * o_cmp[rows]
                         + gate[:, n_hd + h:n_hd + h + 1] * o_slc[rows]
                         + gate[:, 2 * n_hd + h:2 * n_hd + h + 1] * o_win[rows])
        o_ref[:, r * KV_LANES:(r + 1) * KV_LANES] = jnp.where(low, per_g[0], per_g[1])


def _gate_block(rows):
    c = _SEG_DST["nsa_g"] // LANE
    assert _SEG_DST["nsa_g"] % LANE == 0
    return pl.BlockSpec((None, rows, LANE), lambda b, i, *_: (b, i, c))


def _nsa_prompt(q, proj3, rows, kvb):
    B, T, _ = q.shape
    assert T % SLC_CHUNK == 0 and T >= WIN_KEYS and T // SLC_BLOCK <= MAX_SLC_BLOCKS
    n_top = min(N_SEL, T // SLC_BLOCK)
    blk = lambda b, i: (b, i, 0)
    kv_col = lambda c: pl.BlockSpec((None, T, KV_LANES), lambda b, i: (b, 0, c))
    return pl.pallas_call(
        functools.partial(_nsa_prompt_kernel, seq_len=T, n_top=n_top),
        out_shape=jax.ShapeDtypeStruct((B, T, NSA_WIDTH), jnp.float32),
        grid=(B, T // Q_BLOCK),
        in_specs=[
            pl.BlockSpec((None, Q_BLOCK, NSA_WIDTH), blk),
            _gate_block(Q_BLOCK),
            kv_col(0), kv_col(1), kv_col(0), kv_col(1), kv_col(2), kv_col(3),
        ],
        out_specs=pl.BlockSpec((None, Q_BLOCK, NSA_WIDTH), blk),
        scratch_shapes=[
            pltpu.VMEM((LANE, KV_LANES), jnp.float32),
            pltpu.VMEM((LANE, KV_LANES), jnp.float32),
            pltpu.VMEM((N_ROWS, 1), jnp.float32),
            pltpu.VMEM((N_ROWS, 1), jnp.float32),
            pltpu.VMEM((N_ROWS, KV_LANES), jnp.float32),
        ],
        compiler_params=pltpu.CompilerParams(
            dimension_semantics=("parallel", "arbitrary"), vmem_limit_bytes=48 * 1024 * 1024),
        name="nsa_prompt",
    )(q, proj3, rows, rows, kvb, kvb, kvb, kvb)


PAGES_PER_STEP = 16
STASH_CHUNK = 1024
NEW_PAD = 128


def _qpad_rows(q, nq):
    low = lax.broadcasted_iota(jnp.int32, (nq, LANE), 1) < NSA_HD
    parts = []
    for g in range(NSA_KV):
        for r in range(NSA_REP):
            qr = q[:, r * KV_LANES:(r + 1) * KV_LANES]
            parts.append(jnp.where(low if g == 0 else ~low, qr, 0.0))
    return jnp.concatenate(parts, axis=0).astype(jnp.bfloat16)


def _nsa_sample_kernel(pt_ref, *refs, n_q, past_len, n_top):
    f32, bf16 = jnp.float32, jnp.bfloat16
    pages = refs[:PAGES_PER_STEP]
    (q_ref, gate_ref, ksn_ref, vsn_ref, win_ref, kwn_ref, vwn_ref, o_ref,
     kcb_ref, vcb_ref, sel_ref, ocmp_ref, m_ref, l_ref, acc_ref, stash_ref) = refs[PAGES_PER_STEP:]
    p = pl.program_id(1)
    n_steps = pl.num_programs(1) - 1
    last = n_steps - 1
    n_rows = NSA_KV * NSA_REP * n_q
    n_blk = past_len // SLC_BLOCK
    cmp_per_step = PAGES_PER_STEP * PAGE_SIZE // CMP_BLOCK
    keys_per_step = PAGES_PER_STEP * PAGE_SIZE
    heads = (NSA_KV, NSA_REP, n_q)
    qpad = _qpad_rows(q_ref[...], n_q)

    @pl.when(p < n_steps)
    def _():
        pair = 2 * KV_LANES
        means = [pg[:, :pair].reshape(PAGE_SIZE // CMP_BLOCK, CMP_BLOCK, pair).sum(axis=1) * (1.0 / CMP_BLOCK)
                 for pg in pages]
        means = jnp.concatenate(means, axis=0)
        row0 = pl.multiple_of(p * cmp_per_step, cmp_per_step)
        kcb_ref[pl.ds(row0, cmp_per_step), :] = means[:, :KV_LANES]
        vcb_ref[pl.ds(row0, cmp_per_step), :] = means[:, KV_LANES:]
        key0 = pl.multiple_of(p * keys_per_step, keys_per_step)
        for k, pg in enumerate(pages):
            stash_ref[pl.ds(key0 + k * PAGE_SIZE, PAGE_SIZE), :] = pg[:, pair:].astype(bf16)

    @pl.when(p == last)
    def _():
        halves = []
        for par in range(2):
            kc = kcb_ref[pl.ds(par, n_blk, stride=2), :].astype(bf16)
            halves.append(_dot_t(qpad, kc))
        mx = jnp.maximum(jnp.max(halves[0], axis=-1, keepdims=True), jnp.max(halves[1], axis=-1, keepdims=True))
        e0, e1 = jnp.exp(halves[0] - mx), jnp.exp(halves[1] - mx)
        inv = 1.0 / (jnp.sum(e0, axis=-1, keepdims=True) + jnp.sum(e1, axis=-1, keepdims=True))
        p0, p1 = e0 * inv, e1 * inv
        oc = jnp.zeros((n_rows, KV_LANES), f32)
        for par, pp in ((0, p0), (1, p1)):
            vc = vcb_ref[pl.ds(par, n_blk, stride=2), :].astype(bf16)
            oc = oc + jnp.dot(pp.astype(bf16), vc, preferred_element_type=f32)
        ocmp_ref[...] = oc
        imp = (p0 + p1).reshape(heads + (n_blk,)).sum(axis=1).reshape(NSA_KV * n_q, n_blk)
        lane = lax.broadcasted_iota(jnp.int32, imp.shape, 1)
        v = jnp.where(lane == 0, jnp.inf, imp)
        rank = jnp.zeros(imp.shape, f32)
        for j in range(n_blk):
            col = v[:, j:j + 1]
            tie = jnp.where(lane > j, 1.0, 0.0)
            rank = rank + jnp.where(col > v, 1.0, jnp.where(col == v, tie, 0.0))
        sel_ref[...] = jnp.where(rank < n_top - 1, 1.0, 0.0)
        m_ref[...] = jnp.full(m_ref.shape, NEG_BIG, f32)
        l_ref[...] = jnp.zeros(l_ref.shape, f32)
        acc_ref[...] = jnp.zeros(acc_ref.shape, f32)

    def online(s4, vals):
        m_old = m_ref[...].reshape(heads + (1,))
        m_new = jnp.maximum(m_old, jnp.max(s4, axis=-1, keepdims=True))
        alpha = jnp.exp(m_old - m_new)
        ex = jnp.exp(s4 - m_new)
        l_new = alpha * l_ref[...].reshape(heads + (1,)) + jnp.sum(ex, axis=-1, keepdims=True)
        pv = jnp.dot(ex.reshape(n_rows, s4.shape[-1]).astype(bf16), vals, preferred_element_type=f32)
        acc_ref[...] = alpha.reshape(n_rows, 1) * acc_ref[...] + pv
        l_ref[...] = l_new.reshape(n_rows, 1)
        m_ref[...] = m_new.reshape(n_rows, 1)

    @pl.when(p == n_steps)
    def _():
        sel = sel_ref[...].astype(bf16)

        def chunk(c, carry):
            key0 = pl.multiple_of(c * STASH_CHUNK, STASH_CHUNK)
            kv = stash_ref[pl.ds(key0, STASH_CHUNK), :]
            sc = _dot_t(qpad, kv[:, :KV_LANES])
            kblk = (key0 + lax.broadcasted_iota(jnp.int32, (n_blk, STASH_CHUNK), 1)) // SLC_BLOCK
            expand = jnp.where(kblk == lax.broadcasted_iota(jnp.int32, (n_blk, STASH_CHUNK), 0), 1.0, 0.0).astype(bf16)
            picked = jnp.dot(sel, expand, preferred_element_type=f32) > 0.5
            bias = jnp.where(picked, 0.0, NEG_BIG).reshape(NSA_KV, 1, n_q, STASH_CHUNK)
            online(sc.reshape(heads + (STASH_CHUNK,)) + bias, kv[:, KV_LANES:])
            return carry

        lax.fori_loop(0, past_len // STASH_CHUNK, chunk, 0)

        tq = lax.broadcasted_iota(jnp.int32, (n_q, NEW_PAD), 0)
        tk = lax.broadcasted_iota(jnp.int32, (n_q, NEW_PAD), 1)
        new_bias = jnp.where(tk <= tq, 0.0, NEG_BIG)[None, None]
        s_new = _dot_t(qpad, ksn_ref[...]).reshape(heads + (NEW_PAD,)) + new_bias
        online(s_new, vsn_ref[...])
        o_slc = acc_ref[...] / l_ref[...]

        win = win_ref[...]
        n_prev = win.shape[0]
        s_prev = _dot_t(qpad, win[:, :KV_LANES].astype(bf16))
        pk = lax.broadcasted_iota(jnp.int32, (n_q, n_prev), 1)
        pq = lax.broadcasted_iota(jnp.int32, (n_q, n_prev), 0)
        s_prev = s_prev.reshape(heads + (n_prev,)) + jnp.where(pk > pq + (n_prev - WINDOW), 0.0, NEG_BIG)[None, None]
        s_wnew = _dot_t(qpad, kwn_ref[...]).reshape(heads + (NEW_PAD,)) + new_bias
        mw = jnp.maximum(jnp.max(s_prev, axis=-1, keepdims=True), jnp.max(s_wnew, axis=-1, keepdims=True))
        e_prev, e_new = jnp.exp(s_prev - mw), jnp.exp(s_wnew - mw)
        lw = jnp.sum(e_prev, axis=-1, keepdims=True) + jnp.sum(e_new, axis=-1, keepdims=True)
        o_win = (jnp.dot(e_prev.reshape(n_rows, n_prev).astype(bf16), win[:, KV_LANES:].astype(bf16),
                         preferred_element_type=f32)
                 + jnp.dot(e_new.reshape(n_rows, NEW_PAD).astype(bf16), vwn_ref[...], preferred_element_type=f32))
        o_win = o_win / lw.reshape(n_rows, 1)

        gate = jax.nn.sigmoid(gate_ref[...])
        low = lax.broadcasted_iota(jnp.int32, (n_q, LANE), 1) < NSA_HD
        o_cmp = ocmp_ref[...]
        n_hd = NSA_KV * NSA_REP
        for r in range(NSA_REP):
            per_g = []
            for g in range(NSA_KV):
                h = g * NSA_REP + r
                rows = slice(h * n_q, (h + 1) * n_q)
                per_g.append(gate[:, h:h + 1] * o_cmp[rows]
                             + gate[:, n_hd + h:n_hd + h + 1] * o_slc[rows]
                             + gate[:, 2 * n_hd + h:2 * n_hd + h + 1] * o_win[rows])
            o_ref[:, r * KV_LANES:(r + 1) * KV_LANES] = jnp.where(low, per_g[0], per_g[1])


def _nsa_sample(page_table, cache, layer, q, proj3, kvb_new, win_prev):
    B, T, _ = q.shape
    n_pages = page_table.shape[1]
    past_len = n_pages * PAGE_SIZE
    n_prev = win_prev.shape[1]
    n_blk = past_len // SLC_BLOCK
    assert T % 8 == 0 and T < CMP_BLOCK and n_pages % PAGES_PER_STEP == 0 and n_blk % LANE == 0
    assert n_blk <= LANE and n_prev == WINDOW
    n_top = min(N_SEL, n_blk + 1)
    n_rows = NSA_KV * NSA_REP * T
    assert past_len % STASH_CHUNK == 0
    per_b = lambda b, p, pt: (b, 0, 0)
    n_steps = n_pages // PAGES_PER_STEP

    def page_spec(k):
        def index(b, p, pt):
            return (layer, pt[b, jnp.minimum(p, n_steps - 1) * PAGES_PER_STEP + k], 0, 0)
        return pl.BlockSpec((None, None, PAGE_SIZE, 4 * KV_LANES), index)

    new_col = lambda c: pl.BlockSpec((None, NEW_PAD, KV_LANES), lambda b, p, pt: (b, 0, c))
    gate_col = _SEG_DST["nsa_g"] // LANE
    grid_spec = pltpu.PrefetchScalarGridSpec(
        num_scalar_prefetch=1,
        grid=(B, n_steps + 1),
        in_specs=[page_spec(k) for k in range(PAGES_PER_STEP)] + [
            pl.BlockSpec((None, T, NSA_WIDTH), per_b),
            pl.BlockSpec((None, T, LANE), lambda b, p, pt: (b, 0, gate_col)),
            new_col(0), new_col(1),
            pl.BlockSpec((None, n_prev, 2 * KV_LANES), per_b),
            new_col(2), new_col(3),
        ],
        out_specs=pl.BlockSpec((None, T, NSA_WIDTH), per_b),
        scratch_shapes=[
            pltpu.VMEM((past_len // CMP_BLOCK, KV_LANES), jnp.float32),
            pltpu.VMEM((past_len // CMP_BLOCK, KV_LANES), jnp.float32),
            pltpu.VMEM((NSA_KV * T, n_blk), jnp.float32),
            pltpu.VMEM((n_rows, KV_LANES), jnp.float32),
            pltpu.VMEM((n_rows, 1), jnp.float32),
            pltpu.VMEM((n_rows, 1), jnp.float32),
            pltpu.VMEM((n_rows, KV_LANES), jnp.float32),
            pltpu.VMEM((past_len, 2 * KV_LANES), jnp.bfloat16),
        ],
    )
    return pl.pallas_call(
        functools.partial(_nsa_sample_kernel, n_q=T, past_len=past_len, n_top=n_top),
        out_shape=jax.ShapeDtypeStruct((B, T, NSA_WIDTH), jnp.float32),
        grid_spec=grid_spec,
        compiler_params=pltpu.CompilerParams(dimension_semantics=("parallel", "arbitrary")),
        name="nsa_sample",
    )(page_table, *([cache] * PAGES_PER_STEP), q, proj3, kvb_new, kvb_new, win_prev, kvb_new, kvb_new)


LRU_TILE = 512
SUBLANES = 8


EXPM1_SERIES_RANGE = 0.35
_EXPM1_COEFFS = tuple(1.0 / math.factorial(n) for n in range(9, 1, -1))


def _one_minus_exp(y):
    acc = jnp.full_like(y, _EXPM1_COEFFS[0])
    for c in _EXPM1_COEFFS[1:]:
        acc = acc * y + c
    series = -(y * (1.0 + y * acc))
    return jnp.where(y > -EXPM1_SERIES_RANGE, series, 1.0 - jnp.exp(y))


def _lru_kernel(x_ref, buf_ref, h0_ref, cw_ref, cb_ref, wa_ref, ba_ref, wx_ref, bx_ref, lsl_ref,
                hs_ref, hlast_ref, cbuf_ref, xe_ref, h_ref, *, tile):
    f32 = jnp.float32
    t = pl.program_id(1)
    tail = CONV_W - 1

    @pl.when(t == 0)
    def _():
        xe_ref[0:SUBLANES, :] = jnp.zeros((SUBLANES, D_RNN), f32)
        xe_ref[SUBLANES - tail:SUBLANES, :] = buf_ref[...]
        h_ref[...] = h0_ref[...]

    xe_ref[SUBLANES:SUBLANES + tile, :] = x_ref[...]
    cw = cw_ref[...]
    u = cb_ref[...]
    for j in range(CONV_W):
        u = u + cw[j:j + 1] * xe_ref[SUBLANES - tail + j:SUBLANES - tail + j + tile, :]

    @pl.when(t == pl.num_programs(1) - 1)
    def _():
        cbuf_ref[...] = xe_ref[SUBLANES + tile - tail:SUBLANES + tile, :]

    xe_ref[0:SUBLANES, :] = xe_ref[tile:tile + SUBLANES, :]

    ub = u.astype(jnp.bfloat16)
    r = jax.nn.sigmoid(jnp.dot(ub, wa_ref[...], preferred_element_type=f32) + ba_ref[...])
    i = jax.nn.sigmoid(jnp.dot(ub, wx_ref[...], preferred_element_type=f32) + bx_ref[...])
    log_a = LRU_C * r * lsl_ref[...]
    a = jnp.exp(log_a)
    b = jnp.sqrt(_one_minus_exp(2.0 * log_a)) * (i * u)

    row = lax.broadcasted_iota(jnp.int32, (SUBLANES, D_RNN), 0)
    h = h_ref[...]
    for k in range(tile // SUBLANES):
        ak = a[k * SUBLANES:(k + 1) * SUBLANES]
        bk = b[k * SUBLANES:(k + 1) * SUBLANES]
        for s in (1, 2, 4):
            a_prev = jnp.where(row >= s, pltpu.roll(ak, s, axis=0), 1.0)
            b_prev = jnp.where(row >= s, pltpu.roll(bk, s, axis=0), 0.0)
            bk = ak * b_prev + bk
            ak = ak * a_prev
        hk = ak * h + bk
        hs_ref[k * SUBLANES:(k + 1) * SUBLANES, :] = hk
        h = hk[SUBLANES - 1:SUBLANES]
    h_ref[...] = h
    hlast_ref[...] = h


def _block_diag(w):
    n, c, d = w.shape
    return jnp.einsum('ncd,nm->ncmd', w, jnp.eye(n, dtype=w.dtype)).reshape(n * c, n * d)


def _lru(proj, B, T, buf, h0, conv_w, conv_b, wa_bd, ba, wx_bd, bx, lam):
    tile = min(LRU_TILE, T)
    assert T % tile == 0 and tile % SUBLANES == 0 and _SEG_DST["lru_x"] == 0
    row = lambda a: a.reshape(1, D_RNN)
    per_b = lambda b, t: (b, 0, 0)
    const = lambda b, t: (0, 0)
    vec = pl.BlockSpec((1, D_RNN), const)
    mat = pl.BlockSpec((D_RNN, D_RNN), const)
    hs, hlast, cbuf = pl.pallas_call(
        functools.partial(_lru_kernel, tile=tile),
        out_shape=(jax.ShapeDtypeStruct((B, T, D_RNN), jnp.float32),
                   jax.ShapeDtypeStruct((B, 1, D_RNN), jnp.float32),
                   jax.ShapeDtypeStruct((B, CONV_W - 1, D_RNN), jnp.float32)),
        grid=(B, T // tile),
        in_specs=[
            pl.BlockSpec((None, tile, D_RNN), lambda b, t: (b, t, 0)),
            pl.BlockSpec((None, CONV_W - 1, D_RNN), per_b),
            pl.BlockSpec((None, 1, D_RNN), per_b),
            pl.BlockSpec((CONV_W, D_RNN), const),
            vec, mat, vec, mat, vec, vec,
        ],
        out_specs=(pl.BlockSpec((None, tile, D_RNN), lambda b, t: (b, t, 0)),
                   pl.BlockSpec((None, 1, D_RNN), per_b),
                   pl.BlockSpec((None, CONV_W - 1, D_RNN), per_b)),
        scratch_shapes=[pltpu.VMEM((tile + SUBLANES, D_RNN), jnp.float32),
                        pltpu.VMEM((1, D_RNN), jnp.float32)],
        compiler_params=pltpu.CompilerParams(dimension_semantics=("parallel", "arbitrary")),
        name="rg_lru",
    )(proj.reshape(B, T, PROJ_COLS), buf, h0.reshape(B, 1, D_RNN), conv_w, row(conv_b),
      wa_bd, row(ba), wx_bd, row(bx), row(jax.nn.log_sigmoid(lam)))
    return hs, hlast.reshape(B, D_RNN), cbuf


DN_CHUNKS_PER_STEP = 4


def _split2(a):
    hi = a.astype(jnp.bfloat16)
    return hi, (a - hi.astype(jnp.float32)).astype(jnp.bfloat16)


def _dot_hi(a, b):
    a1, a2 = _split2(a)
    b1, b2 = _split2(b)
    d = lambda x, y: jnp.dot(x, y, preferred_element_type=jnp.float32)
    return d(a1, b1) + (d(a1, b2) + d(a2, b1))


def _unit_lower_solve(n_mats, rhss):
    C, n = rhss[0].shape
    f32 = jnp.float32
    ci = lax.broadcasted_iota(jnp.int32, (C, C), 0)
    ei = lax.broadcasted_iota(jnp.int32, (C, C), 1)
    same = (ci // SUBLANES) == (ei // SUBLANES)
    eye = jnp.where(ci == ei, 1.0, 0.0)
    diags = [jnp.where(same, m, 0.0) for m in n_mats]
    offs = [jnp.where(same, 0.0, m) for m in n_mats]
    d2 = [_dot_hi(d, d) for d in diags]
    d4 = [_dot_hi(d, d) for d in d2]
    part = [_dot_hi(eye - d, eye + s) for d, s in zip(diags, d2)]
    dinv = [_dot_hi(p, eye + s) for p, s in zip(part, d4)]
    solved = [[] for _ in rhss]
    for i in range(C // SUBLANES):
        rows = slice(i * SUBLANES, (i + 1) * SUBLANES)
        ys = [r[rows] for r in rhss]
        if i:
            below = jnp.zeros((C - i * SUBLANES, n), f32)
            ys = [y - _dot_hi(off[rows], jnp.concatenate(done + [below], axis=0))
                  for y, off, done in zip(ys, offs, solved)]
        above = [jnp.zeros((i * SUBLANES, n), f32)] if i else []
        rest = [jnp.zeros((C - (i + 1) * SUBLANES, n), f32)] if (i + 1) * SUBLANES < C else []
        xs = [_dot_hi(inv[rows], jnp.concatenate(above + [y] + rest, axis=0)) for inv, y in zip(dinv, ys)]
        for done, x in zip(solved, xs):
            done.append(x)
    return [jnp.concatenate(done, axis=0) for done in solved]


def _dn_kernel(xq_ref, xk_ref, xv_ref, ab_ref, buf_ref, s0_ref, cw_ref, alog_ref, dtb_ref, onorm_ref,
               o_ref, s_out_ref, cbuf_ref, xe_ref, s_ref, *, n_in):
    f32, bf16 = jnp.float32, jnp.bfloat16
    C = DN_CHUNK
    n = pl.program_id(1)
    last = pl.num_programs(1) - 1
    tail = CONV_W - 1
    width = 3 * DN_WIDTH

    @pl.when(n == 0)
    def _():
        xe_ref[...] = jnp.zeros(xe_ref.shape, f32)
        xe_ref[SUBLANES - tail:SUBLANES, :] = buf_ref[...]
        s_ref[...] = s0_ref[...]

    xe_ref[SUBLANES:SUBLANES + n_in, 0:DN_WIDTH] = xq_ref[...]
    xe_ref[SUBLANES:SUBLANES + n_in, DN_WIDTH:2 * DN_WIDTH] = xk_ref[...]
    xe_ref[SUBLANES:SUBLANES + n_in, 2 * DN_WIDTH:width] = xv_ref[...]
    n_chunks = -(-n_in // C)
    R = n_chunks * C
    cw = cw_ref[...]
    y = jnp.zeros((R, width), f32)
    for j in range(CONV_W):
        y = y + cw[j:j + 1] * xe_ref[SUBLANES - tail + j:SUBLANES - tail + j + R, :]

    @pl.when(n == last)
    def _():
        cbuf_ref[...] = xe_ref[SUBLANES + n_in - tail:SUBLANES + n_in, :]

    xe_ref[0:SUBLANES, :] = xe_ref[n_in:n_in + SUBLANES, :]

    valid = lax.broadcasted_iota(jnp.int32, (R, 1), 0) < n_in
    y = jnp.where(valid, y * jax.nn.sigmoid(y), 0.0)
    ab = ab_ref[...]
    if n_in < R:
        ab = jnp.concatenate([ab, jnp.zeros((R - n_in, LANE), f32)], axis=0)
    g_all = jnp.where(valid, -jnp.exp(alog_ref[...]) * jax.nn.softplus(ab + dtb_ref[...]), 0.0)
    beta_all = jnp.where(valid, jax.nn.sigmoid(ab), 0.0)

    ci = lax.broadcasted_iota(jnp.int32, (C, C), 0)
    ei = lax.broadcasted_iota(jnp.int32, (C, C), 1)
    tri = jnp.where(ei <= ci, 1.0, 0.0).astype(bf16)
    csum = lambda x: jnp.dot(tri, x, preferred_element_type=f32)
    heads = range(DN_HEADS)
    mm = lambda a, b: jnp.dot(a, b, preferred_element_type=f32)
    col = lambda a, h: a[:, h:h + 1]

    q_dec, qks, k_dec, e_last, n_mats, rhss = [], [], [], [], [], []
    for c in range(n_chunks):
        rows = slice(c * C, (c + 1) * C)
        g_c = g_all[rows]
        g1 = g_c.astype(bf16)
        r1 = g_c - g1.astype(f32)
        g2 = r1.astype(bf16)
        g3 = (r1 - g2.astype(f32)).astype(bf16)
        G = csum(g1) + (csum(g2) + csum(g3))
        G_t = G.T
        g_last = G[C - 1:C, :]
        e_g = jnp.exp(G)
        e_rest = jnp.exp(g_last - G)
        e_last.append(jnp.exp(g_last))
        head_lanes = lambda base, h: y[rows, base + h * DN_HD:base + (h + 1) * DN_HD]
        qs = [head_lanes(0, h) for h in heads]
        ks = [head_lanes(DN_WIDTH, h) for h in heads]
        vs = [head_lanes(2 * DN_WIDTH, h) for h in heads]
        qs = [q * lax.rsqrt(jnp.sum(q * q, axis=-1, keepdims=True) + L2_EPS) * (DN_HD ** -0.5) for q in qs]
        ks = [k * lax.rsqrt(jnp.sum(k * k, axis=-1, keepdims=True) + L2_EPS) for k in ks]
        decays = [jnp.where(ei <= ci, jnp.exp(col(G, h) - G_t[h:h + 1, :]), 0.0) for h in heads]
        betas = [col(beta_all[rows], DN_HEADS + h) for h in heads]
        kbs = [k.astype(bf16) for k in ks]
        kks = [_dot_t(kb, kb) for kb in kbs]
        qk = [_dot_t(q.astype(bf16), kb) for q, kb in zip(qs, kbs)]
        n_mats += [jnp.where(ei < ci, b * kk * d, 0.0) for b, kk, d in zip(betas, kks, decays)]
        qks.append([(a * d).astype(bf16) for a, d in zip(qk, decays)])
        rhss += [jnp.concatenate([betas[h] * vs[h], betas[h] * ks[h] * col(e_g, h)], axis=1) for h in heads]
        q_dec.append([(qs[h] * col(e_g, h)).astype(bf16) for h in heads])
        k_dec.append([(ks[h] * col(e_rest, h)).astype(bf16) for h in heads])
    xs = _unit_lower_solve(n_mats, rhss)

    states = [s_ref[h] for h in heads]
    o_gain = onorm_ref[...]
    for c in range(n_chunks):
        xc = xs[c * DN_HEADS:(c + 1) * DN_HEADS]
        sbs = [S.astype(bf16) for S in states]
        v_news = [x[:, :DN_HD] - mm(x[:, DN_HD:].astype(bf16), sb) for x, sb in zip(xc, sbs)]
        vbs = [v.astype(bf16) for v in v_news]
        o_state = [mm(qd, sb) for qd, sb in zip(q_dec[c], sbs)]
        o_local = [mm(qk, vb) for qk, vb in zip(qks[c], vbs)]
        grow = [lax.dot_general(kd, vb, (((0,), (0,)), ((), ())), preferred_element_type=f32)
                for kd, vb in zip(k_dec[c], vbs)]
        states = [col(e_last[c], h) * states[h] + grow[h] for h in heads]
        n_out = min(C, n_in)
        for h in heads:
            o = o_state[h] + o_local[h]
            o = o * lax.rsqrt(jnp.mean(o * o, axis=-1, keepdims=True) + RMS_EPS) * o_gain
            o_ref[c * C:c * C + n_out, h * DN_HD:(h + 1) * DN_HD] = o[:n_out]
    for h in heads:
        s_ref[h] = states[h]

    @pl.when(n == last)
    def _():
        s_out_ref[...] = s_ref[...]


def _gated_delta_net(proj, B, T, buf, S0, conv_w, A_log, dt_bias, o_norm):
    n_in = min(DN_CHUNKS_PER_STEP * DN_CHUNK, T)
    assert T % n_in == 0 and n_in % SUBLANES == 0 and (n_in <= DN_CHUNK or n_in % DN_CHUNK == 0)
    rows_pad = -(-n_in // DN_CHUNK) * DN_CHUNK
    qkv0 = _SEG_DST["dn_qkv"] // DN_WIDTH
    assert _SEG_DST["dn_qkv"] % DN_WIDTH == 0 and _SEG_DST["dn_b"] == _SEG_DST["dn_a"] + DN_HEADS
    ab_blk = _SEG_DST["dn_a"] // LANE
    proj3 = proj.reshape(B, T, PROJ_COLS)
    col = lambda c: (lambda b, n: (b, n, c))
    per_b = lambda b, n: (b, 0, 0)
    const = lambda b, n: (0, 0)
    pad_row = lambda a, off: jnp.zeros((1, LANE), jnp.float32).at[0, off:off + DN_HEADS].set(a)
    o, s_out, cbuf = pl.pallas_call(
        functools.partial(_dn_kernel, n_in=n_in),
        out_shape=(jax.ShapeDtypeStruct((B, T, DN_WIDTH), jnp.float32),
                   jax.ShapeDtypeStruct((B, DN_HEADS, DN_HD, DN_HD), jnp.float32),
                   jax.ShapeDtypeStruct((B, CONV_W - 1, 3 * DN_WIDTH), jnp.float32)),
        grid=(B, T // n_in),
        in_specs=[
            pl.BlockSpec((None, n_in, DN_WIDTH), col(qkv0)),
            pl.BlockSpec((None, n_in, DN_WIDTH), col(qkv0 + 1)),
            pl.BlockSpec((None, n_in, DN_WIDTH), col(qkv0 + 2)),
            pl.BlockSpec((None, n_in, LANE), col(ab_blk)),
            pl.BlockSpec((None, CONV_W - 1, 3 * DN_WIDTH), per_b),
            pl.BlockSpec((None, DN_HEADS, DN_HD, DN_HD), lambda b, n: (b, 0, 0, 0)),
            pl.BlockSpec((CONV_W, 3 * DN_WIDTH), const),
            pl.BlockSpec((1, LANE), const),
            pl.BlockSpec((1, LANE), const),
            pl.BlockSpec((1, DN_HD), const),
        ],
        out_specs=(pl.BlockSpec((None, n_in, DN_WIDTH), lambda b, n: (b, n, 0)),
                   pl.BlockSpec((None, DN_HEADS, DN_HD, DN_HD), lambda b, n: (b, 0, 0, 0)),
                   pl.BlockSpec((None, CONV_W - 1, 3 * DN_WIDTH), per_b)),
        scratch_shapes=[pltpu.VMEM((rows_pad + 2 * SUBLANES, 3 * DN_WIDTH), jnp.float32),
                        pltpu.VMEM((DN_HEADS, DN_HD, DN_HD), jnp.float32)],
        compiler_params=pltpu.CompilerParams(dimension_semantics=("parallel", "arbitrary")),
        name="gated_delta",
    )(proj3, proj3, proj3, proj3, buf, S0, conv_w, pad_row(A_log, 0), pad_row(dt_bias, 0),
      o_norm.reshape(1, DN_HD))
    return o, s_out, cbuf


def _layer(x, paged, lru_h0, lru_buf, dn_S0, dn_buf,
           norm_g, w_in_pad, lru_conv_w, lru_conv_b, lru_wa, lru_ba, lru_wx, lru_bx, lru_lam,
           q_norm, k_norm, dn_conv_w, dn_A_log, dn_dt_bias, dn_o_norm,
           w_lru_out, w_nsa_out, w_dn_out, w_out):
    B, T, _ = x.shape
    P = 0 if paged is None else paged[0].shape[1] * PAGE_SIZE
    pos = P + jnp.arange(T, dtype=jnp.int32)
    x2d = x.reshape(B * T, D_MODEL)
    proj = _in_proj(x2d, norm_g, w_in_pad)
    proj3 = proj.reshape(B, T, PROJ_COLS)

    lru_seq, lru_h, lru_buf_new = _lru(proj, B, T, lru_buf, lru_h0, lru_conv_w, lru_conv_b,
                                       lru_wa, lru_ba, lru_wx, lru_bx, lru_lam)

    q, rows, win_rows, kvb = _nsa_prep(proj3, pos, q_norm, k_norm)
    if paged is None:
        o_nsa = _nsa_prompt(q, proj3, rows, kvb)
        win_state = win_rows[:, -min(WINDOW, T):]
    else:
        page_table, cache, layer, win_prev = paged
        win_prev = win_prev.reshape(B, win_prev.shape[1], 2 * KV_LANES)
        kvb_new = jnp.pad(kvb, ((0, 0), (0, NEW_PAD - T), (0, 0)))
        o_nsa = _nsa_sample(page_table, cache, layer, q, proj3, kvb_new, win_prev)
        win_state = jnp.concatenate([win_prev, win_rows], axis=1)[:, -WINDOW:]
    rows = rows.reshape(B, T, 4, NSA_KV, NSA_HD)
    win_state = win_state.reshape(B, win_state.shape[1], 2, NSA_KV, NSA_HD)

    o_dn, dn_S, dn_buf_new = _gated_delta_net(proj, B, T, dn_buf, dn_S0, dn_conv_w, dn_A_log, dn_dt_bias,
                                              dn_o_norm)

    y2d = _out_stage(x2d, norm_g, w_in_pad, proj, lru_seq.reshape(B * T, D_RNN), o_nsa.reshape(B * T, NSA_WIDTH),
                     o_dn.reshape(B * T, DN_WIDTH), w_lru_out, w_nsa_out, w_dn_out, w_out)
    return y2d.reshape(B, T, D_MODEL), (rows, win_state, lru_h, lru_buf_new, dn_S, dn_buf_new)


def kernel(x_prompt, x_sample, cache_nsa_kv, page_table, state_nsa_win, state_lru_h, state_lru_conv,
           state_dn_S, state_dn_conv, norm_gain, w_in, lru_conv_w, lru_conv_b, lru_wa, lru_ba,
           lru_wx, lru_bx, lru_lambda, nsa_q_norm, nsa_k_norm, dn_conv_w, dn_A_log, dn_dt_bias,
           dn_o_norm, w_lru_out, w_nsa_out, w_dn_out, w_out):
    Bp = x_prompt.shape[0]
    dt = x_prompt.dtype
    bf16 = jnp.bfloat16
    cache = cache_nsa_kv.reshape(cache_nsa_kv.shape[:2] + (PAGE_SIZE, 4 * KV_LANES))
    p_h0 = jnp.zeros((Bp, D_RNN), jnp.float32)
    p_lbuf0 = jnp.zeros((Bp, CONV_W - 1, D_RNN), dt)
    p_S0 = jnp.zeros((Bp, DN_HEADS, DN_HD, DN_HD), jnp.float32)
    p_dbuf0 = jnp.zeros((Bp, CONV_W - 1, 3 * DN_WIDTH), dt)
    wa_bd = jax.vmap(_block_diag)(lru_wa).astype(bf16)
    wx_bd = jax.vmap(_block_diag)(lru_wx).astype(bf16)
    weights = (norm_gain, _pad_in_weight(w_in), lru_conv_w, lru_conv_b, wa_bd, lru_ba, wx_bd, lru_bx,
               lru_lambda, nsa_q_norm, nsa_k_norm, dn_conv_w, dn_A_log, dn_dt_bias, dn_o_norm,
               w_lru_out.astype(bf16), _heads_rg(w_nsa_out.swapaxes(1, 2)).swapaxes(1, 2).astype(bf16),
               w_dn_out.astype(bf16), w_out.astype(bf16))
    xp, xs = x_prompt, x_sample
    st_p, st_s = [], []
    for l in range(DEPTH):
        lw = [w[l] for w in weights]
        xp, sp = _layer(xp, None, p_h0, p_lbuf0, p_S0, p_dbuf0, *lw)
        xs, ss = _layer(xs, (page_table, cache, l, state_nsa_win[l]), state_lru_h[l], state_lru_conv[l],
                        state_dn_S[l], state_dn_conv[l], *lw)
        st_p.append(sp)
        st_s.append(ss)
    kv_p, win_p, lh_p, lc_p, S_p, dc_p = [jnp.stack(a) for a in zip(*st_p)]
    kv_s, win_s, lh_s, lc_s, S_s, dc_s = [jnp.stack(a) for a in zip(*st_s)]
    return (xp, xs, kv_p, kv_s, win_p, win_s, lh_p, lh_s, lc_p, lc_s, S_p, S_s, dc_p, dc_s)
```
